```python
import math
import jax
import jax.numpy as jnp
from jax import lax
import numpy as np

D_MODEL = 1024
BATCH = 4
SEQ = 8192
DEPTH = 2

GRID_W = 64
CTX_LEN = 256
NORM_EPS = 1e-6
N_BRANCHES = 4

S5_WIDTH = 256
S5_GROUP_SIZE = 16
S5_GROUPS = S5_WIDTH // S5_GROUP_SIZE
S5_STATE = 64
S5_DT_MIN = 1e-3
S5_DT_MAX = 1e-1

GLA_HEADS = 4
GLA_DK = 64
GLA_DV = 128
GLA_GATE_RANK = 16
GLA_TAU = 16.0
GLA_CHUNK = 64
ROPE_BASE = 10000.0

NA_HEADS = 4
NA_HEAD_DIM = 64
NA_WIN_ROWS = 8
NA_WIN_COLS = 16

CONV_WIDTH = 256
CONV_KERNEL = 31

MOE_GROUPS = 4
MOE_EXPERTS_PER_GROUP = 8
MOE_TOP_K = 2
MOE_HIDDEN = 256

IN_PARTS = (
    ('s5_u', S5_WIDTH),
    ('gla_q', GLA_HEADS * GLA_DK),
    ('gla_k', GLA_HEADS * GLA_DK),
    ('gla_v', GLA_HEADS * GLA_DV),
    ('gla_r', GLA_HEADS * GLA_DV),
    ('gla_a', 2 * GLA_GATE_RANK),
    ('na_q', NA_HEADS * NA_HEAD_DIM),
    ('na_k', NA_HEADS * NA_HEAD_DIM),
    ('na_v', NA_HEADS * NA_HEAD_DIM),
    ('conv_in', 2 * CONV_WIDTH),
    ('gates', N_BRANCHES * D_MODEL),
)
IN_WIDTH = sum(width for _, width in IN_PARTS)
ALL_PARTS = tuple(name for name, _ in IN_PARTS)
CTX_STATE_PARTS = ('s5_u', 'gla_k', 'gla_v', 'gla_a', 'na_k', 'na_v')

kernel_name = 'hybrid_prefix_ssm_gla_natten_conv_hmoe'


def _orient(a, reverse):
    return jnp.flip(a, axis=1) if reverse else a


def rms_norm(x, g):
    xf = x.astype(jnp.float32)
    y = xf * lax.rsqrt(jnp.mean(xf * xf, axis=-1, keepdims=True) + NORM_EPS)
    return (y * g.astype(jnp.float32)).astype(x.dtype)


def layer_norm(x, g, b):
    xf = x.astype(jnp.float32)
    xc = xf - jnp.mean(xf, axis=-1, keepdims=True)
    y = xc * lax.rsqrt(jnp.mean(xc * xc, axis=-1, keepdims=True) + NORM_EPS)
    return (y * g.astype(jnp.float32) + b.astype(jnp.float32)).astype(x.dtype)


def modulate(x, g, shift, scale):
    return rms_norm(x, g) * (1 + scale) + shift


def in_proj(h, w_in, names):
    offsets = {}
    start = 0
    for name, width in IN_PARTS:
        offsets[name] = (start, width)
        start += width
    if tuple(names) == ALL_PARTS:
        w = w_in
    else:
        w = jnp.concatenate([w_in[:, offsets[n][0]:offsets[n][0] + offsets[n][1]] for n in names], axis=1)
    y = h @ w
    out = {}
    col = 0
    for n in names:
        width = offsets[n][1]
        out[n] = y[..., col:col + width]
        col += width
    return out


def s5_discretise(lam_re, lam_im, log_dt, b_re, b_im):
    lr = lam_re.astype(jnp.float32)
    li = lam_im.astype(jnp.float32)
    dt = jnp.exp(log_dt.astype(jnp.float32))[:, None]
    mag = jnp.exp(lr * dt)
    abar_re = mag * jnp.cos(li * dt)
    abar_im = mag * jnp.sin(li * dt)
    den = lr * lr + li * li
    nr = abar_re - 1.0
    ni = abar_im
    coef_re = (nr * lr + ni * li) / den
    coef_im = (ni * lr - nr * li) / den
    br = b_re.astype(jnp.float32)
    bi = b_im.astype(jnp.float32)
    bbar_re = coef_re[..., None] * br - coef_im[..., None] * bi
    bbar_im = coef_re[..., None] * bi + coef_im[..., None] * br
    return abar_re, abar_im, bbar_re, bbar_im


def _ssm_combine(e1, e2):
    a1r, a1i, b1r, b1i = e1
    a2r, a2i, b2r, b2i = e2
    ar = a1r * a2r - a1i * a2i
    ai = a1r * a2i + a1i * a2r
    br = a2r * b1r - a2i * b1i + b2r
    bi = a2r * b1i + a2i * b1r + b2i
    return ar, ai, br, bi


def s5_states(u, abar_re, abar_im, bbar_re, bbar_im, h0_re, h0_im):
    bu_re = jnp.einsum('blgh,gph->blgp', u, bbar_re)
    bu_im = jnp.einsum('blgh,gph->blgp', u, bbar_im)
    if h0_re is not None:
        bu_re = bu_re.at[:, 0].add(abar_re * h0_re - abar_im * h0_im)
        bu_im = bu_im.at[:, 0].add(abar_re * h0_im + abar_im * h0_re)
    length = u.shape[1]
    a_re = jnp.broadcast_to(abar_re, (1, length) + abar_re.shape)
    a_im = jnp.broadcast_to(abar_im, (1, length) + abar_im.shape)
    _, _, h_re, h_im = lax.associative_scan(_ssm_combine, (a_re, a_im, bu_re, bu_im), axis=1)
    return h_re, h_im


def s5_readout(h_re, h_im, c_re, c_im):
    return jnp.einsum('blgp,ghp->blgh', h_re, c_re) - jnp.einsum('blgp,ghp->blgh', h_im, c_im)


def s5_output(y, lp):
    z = jax.nn.gelu(y)
    z = z * jax.nn.sigmoid(z @ lp['s5_w_glu'].astype(jnp.float32) + lp['s5_b_glu'].astype(jnp.float32))
    return z.astype(lp['s5_w_out'].dtype) @ lp['s5_w_out']


def s5_mixer(u, uc, lp, ctx_out):
    bsz, length, _ = u.shape
    lc = uc.shape[1]
    uf = u.astype(jnp.float32)
    ucf = uc.astype(jnp.float32)
    ug = uf.reshape(bsz, length, S5_GROUPS, S5_GROUP_SIZE)
    ucg = ucf.reshape(bsz, lc, S5_GROUPS, S5_GROUP_SIZE)
    d_skip = lp['s5_d'].astype(jnp.float32)
    y = uf * d_skip
    yc = ucf * d_skip if ctx_out else None
    for direction in range(2):
        rev = direction == 1
        abr, abi, bbr, bbi = s5_discretise(lp['s5_lam_re'][direction], lp['s5_lam_im'][direction],
                                           lp['s5_log_dt'][direction], lp['s5_b_re'][direction],
                                           lp['s5_b_im'][direction])
        c_re = lp['s5_c_re'][direction].astype(jnp.float32)
        c_im = lp['s5_c_im'][direction].astype(jnp.float32)
        hcr, hci = s5_states(_orient(ucg, rev), abr, abi, bbr, bbi, None, None)
        hr, hi = s5_states(_orient(ug, rev), abr, abi, bbr, bbi, hcr[:, -1], hci[:, -1])
        y = y + _orient(s5_readout(hr, hi, c_re, c_im), rev).reshape(bsz, length, S5_WIDTH)
        if ctx_out:
            yc = yc + _orient(s5_readout(hcr, hci, c_re, c_im), rev).reshape(bsz, lc, S5_WIDTH)
    out = s5_output(y, lp)
    outc = s5_output(yc, lp) if ctx_out else None
    return out, outc


def axial_rope_angles(length, dim):
    t = jnp.arange(length, dtype=jnp.int32)
    row = (t // GRID_W).astype(jnp.float32)
    col = (t % GRID_W).astype(jnp.float32)
    half = dim // 2
    inv_freq = ROPE_BASE ** (-jnp.arange(0, half, 2, dtype=jnp.float32) / half)
    return row[:, None] * inv_freq, col[:, None] * inv_freq


def _rotate(x, ang):
    f = x.shape[-1] // 2
    cos = jnp.cos(ang)[:, None, :]
    sin = jnp.sin(ang)[:, None, :]
    x1 = x[..., :f]
    x2 = x[..., f:]
    return jnp.concatenate([x1 * cos - x2 * sin, x2 * cos + x1 * sin], axis=-1)


def apply_axial_rope(x, ang_row, ang_col):
    half = x.shape[-1] // 2
    return jnp.concatenate([_rotate(x[..., :half], ang_row), _rotate(x[..., half:], ang_col)], axis=-1)


def gla_log_gates(a_low, w_a2, b_a):
    gates = []
    for direction in range(2):
        z = a_low[..., direction * GLA_GATE_RANK:(direction + 1) * GLA_GATE_RANK] @ w_a2[direction] + b_a[direction]
        g = jax.nn.log_sigmoid(z.astype(jnp.float32)) / GLA_TAU
        gates.append(g.reshape(g.shape[0], g.shape[1], GLA_HEADS, GLA_DK))
    return gates


def gla_chunk_states(k, v, g, s0):
    bsz, length, nh, dk = k.shape
    n = length // GLA_CHUNK
    kc = k.reshape(bsz, n, GLA_CHUNK, nh, dk)
    vc = v.reshape(bsz, n, GLA_CHUNK, nh, v.shape[-1])
    b = jnp.cumsum(g.reshape(bsz, n, GLA_CHUNK, nh, dk), axis=2)
    b_last = b[:, :, -1]
    k_end = kc * jnp.exp(b_last[:, :, None] - b)
    kv = jnp.einsum('bnchd,bnche->bnhde', k_end, vc)
    decay = jnp.exp(b_last)

    def step(s, inp):
        dec, kv_n = inp
        return dec[..., None] * s + kv_n, s

    s_final, s_before = lax.scan(step, s0, (jnp.moveaxis(decay, 1, 0), jnp.moveaxis(kv, 1, 0)))
    return jnp.moveaxis(s_before, 0, 1), s_final, b


def gla_chunk_output(q, k, v, b, s_before):
    bsz, length, nh, dk = q.shape
    n = length // GLA_CHUNK
    dv = v.shape[-1]
    qc = q.reshape(bsz, n, GLA_CHUNK, nh, dk)
    kc = k.reshape(bsz, n, GLA_CHUNK, nh, dk)
    vc = v.reshape(bsz, n, GLA_CHUNK, nh, dv)
    q_t = qc * jnp.exp(b)
    k_t = kc * jnp.exp(-b)
    att = jnp.einsum('bnihd,bnjhd->bnhij', q_t, k_t)
    mask = jnp.tril(jnp.ones((GLA_CHUNK, GLA_CHUNK), dtype=bool))
    att = jnp.where(mask, att, 0.0)
    o = jnp.einsum('bnhij,bnjhe->bnihe', att, vc) + jnp.einsum('bnihd,bnhde->bnihe', q_t, s_before)
    return o.reshape(bsz, length, nh, dv)


def gla_readout(o, r, norm_g, w_out):
    of = o * lax.rsqrt(jnp.mean(o * o, axis=-1, keepdims=True) + NORM_EPS) * norm_g.astype(jnp.float32)
    rf = jax.nn.silu(r.astype(jnp.float32)).reshape(o.shape)
    y = (of * rf).reshape(o.shape[0], o.shape[1], GLA_HEADS * GLA_DV)
    return y.astype(w_out.dtype) @ w_out


def gla_mixer(p, pc, lp, ctx_out):
    bsz, length, _ = p['gla_k'].shape

    def heads(a, dh):
        return a.astype(jnp.float32).reshape(a.shape[0], a.shape[1], GLA_HEADS, dh)

    scale = GLA_DK ** -0.5
    ang_r, ang_c = axial_rope_angles(length, GLA_DK)
    q = apply_axial_rope(heads(p['gla_q'], GLA_DK), ang_r, ang_c) * scale
    k = apply_axial_rope(heads(p['gla_k'], GLA_DK), ang_r, ang_c)
    v = heads(p['gla_v'], GLA_DV)
    kc = heads(pc['gla_k'], GLA_DK)
    vc = heads(pc['gla_v'], GLA_DV)
    qc = heads(pc['gla_q'], GLA_DK) * scale if ctx_out else None
    g = gla_log_gates(p['gla_a'], lp['gla_w_a2'], lp['gla_b_a'])
    gc = gla_log_gates(pc['gla_a'], lp['gla_w_a2'], lp['gla_b_a'])
    s0 = jnp.zeros((bsz, GLA_HEADS, GLA_DK, GLA_DV), jnp.float32)
    o = jnp.zeros_like(v)
    oc = jnp.zeros_like(vc) if ctx_out else None
    for direction in range(2):
        rev = direction == 1
        kcd, vcd, gcd = _orient(kc, rev), _orient(vc, rev), _orient(gc[direction], rev)
        sb_c, sf_c, b_c = gla_chunk_states(kcd, vcd, gcd, s0)
        if ctx_out:
            oc = oc + _orient(gla_chunk_output(_orient(qc, rev), kcd, vcd, b_c, sb_c), rev)
        kd, vd, gd = _orient(k, rev), _orient(v, rev), _orient(g[direction], rev)
        sb, _, b = gla_chunk_states(kd, vd, gd, sf_c)
        o = o + _orient(gla_chunk_output(_orient(q, rev), kd, vd, b, sb), rev)
    out = gla_readout(o, p['gla_r'], lp['gla_norm_g'], lp['gla_w_out'])
    outc = gla_readout(oc, pc['gla_r'], lp['gla_norm_g'], lp['gla_w_out']) if ctx_out else None
    return out, outc


def na_mixer(p, pc, lp, ctx_out):
    bsz, length, _ = p['na_q'].shape
    lc = pc['na_k'].shape[1]
    rows = length // GRID_W
    wr = min(NA_WIN_ROWS, rows)
    scale = NA_HEAD_DIM ** -0.5
    grid = (bsz, rows, GRID_W, NA_HEADS, NA_HEAD_DIM)
    q = p['na_q'].astype(jnp.float32).reshape(grid) * scale
    k = p['na_k'].astype(jnp.float32).reshape(grid)
    v = p['na_v'].astype(jnp.float32).reshape(grid)
    kc = pc['na_k'].astype(jnp.float32).reshape(bsz, lc, NA_HEADS, NA_HEAD_DIM)
    vc = pc['na_v'].astype(jnp.float32).reshape(bsz, lc, NA_HEADS, NA_HEAD_DIM)
    r_idx = jnp.arange(rows)
    c_idx = jnp.arange(GRID_W)
    row_start = jnp.clip(r_idx - wr // 2, 0, rows - wr)
    col_start = jnp.clip(c_idx - NA_WIN_COLS // 2, 0, GRID_W - NA_WIN_COLS)
    col_in = (c_idx[None, :] >= col_start[:, None]) & (c_idx[None, :] < col_start[:, None] + NA_WIN_COLS)
    dc_idx = jnp.clip(c_idx[None, :] - c_idx[:, None] + NA_WIN_COLS - 1, 0, 2 * NA_WIN_COLS - 2)
    rpb = lp['na_rpb'].astype(jnp.float32)
    scores = []
    key_rows = []
    for off in range(wr):
        kr = row_start + off
        dr_idx = kr - r_idx + NA_WIN_ROWS - 1
        bias = rpb[:, dr_idx][:, :, dc_idx]
        s = jnp.einsum('brqhd,brkhd->bhrqk', q, k[:, kr]) + bias[None]
        scores.append(jnp.where(col_in, s, -jnp.inf))
        key_rows.append(kr)
    s_lat = jnp.stack(scores, axis=4)
    s_ctx = jnp.einsum('brqhd,bchd->bhrqc', q, kc)
    m = jnp.maximum(jnp.max(s_lat, axis=(4, 5)), jnp.max(s_ctx, axis=-1))
    p_lat = jnp.exp(s_lat - m[..., None, None])
    p_ctx = jnp.exp(s_ctx - m[..., None])
    denom = jnp.sum(p_lat, axis=(4, 5)) + jnp.sum(p_ctx, axis=-1)
    o = jnp.einsum('bhrqc,bchd->brqhd', p_ctx, vc)
    for off, kr in enumerate(key_rows):
        o = o + jnp.einsum('bhrqk,brkhd->brqhd', p_lat[:, :, :, :, off], v[:, kr])
    o = o / jnp.transpose(denom, (0, 2, 3, 1))[..., None]
    w_out = lp['na_w_out']
    out = o.reshape(bsz, length, NA_HEADS * NA_HEAD_DIM).astype(w_out.dtype) @ w_out
    if not ctx_out:
        return out, None
    qc = pc['na_q'].astype(jnp.float32).reshape(bsz, lc, NA_HEADS, NA_HEAD_DIM) * scale
    pcw = jax.nn.softmax(jnp.einsum('bqhd,bkhd->bhqk', qc, kc), axis=-1)
    oc = jnp.einsum('bhqk,bkhd->bqhd', pcw, vc).reshape(bsz, lc, NA_HEADS * NA_HEAD_DIM)
    return out, oc.astype(w_out.dtype) @ w_out


def conformer_conv(a, dw, dw_b, ln_g, ln_b, w_out):
    val, gate = jnp.split(a, 2, axis=-1)
    y = (val * jax.nn.sigmoid(gate)).astype(dw.dtype)
    y = lax.conv_general_dilated(y, dw[:, None, :], window_strides=(1,),
                                 padding=((CONV_KERNEL // 2, CONV_KERNEL // 2),),
                                 dimension_numbers=('NWC', 'WIO', 'NWC'),
                                 feature_group_count=CONV_WIDTH) + dw_b
    y = jax.nn.silu(layer_norm(y, ln_g, ln_b))
    return y @ w_out


def merge_branches(gate_pre, gate_b, branches, w_out):
    g = jax.nn.sigmoid(gate_pre + gate_b).reshape(gate_pre.shape[:-1] + (N_BRANCHES, D_MODEL))
    merged = g[..., 0, :] * branches[0]
    for i in range(1, N_BRANCHES):
        merged = merged + g[..., i, :] * branches[i]
    return merged @ w_out


def token_mixer(h, hc, lp, ctx_out):
    p = in_proj(h, lp['w_in'], ALL_PARTS)
    pc = in_proj(hc, lp['w_in'], ALL_PARTS if ctx_out else CTX_STATE_PARTS)
    y_s5, yc_s5 = s5_mixer(p['s5_u'], pc['s5_u'], lp, ctx_out)
    y_gla, yc_gla = gla_mixer(p, pc, lp, ctx_out)
    y_na, yc_na = na_mixer(p, pc, lp, ctx_out)
    conv_args = (lp['conv_dw'], lp['conv_dw_b'], lp['conv_ln_g'], lp['conv_ln_b'], lp['conv_w_out'])
    y_cv = conformer_conv(p['conv_in'], *conv_args)
    out = merge_branches(p['gates'], lp['gate_b'], (y_s5, y_gla, y_na, y_cv), lp['w_mix_out'])
    if not ctx_out:
        return out, None
    yc_cv = conformer_conv(pc['conv_in'], *conv_args)
    outc = merge_branches(pc['gates'], lp['gate_b'], (yc_s5, yc_gla, yc_na, yc_cv), lp['w_mix_out'])
    return out, outc


def hierarchical_moe(h, w_group, b_group, w_expert, b_expert, w1, w3, w2):
    shape = h.shape
    t = h.reshape(-1, shape[-1])
    n_tok = t.shape[0]
    group_prob = jax.nn.softmax((t @ w_group + b_group).astype(jnp.float32), axis=-1)
    group_p, group_idx = lax.top_k(group_prob, 1)
    exp_logits = (t @ w_expert + b_expert).astype(jnp.float32).reshape(n_tok, MOE_GROUPS, MOE_EXPERTS_PER_GROUP)
    sel_logits = jnp.take_along_axis(exp_logits, group_idx[:, :, None], axis=1)[:, 0]
    top_val, top_idx = lax.top_k(sel_logits, MOE_TOP_K)
    top_w = jax.nn.softmax(top_val, axis=-1) * group_p
    slot_w = jnp.sum(top_w[..., None] * jax.nn.one_hot(top_idx, MOE_EXPERTS_PER_GROUP, dtype=jnp.float32), axis=1)
    combine = (jax.nn.one_hot(group_idx[:, 0], MOE_GROUPS, dtype=jnp.float32)[:, :, None]
               * slot_w[:, None, :]).astype(t.dtype)
    y = jnp.zeros_like(t)
    for gi in range(MOE_GROUPS):
        a = jnp.einsum('td,edf->tef', t, w1[gi])
        b = jnp.einsum('td,edf->tef', t, w3[gi])
        act = jax.nn.silu(a) * b * combine[:, gi, :, None]
        y = y + jnp.einsum('tef,efd->td', act, w2[gi])
    return y.reshape(shape)


def setup_inputs(seed: int = 0) -> dict:
    key = jax.random.key(seed)
    keys = iter(jax.random.split(key, 64))

    def nrm(shape, scale):
        return jax.random.normal(next(keys), shape, jnp.float32) * scale

    def gain(shape):
        return 1.0 + nrm(shape, 0.05)

    G, P, H = S5_GROUPS, S5_STATE, S5_GROUP_SIZE
    n_exp = MOE_GROUPS * MOE_EXPERTS_PER_GROUP
    lam_im_base = jnp.pi * jnp.arange(P, dtype=jnp.float32)
    return {
        'x': nrm((BATCH, SEQ, D_MODEL), 1.0),
        'c': nrm((BATCH, D_MODEL), 1.0),
        'ctx': nrm((BATCH, CTX_LEN, D_MODEL), 1.0),
        'c_ctx': nrm((D_MODEL,), 1.0),
        'norm1_g': gain((DEPTH, D_MODEL)),
        'norm2_g': gain((DEPTH, D_MODEL)),
        'w_mod': nrm((DEPTH, D_MODEL, 6 * D_MODEL), 0.5 * D_MODEL ** -0.5),
        'b_mod': nrm((DEPTH, 6 * D_MODEL), 0.02),
        'w_in': nrm((DEPTH, D_MODEL, IN_WIDTH), D_MODEL ** -0.5),
        'gate_b': nrm((DEPTH, N_BRANCHES * D_MODEL), 0.1),
        'w_mix_out': nrm((DEPTH, D_MODEL, D_MODEL), D_MODEL ** -0.5),
        's5_lam_re': -0.5 + nrm((DEPTH, 2, G, P), 0.01),
        's5_lam_im': lam_im_base + nrm((DEPTH, 2, G, P), 0.01),
        's5_log_dt': jax.random.uniform(next(keys), (DEPTH, 2, G), jnp.float32,
                                        minval=math.log(S5_DT_MIN), maxval=math.log(S5_DT_MAX)),
        's5_b_re': nrm((DEPTH, 2, G, P, H), (2 * H) ** -0.5),
        's5_b_im': nrm((DEPTH, 2, G, P, H), (2 * H) ** -0.5),
        's5_c_re': nrm((DEPTH, 2, G, H, P), 0.5),
        's5_c_im': nrm((DEPTH, 2, G, H, P), 0.5),
        's5_d': nrm((DEPTH, S5_WIDTH), 0.5),
        's5_w_glu': nrm((DEPTH, S5_WIDTH, S5_WIDTH), S5_WIDTH ** -0.5),
        's5_b_glu': nrm((DEPTH, S5_WIDTH), 0.02),
        's5_w_out': nrm((DEPTH, S5_WIDTH, D_MODEL), S5_WIDTH ** -0.5),
        'gla_w_a2': nrm((DEPTH, 2, GLA_GATE_RANK, GLA_HEADS * GLA_DK), GLA_GATE_RANK ** -0.5),
        'gla_b_a': nrm((DEPTH, 2, GLA_HEADS * GLA_DK), 0.1),
        'gla_norm_g': gain((DEPTH, GLA_DV)),
        'gla_w_out': nrm((DEPTH, GLA_HEADS * GLA_DV, D_MODEL), (GLA_HEADS * GLA_DV) ** -0.5),
        'na_rpb': nrm((DEPTH, NA_HEADS, 2 * NA_WIN_ROWS - 1, 2 * NA_WIN_COLS - 1), 0.1),
        'na_w_out': nrm((DEPTH, NA_HEADS * NA_HEAD_DIM, D_MODEL), (NA_HEADS * NA_HEAD_DIM) ** -0.5),
        'conv_dw': nrm((DEPTH, CONV_KERNEL, CONV_WIDTH), CONV_KERNEL ** -0.5),
        'conv_dw_b': nrm((DEPTH, CONV_WIDTH), 0.02),
        'conv_ln_g': gain((DEPTH, CONV_WIDTH)),
        'conv_ln_b': nrm((DEPTH, CONV_WIDTH), 0.02),
        'conv_w_out': nrm((DEPTH, CONV_WIDTH, D_MODEL), CONV_WIDTH ** -0.5),
        'moe_w_group': nrm((DEPTH, D_MODEL, MOE_GROUPS), D_MODEL ** -0.5),
        'moe_b_group': nrm((DEPTH, MOE_GROUPS), 0.01),
        'moe_w_expert': nrm((DEPTH, D_MODEL, n_exp), D_MODEL ** -0.5),
        'moe_b_expert': nrm((DEPTH, n_exp), 0.01),
        'moe_w1': nrm((DEPTH, MOE_GROUPS, MOE_EXPERTS_PER_GROUP, D_MODEL, MOE_HIDDEN), D_MODEL ** -0.5),
        'moe_w3': nrm((DEPTH, MOE_GROUPS, MOE_EXPERTS_PER_GROUP, D_MODEL, MOE_HIDDEN), D_MODEL ** -0.5),
        'moe_w2': nrm((DEPTH, MOE_GROUPS, MOE_EXPERTS_PER_GROUP, MOE_HIDDEN, D_MODEL), MOE_HIDDEN ** -0.5),
        'final_norm_g': gain((D_MODEL,)),
    }


def reference(x, c, ctx, c_ctx, norm1_g, norm2_g, w_mod, b_mod, w_in, gate_b, w_mix_out,
              s5_lam_re, s5_lam_im, s5_log_dt, s5_b_re, s5_b_im, s5_c_re, s5_c_im, s5_d,
              s5_w_glu, s5_b_glu, s5_w_out, gla_w_a2, gla_b_a, gla_norm_g, gla_w_out,
              na_rpb, na_w_out, conv_dw, conv_dw_b, conv_ln_g, conv_ln_b, conv_w_out,
              moe_w_group, moe_b_group, moe_w_expert, moe_b_expert, moe_w1, moe_w3, moe_w2,
              final_norm_g):
    xc = ctx
    c_act = jax.nn.silu(c)
    cc_act = jax.nn.silu(c_ctx)
    for i in range(DEPTH):
        last = i == DEPTH - 1
        lp = {
            'w_in': w_in[i], 'gate_b': gate_b[i], 'w_mix_out': w_mix_out[i],
            's5_lam_re': s5_lam_re[i], 's5_lam_im': s5_lam_im[i], 's5_log_dt': s5_log_dt[i],
            's5_b_re': s5_b_re[i], 's5_b_im': s5_b_im[i], 's5_c_re': s5_c_re[i], 's5_c_im': s5_c_im[i],
            's5_d': s5_d[i], 's5_w_glu': s5_w_glu[i], 's5_b_glu': s5_b_glu[i], 's5_w_out': s5_w_out[i],
            'gla_w_a2': gla_w_a2[i], 'gla_b_a': gla_b_a[i], 'gla_norm_g': gla_norm_g[i],
            'gla_w_out': gla_w_out[i], 'na_rpb': na_rpb[i], 'na_w_out': na_w_out[i],
            'conv_dw': conv_dw[i], 'conv_dw_b': conv_dw_b[i], 'conv_ln_g': conv_ln_g[i],
            'conv_ln_b': conv_ln_b[i], 'conv_w_out': conv_w_out[i],
        }
        moe_args = (moe_w_group[i], moe_b_group[i], moe_w_expert[i], moe_b_expert[i],
                    moe_w1[i], moe_w3[i], moe_w2[i])
        mod = (c_act @ w_mod[i] + b_mod[i])[:, None, :]
        sh1, sc1, g1, sh2, sc2, g2 = jnp.split(mod, 6, axis=-1)
        n_ctx_mod = 2 if last else 6
        modc = cc_act @ w_mod[i][:, :n_ctx_mod * D_MODEL] + b_mod[i][:n_ctx_mod * D_MODEL]
        modc_parts = jnp.split(modc, n_ctx_mod, axis=-1)
        h = modulate(x, norm1_g[i], sh1, sc1)
        hc = modulate(xc, norm1_g[i], modc_parts[0], modc_parts[1])
        mix, mixc = token_mixer(h, hc, lp, not last)
        x = x + g1 * mix
        x = x + g2 * hierarchical_moe(modulate(x, norm2_g[i], sh2, sc2), *moe_args)
        if not last:
            xc = xc + modc_parts[2] * mixc
            xc = xc + modc_parts[5] * hierarchical_moe(modulate(xc, norm2_g[i], modc_parts[3], modc_parts[4]), *moe_args)
    return rms_norm(x, final_norm_g)
```

```python
import functools
import math

import jax
import jax.numpy as jnp
from jax import lax
from jax.experimental import pallas as pl
from jax.experimental.pallas import tpu as pltpu

F32 = jnp.float32
BF16 = jnp.bfloat16
HIGHEST = lax.Precision.HIGHEST

GRID_W = 64
NORM_EPS = 1e-6
N_BRANCHES = 4
S5_WIDTH = 256
S5_GROUP_SIZE = 16
S5_GROUPS = 16
S5_STATE = 64
GLA_HEADS = 4
GLA_DK = 64
GLA_DV = 128
GLA_GATE_RANK = 16
GLA_TAU = 16.0
GLA_CHUNK = 64
ROPE_BASE = 10000.0
NA_HEADS = 4
NA_HEAD_DIM = 64
NA_WIN_ROWS = 8
NA_WIN_COLS = 16
CONV_WIDTH = 256
CONV_KERNEL = 31
MOE_GROUPS = 4
MOE_EXPERTS_PER_GROUP = 8
MOE_HIDDEN = 256
N_EXPERTS = MOE_GROUPS * MOE_EXPERTS_PER_GROUP

LANES = 128
MOD_ROWS = 8
VMEM_LIMIT = 56 * 1024 * 1024
S5_CHUNK = 32
SEQ_BLOCK = 256
CONV_HALO = 16
NEG_BIG = -1e30


def _cparams(sem):
    return pltpu.CompilerParams(dimension_semantics=sem, vmem_limit_bytes=VMEM_LIMIT)


def _resident(shape):
    nd = len(shape)
    return pl.BlockSpec(shape, lambda *_: (0,) * nd, pipeline_mode=pl.Buffered(1))


def _sigmoid(x):
    return 1.0 / (1.0 + jnp.exp(-x))


def _silu(x):
    return x * _sigmoid(x)


def _mod_kernel(c_ref, w_ref, b_ref, o_ref):
    c = c_ref[...]
    o_ref[0] = jnp.dot(_silu(c), w_ref[0], preferred_element_type=F32, precision=HIGHEST) + b_ref[0]


def modulation_table(c_rows, w_mod, b_mod):
    depth, d, n6 = w_mod.shape
    tn = 1024
    return pl.pallas_call(
        _mod_kernel,
        grid=(depth, n6 // tn),
        in_specs=[
            pl.BlockSpec((MOD_ROWS, d), lambda l, j: (0, 0)),
            pl.BlockSpec((1, d, tn), lambda l, j: (l, 0, j)),
            pl.BlockSpec((1, 1, tn), lambda l, j: (l, 0, j)),
        ],
        out_specs=pl.BlockSpec((1, MOD_ROWS, tn), lambda l, j: (l, 0, j)),
        out_shape=jax.ShapeDtypeStruct((depth, MOD_ROWS, n6), F32),
        compiler_params=_cparams(("arbitrary", "arbitrary")),
        name="mod_table",
    )(c_rows, w_mod, b_mod.reshape(depth, 1, n6))


def _mod_row_spec(n_lat_tiles, tiles_per_batch, nb, width):
    def imap(i, *_):
        return (jnp.where(i < n_lat_tiles, i // tiles_per_batch, nb), 0, 0)
    return pl.BlockSpec((None, 1, width), imap)


def _modulated_norm(x, g, shift, scale):
    y = x * lax.rsqrt(jnp.mean(x * x, axis=-1, keepdims=True) + NORM_EPS)
    return (y * g) * (1.0 + scale) + shift


IN_CHUNK = 512


def _inproj_kernel(x_ref, mod_ref, g_ref, cos_ref, sin_ref, wqk_ref, wmisc_ref, wgate_ref,
                   gq_ref, gk_ref, u_ref, nq_ref, nk_ref, nv_ref, gv_ref, gr_ref, cv_ref, ga_ref, gt_ref):
    d = x_ref.shape[1]
    mod = mod_ref[...]
    h = _modulated_norm(x_ref[...], g_ref[...], mod[:, 0:d], mod[:, d:2 * d]).astype(BF16)
    cos = cos_ref[...]
    sin = sin_ref[...]
    for j, o_ref in enumerate((gq_ref, gk_ref)):
        y = jnp.dot(h, wqk_ref[:, j * 512:(j + 1) * 512], preferred_element_type=F32)
        o_ref[...] = (y[:, :256] * cos + y[:, 256:] * sin).astype(o_ref.dtype)
    col = 0
    for o_ref in (u_ref, nq_ref, nk_ref, nv_ref, gv_ref, gr_ref, cv_ref, ga_ref):
        w = o_ref.shape[1]
        o_ref[...] = jnp.dot(h, wmisc_ref[:, col:col + w], preferred_element_type=F32).astype(o_ref.dtype)
        col += w
    for j in range(gt_ref.shape[1] // IN_CHUNK):
        sl = slice(j * IN_CHUNK, (j + 1) * IN_CHUNK)
        gt_ref[:, sl] = jnp.dot(h, wgate_ref[:, sl], preferred_element_type=F32).astype(gt_ref.dtype)


def in_projection(xs, modtab, norm_g, cos_tab, sin_tab, wqk, wmisc, wgate, *, nb, seq, tm):
    n, d = xs.shape
    n_lat_tiles = nb * seq // tm
    tpb = seq // tm
    widths = (256, 256, 256, 256, 256, 256, 512, 512, 512, LANES, N_BRANCHES * d)
    dtypes = (BF16,) * 9 + (F32, BF16)

    def row(i):
        return (i, 0)

    def rope_row(i):
        return (jnp.where(i < n_lat_tiles, i % tpb, tpb), 0)

    return pl.pallas_call(
        _inproj_kernel,
        grid=(n // tm,),
        in_specs=[
            pl.BlockSpec((tm, d), row),
            _mod_row_spec(n_lat_tiles, tpb, nb, modtab.shape[-1]),
            _resident((1, d)),
            pl.BlockSpec((tm, 256), rope_row),
            pl.BlockSpec((tm, 256), rope_row),
            _resident(wqk.shape),
            _resident(wmisc.shape),
            _resident(wgate.shape),
        ],
        out_specs=[pl.BlockSpec((tm, w), row) for w in widths],
        out_shape=[jax.ShapeDtypeStruct((n, w), dt) for w, dt in zip(widths, dtypes)],
        compiler_params=_cparams(("arbitrary",)),
        name="in_proj",
    )(xs, modtab, norm_g.reshape(1, d), cos_tab, sin_tab, wqk, wmisc, wgate)


def _s5_state_kernel(u_ref, w_ref, s_ref):
    s_ref[0] = jnp.dot(u_ref[0, 0], w_ref[0], preferred_element_type=F32)


def _s5_scan_kernel(s_ref, a_ref, h_ref, *, n_ctx_chunks):
    d = pl.program_id(0)
    nc = s_ref.shape[1]
    a1 = a_ref[0, 0]
    a2 = a_ref[0, 1]

    def step(s, h):
        fwd_row = s
        rev_row = jnp.where(s < n_ctx_chunks, n_ctx_chunks - 1 - s, nc - 1 - (s - n_ctx_chunks))
        r = jnp.where(d == 0, fwd_row, rev_row)
        h_ref[0, r] = h
        return h * a1 + pltpu.roll(h, S5_STATE, axis=1) * a2 + s_ref[0, r]

    lax.fori_loop(0, nc, step, jnp.zeros(h_ref.shape[2:], F32))


def _s5_out_kernel(u_ref, h_ref, t_ref, v_ref, y_ref):
    y = jnp.dot(u_ref[0, 0], t_ref[0], preferred_element_type=F32)
    y += jnp.dot(h_ref[0].astype(BF16), v_ref[0], preferred_element_type=F32)
    y_ref[0, 0] = y.astype(y_ref.dtype)


def s5_scan(u_gm, mats, *, n_ctx_chunks):
    g, nb, nc, cw = u_gm.shape
    t_sum, w_cat, v_cat, a12 = mats
    sw = 4 * S5_STATE
    s = pl.pallas_call(
        _s5_state_kernel,
        grid=(g, nb),
        in_specs=[pl.BlockSpec((1, 1, nc, cw), lambda i, b: (i, b, 0, 0)),
                  pl.BlockSpec((1, cw, sw), lambda i, b: (i, 0, 0))],
        out_specs=pl.BlockSpec((1, nc, sw), lambda i, b: (i * nb + b, 0, 0)),
        out_shape=jax.ShapeDtypeStruct((g * nb, nc, sw), F32),
        compiler_params=_cparams(("arbitrary", "arbitrary")),
        name="s5_chunk_state",
    )(u_gm, w_cat)
    s_t = jnp.transpose(s.reshape(g * nb, nc, 2, 2 * S5_STATE), (2, 1, 0, 3))
    pb = 16
    h_t = pl.pallas_call(
        functools.partial(_s5_scan_kernel, n_ctx_chunks=n_ctx_chunks),
        grid=(2, g * nb // pb),
        in_specs=[pl.BlockSpec((1, nc, pb, 2 * S5_STATE), lambda d, j: (d, 0, j, 0)),
                  pl.BlockSpec((1, 2, pb, 2 * S5_STATE), lambda d, j: (d, 0, j, 0))],
        out_specs=pl.BlockSpec((1, nc, pb, 2 * S5_STATE), lambda d, j: (d, 0, j, 0)),
        out_shape=jax.ShapeDtypeStruct(s_t.shape, F32),
        compiler_params=_cparams(("arbitrary", "arbitrary")),
        name="s5_chunk_scan",
    )(s_t, a12)
    h = jnp.transpose(h_t, (2, 1, 0, 3)).reshape(g * nb, nc, sw)
    return pl.pallas_call(
        _s5_out_kernel,
        grid=(g, nb),
        in_specs=[pl.BlockSpec((1, 1, nc, cw), lambda i, b: (i, b, 0, 0)),
                  pl.BlockSpec((1, nc, sw), lambda i, b: (i * nb + b, 0, 0)),
                  pl.BlockSpec((1, cw, cw), lambda i, b: (i, 0, 0)),
                  pl.BlockSpec((1, sw, cw), lambda i, b: (i, 0, 0))],
        out_specs=pl.BlockSpec((1, 1, nc, cw), lambda i, b: (i, b, 0, 0)),
        out_shape=jax.ShapeDtypeStruct(u_gm.shape, BF16),
        compiler_params=_cparams(("arbitrary", "arbitrary")),
        name="s5_readout",
    )(u_gm, h, t_sum, v_cat)


def s5_matrices(lam_re, lam_im, log_dt, b_re, b_im, c_re, c_im, d_skip, nb):
    ch = S5_CHUNK
    gsz = S5_GROUP_SIZE
    dt = jnp.exp(log_dt.astype(F32))[..., None]
    lr = lam_re.astype(F32)
    li = lam_im.astype(F32)

    def power(n):
        n = n.astype(F32)[:, None, None, None]
        mag = jnp.exp(lr * dt * n)
        return mag * jnp.cos(li * dt * n), mag * jnp.sin(li * dt * n)

    ab_re, ab_im = power(jnp.ones((1,), F32))
    ab_re, ab_im = ab_re[0], ab_im[0]
    den = lr * lr + li * li
    nr = ab_re - 1.0
    ni = ab_im
    coef_re = (nr * lr + ni * li) / den
    coef_im = (ni * lr - nr * li) / den
    br = b_re.astype(F32)
    bi = b_im.astype(F32)
    bb_re = coef_re[..., None] * br - coef_im[..., None] * bi
    bb_im = coef_re[..., None] * bi + coef_im[..., None] * br
    cr = c_re.astype(F32)
    ci = c_im.astype(F32)

    p_re, p_im = power(jnp.arange(ch + 1))
    ca_re = cr[None] * p_re[:, :, :, None, :] - ci[None] * p_im[:, :, :, None, :]
    ca_im = cr[None] * p_im[:, :, :, None, :] + ci[None] * p_re[:, :, :, None, :]
    kmat = (jnp.einsum('ndgip,dgpj->ndgij', ca_re[:ch], bb_re, precision=HIGHEST)
            - jnp.einsum('ndgip,dgpj->ndgij', ca_im[:ch], bb_im, precision=HIGHEST))
    sig = jnp.arange(ch)[:, None]
    tau = jnp.arange(ch)[None, :]
    k_f = jnp.where((tau >= sig)[:, :, None, None, None], kmat[jnp.clip(tau - sig, 0, ch - 1), 0], 0.0)
    k_r = jnp.where((sig >= tau)[:, :, None, None, None], kmat[jnp.clip(sig - tau, 0, ch - 1), 1], 0.0)
    t_sum = jnp.transpose(k_f + k_r, (2, 0, 4, 1, 3))
    skip = jnp.eye(ch * gsz, dtype=F32)[None] * jnp.tile(d_skip.astype(F32).reshape(S5_GROUPS, 1, gsz), (1, ch, 1)).reshape(S5_GROUPS, 1, ch * gsz)
    t_sum = t_sum.reshape(S5_GROUPS, ch * gsz, ch * gsz) + skip

    def w_dir(d, exps):
        pr, pi = p_re[exps, d], p_im[exps, d]
        wr = pr[..., None] * bb_re[d][None] - pi[..., None] * bb_im[d][None]
        wi = pr[..., None] * bb_im[d][None] + pi[..., None] * bb_re[d][None]
        w = jnp.concatenate([wr, wi], axis=2)
        return jnp.transpose(w, (1, 0, 3, 2)).reshape(S5_GROUPS, ch * gsz, 2 * S5_STATE)
    w_cat = jnp.concatenate([w_dir(0, ch - 1 - jnp.arange(ch)), w_dir(1, jnp.arange(ch))], axis=-1)

    def v_dir(d, exps):
        vr = ca_re[exps, d]
        vi = -ca_im[exps, d]
        v = jnp.concatenate([vr, vi], axis=-1)
        return jnp.transpose(v, (1, 3, 0, 2)).reshape(S5_GROUPS, 2 * S5_STATE, ch * gsz)
    v_cat = jnp.concatenate([v_dir(0, 1 + jnp.arange(ch)), v_dir(1, ch - jnp.arange(ch))], axis=1)

    a1 = jnp.concatenate([p_re[ch], p_re[ch]], axis=-1)
    a2 = jnp.concatenate([-p_im[ch], p_im[ch]], axis=-1)
    a12 = jnp.stack([a1, a2], axis=1)
    a12 = jnp.repeat(a12, nb, axis=2)
    return t_sum.astype(BF16), w_cat.astype(BF16), v_cat.astype(BF16), a12


def _gla_direction(q, k, v, z, s_ref, reverse, consts):
    tri_f, tri_r, head_mask, block_mask, ones_cols = consts
    c = q.shape[0]
    g = (jnp.minimum(z, 0.0) - jnp.log(1.0 + jnp.exp(-jnp.abs(z)))) * (1.0 / GLA_TAU)
    tri = tri_r if reverse else tri_f
    b = jnp.dot(tri, g, preferred_element_type=F32, precision=HIGHEST)
    b_last = b[0:1] if reverse else b[c - 1:c]
    q_t = q * jnp.exp(b)
    k_t = (k * jnp.exp(-b)).astype(BF16)
    k_end = (k * jnp.exp(b_last - b)).astype(BF16)
    s_old = s_ref[...]
    o = jnp.dot(q_t.astype(BF16), s_old.astype(BF16), preferred_element_type=F32)
    outs = []
    for h in range(GLA_HEADS):
        qh = (q_t * head_mask[h:h + 1]).astype(BF16)
        att = lax.dot_general(qh, k_t, (((1,), (1,)), ((), ())), preferred_element_type=F32)
        att = att * tri
        vh = v[:, h * GLA_DV:(h + 1) * GLA_DV]
        outs.append(jnp.dot(att.astype(BF16), vh, preferred_element_type=F32))
    o = o + jnp.concatenate(outs, axis=1)
    kv = lax.dot_general(k_end, v, (((0,), (0,)), ((), ())), preferred_element_type=F32)
    tot = lax.dot_general(g, ones_cols, (((0,), (0,)), ((), ())), preferred_element_type=F32, precision=HIGHEST)
    decay = jnp.exp(tot)
    decay = jnp.concatenate([decay] * (s_old.shape[1] // LANES), axis=1)
    s_ref[...] = (decay * s_old + kv) * block_mask
    return o


def _gla_kernel(qf_ref, kf_ref, vf_ref, af_ref, qr_ref, kr_ref, vr_ref, ar_ref, wa_ref, ba_ref,
                of_ref, or_ref, sf_ref, sr_ref):
    @pl.when(pl.program_id(1) == 0)
    def _():
        sf_ref[...] = jnp.zeros_like(sf_ref)
        sr_ref[...] = jnp.zeros_like(sr_ref)

    c = GLA_CHUNK
    hk = GLA_HEADS * GLA_DK
    hv = GLA_HEADS * GLA_DV
    ri = lax.broadcasted_iota(jnp.int32, (c, c), 0)
    ci = lax.broadcasted_iota(jnp.int32, (c, c), 1)
    tri_f = (ri >= ci).astype(F32)
    tri_r = (ri <= ci).astype(F32)
    lane = lax.broadcasted_iota(jnp.int32, (GLA_HEADS, hk), 1)
    hrow = lax.broadcasted_iota(jnp.int32, (GLA_HEADS, hk), 0)
    head_mask = (lane // GLA_DK == hrow).astype(F32)
    br = lax.broadcasted_iota(jnp.int32, (hk, hv), 0) // GLA_DK
    bc = lax.broadcasted_iota(jnp.int32, (hk, hv), 1) // GLA_DV
    block_mask = (br == bc).astype(F32)
    ones_cols = jnp.ones((c, LANES), F32)
    consts = (tri_f, tri_r, head_mask, block_mask, ones_cols)
    wa = wa_ref[...]
    ba = ba_ref[...]
    nchunks = qf_ref.shape[0] // c
    for j in range(nchunks):
        sl = slice(j * c, (j + 1) * c)
        z = jnp.dot(af_ref[sl, :], wa[:, :hk], preferred_element_type=F32, precision=HIGHEST) + ba[:, :hk]
        o = _gla_direction(qf_ref[sl, :].astype(F32), kf_ref[sl, :].astype(F32), vf_ref[sl, :], z,
                           sf_ref, False, consts)
        of_ref[sl, :] = o.astype(of_ref.dtype)
    for j in reversed(range(nchunks)):
        sl = slice(j * c, (j + 1) * c)
        z = jnp.dot(ar_ref[sl, :], wa[:, hk:], preferred_element_type=F32, precision=HIGHEST) + ba[:, hk:]
        o = _gla_direction(qr_ref[sl, :].astype(F32), kr_ref[sl, :].astype(F32), vr_ref[sl, :], z,
                           sr_ref, True, consts)
        or_ref[sl, :] = o.astype(or_ref.dtype)


def gla_scan(gq, gk, gv, ga, wa, ba, *, nb, seq, ctx_len):
    n = gq.shape[0]
    blk = SEQ_BLOCK
    assert ctx_len == blk
    lpb = seq // blk
    ctx0 = nb * lpb

    def fwd(b, s):
        return (jnp.where(s == 0, ctx0 + b, b * lpb + s - 1), 0)

    def rev(b, s):
        return (jnp.where(s == 0, ctx0 + b, b * lpb + lpb - s), 0)

    hk = GLA_HEADS * GLA_DK
    hv = GLA_HEADS * GLA_DV
    specs = []
    for imap in (fwd, rev):
        specs += [pl.BlockSpec((blk, hk), imap), pl.BlockSpec((blk, hk), imap),
                  pl.BlockSpec((blk, hv), imap), pl.BlockSpec((blk, LANES), imap)]
    specs += [_resident(wa.shape), _resident(ba.shape)]
    return pl.pallas_call(
        _gla_kernel,
        grid=(nb, lpb + 1),
        in_specs=specs,
        out_specs=[pl.BlockSpec((blk, hv), fwd), pl.BlockSpec((blk, hv), rev)],
        out_shape=[jax.ShapeDtypeStruct((n, hv), BF16)] * 2,
        scratch_shapes=[pltpu.VMEM((hk, hv), F32), pltpu.VMEM((hk, hv), F32)],
        compiler_params=_cparams(("arbitrary", "arbitrary")),
        name="gla_scan",
    )(gq, gk, gv, ga, gq, gk, gv, ga, wa, ba)


NA_ROWS_PER_STEP = 4


def _na_kernel(q_ref, k_ref, v_ref, kc_ref, vc_ref, bias_ref, o_ref, *, n_rows):
    step = pl.program_id(1)
    hd = NA_HEADS * NA_HEAD_DIM
    lane = lax.broadcasted_iota(jnp.int32, (NA_HEADS, hd), 1)
    hrow = lax.broadcasted_iota(jnp.int32, (NA_HEADS, hd), 0)
    head_mask = (lane // NA_HEAD_DIM == hrow).astype(F32)
    kc = kc_ref[...]
    vc = vc_ref[...]
    win = NA_WIN_ROWS * GRID_W
    for i in range(NA_ROWS_PER_STEP):
        r = step * NA_ROWS_PER_STEP + i
        rs = jnp.clip(r - NA_WIN_ROWS // 2, 0, n_rows - NA_WIN_ROWS)
        delta = rs - r + NA_WIN_ROWS - 1
        start = pl.multiple_of(rs * GRID_W, GRID_W)
        kw = k_ref[pl.ds(start, win), :]
        vw = v_ref[pl.ds(start, win), :]
        q = q_ref[i * GRID_W:(i + 1) * GRID_W, :].astype(F32)
        acc = jnp.zeros((GRID_W, hd), F32)
        for h in range(NA_HEADS):
            m_h = head_mask[h:h + 1]
            qh = (q * m_h).astype(BF16)
            s_lat = lax.dot_general(qh, kw, (((1,), (1,)), ((), ())), preferred_element_type=F32)
            s_lat = s_lat + bias_ref[delta, h]
            s_ctx = lax.dot_general(qh, kc, (((1,), (1,)), ((), ())), preferred_element_type=F32)
            m = jnp.maximum(jnp.max(s_lat, axis=1, keepdims=True), jnp.max(s_ctx, axis=1, keepdims=True))
            p_lat = jnp.exp(s_lat - m)
            p_ctx = jnp.exp(s_ctx - m)
            den = jnp.sum(p_lat, axis=1, keepdims=True) + jnp.sum(p_ctx, axis=1, keepdims=True)
            o = jnp.dot(p_lat.astype(BF16), vw, preferred_element_type=F32)
            o += jnp.dot(p_ctx.astype(BF16), vc, preferred_element_type=F32)
            acc += (o / den) * m_h
        o_ref[i * GRID_W:(i + 1) * GRID_W, :] = acc.astype(o_ref.dtype)


def _na_ctx_kernel(q_ref, k_ref, v_ref, o_ref):
    hd = NA_HEADS * NA_HEAD_DIM
    lane = lax.broadcasted_iota(jnp.int32, (NA_HEADS, hd), 1)
    hrow = lax.broadcasted_iota(jnp.int32, (NA_HEADS, hd), 0)
    head_mask = (lane // NA_HEAD_DIM == hrow).astype(F32)
    q = q_ref[...].astype(F32)
    k = k_ref[...]
    v = v_ref[...]
    acc = jnp.zeros(q.shape, F32)
    for h in range(NA_HEADS):
        m_h = head_mask[h:h + 1]
        s = lax.dot_general((q * m_h).astype(BF16), k, (((1,), (1,)), ((), ())), preferred_element_type=F32)
        p = jnp.exp(s - jnp.max(s, axis=1, keepdims=True))
        o = jnp.dot(p.astype(BF16), v, preferred_element_type=F32) / jnp.sum(p, axis=1, keepdims=True)
        acc += o * m_h
    o_ref[...] = acc.astype(o_ref.dtype)


def neighbourhood_attention(nq, nk, nv, bias, *, nb, seq, ctx_len):
    n, hd = nq.shape
    n_rows = seq // GRID_W
    qb = NA_ROWS_PER_STEP * GRID_W
    steps = seq // qb
    ctx0 = nb * seq // ctx_len
    o_lat = pl.pallas_call(
        functools.partial(_na_kernel, n_rows=n_rows),
        grid=(nb, steps),
        in_specs=[pl.BlockSpec((qb, hd), lambda b, s: (b * steps + s, 0)),
                  pl.BlockSpec((seq, hd), lambda b, s: (b, 0)),
                  pl.BlockSpec((seq, hd), lambda b, s: (b, 0)),
                  pl.BlockSpec((ctx_len, hd), lambda b, s: (ctx0 + b, 0)),
                  pl.BlockSpec((ctx_len, hd), lambda b, s: (ctx0 + b, 0)),
                  _resident(bias.shape)],
        out_specs=pl.BlockSpec((qb, hd), lambda b, s: (b * steps + s, 0)),
        out_shape=jax.ShapeDtypeStruct((nb * seq, hd), BF16),
        compiler_params=_cparams(("arbitrary", "arbitrary")),
        name="na_latent",
    )(nq, nk, nv, nk, nv, bias)
    o_ctx = pl.pallas_call(
        _na_ctx_kernel,
        grid=(nb,),
        in_specs=[pl.BlockSpec((ctx_len, hd), lambda b: (ctx0 + b, 0))] * 3,
        out_specs=pl.BlockSpec((ctx_len, hd), lambda b: (b, 0)),
        out_shape=jax.ShapeDtypeStruct((nb * ctx_len, hd), BF16),
        compiler_params=_cparams(("arbitrary",)),
        name="na_context",
    )(nq, nk, nv)
    return jnp.concatenate([o_lat, o_ctx], axis=0)


def na_bias_tables(rpb):
    rpb = rpb.astype(F32)
    c_idx = jnp.arange(GRID_W)
    col_start = jnp.clip(c_idx - NA_WIN_COLS // 2, 0, GRID_W - NA_WIN_COLS)
    col_in = (c_idx[None, :] >= col_start[:, None]) & (c_idx[None, :] < col_start[:, None] + NA_WIN_COLS)
    dc_idx = jnp.clip(c_idx[None, :] - c_idx[:, None] + NA_WIN_COLS - 1, 0, 2 * NA_WIN_COLS - 2)
    delta = jnp.arange(NA_WIN_ROWS)[:, None]
    off = jnp.arange(NA_WIN_ROWS)[None, :]
    dr_idx = delta + off
    b = rpb[:, dr_idx]
    b = b[:, :, :, dc_idx]
    b = jnp.where(col_in[None, None, None], b, NEG_BIG)
    b = jnp.transpose(b, (1, 0, 3, 2, 4))
    return b.reshape(NA_WIN_ROWS, NA_HEADS, GRID_W, NA_WIN_ROWS * GRID_W)


def _conv_kernel(prev_ref, main_ref, next_ref, dw_ref, dwb_ref, lng_ref, lnb_ref, o_ref, buf_ref,
                 *, n_lat_tiles, tiles_per_batch):
    i = pl.program_id(0)
    j = i % tiles_per_batch
    is_lat = i < n_lat_tiles
    has_prev = jnp.logical_and(is_lat, j > 0)
    has_next = jnp.logical_and(is_lat, j < tiles_per_batch - 1)
    cw = CONV_WIDTH
    tl = main_ref.shape[0]

    def glu(a):
        a = a.astype(F32)
        return a[:, :cw] * _sigmoid(a[:, cw:])

    buf_ref[0:CONV_HALO, :] = glu(prev_ref[...]) * has_prev.astype(F32)
    buf_ref[CONV_HALO:CONV_HALO + tl, :] = glu(main_ref[...])
    buf_ref[CONV_HALO + tl:, :] = glu(next_ref[...]) * has_next.astype(F32)
    dw = dw_ref[...]
    acc = jnp.zeros((tl, cw), F32) + dwb_ref[...]
    base = CONV_HALO - CONV_KERNEL // 2
    for k in range(CONV_KERNEL):
        acc += buf_ref[base + k:base + k + tl, :] * dw[k:k + 1, :]
    mu = jnp.mean(acc, axis=-1, keepdims=True)
    xc = acc - mu
    y = xc * lax.rsqrt(jnp.mean(xc * xc, axis=-1, keepdims=True) + NORM_EPS)
    y = y * lng_ref[...] + lnb_ref[...]
    o_ref[...] = _silu(y).astype(o_ref.dtype)


def conv_branch(cv, dw, dw_b, ln_g, ln_b, *, nb, seq):
    n = cv.shape[0]
    tl = SEQ_BLOCK
    hb = tl // CONV_HALO
    n_tiles = n // tl
    cw = CONV_WIDTH
    return pl.pallas_call(
        functools.partial(_conv_kernel, n_lat_tiles=nb * seq // tl, tiles_per_batch=seq // tl),
        grid=(n_tiles,),
        in_specs=[pl.BlockSpec((CONV_HALO, 2 * cw), lambda i: (jnp.maximum(i * hb - 1, 0), 0)),
                  pl.BlockSpec((tl, 2 * cw), lambda i: (i, 0)),
                  pl.BlockSpec((CONV_HALO, 2 * cw), lambda i: (jnp.minimum((i + 1) * hb, n_tiles * hb - 1), 0)),
                  _resident((CONV_KERNEL, cw)), _resident((1, cw)), _resident((1, cw)), _resident((1, cw))],
        out_specs=pl.BlockSpec((tl, cw), lambda i: (i, 0)),
        out_shape=jax.ShapeDtypeStruct((n, cw), BF16),
        scratch_shapes=[pltpu.VMEM((tl + 2 * CONV_HALO, cw), F32)],
        compiler_params=_cparams(("arbitrary",)),
        name="conv_branch",
    )(cv, cv, cv, dw.astype(F32), dw_b.reshape(1, cw).astype(F32), ln_g.reshape(1, cw).astype(F32),
      ln_b.reshape(1, cw).astype(F32))


def _merge_kernel(x_ref, mod_ref, gt_ref, ys5_ref, of_ref, or_ref, gr_ref, na_ref, cv_ref,
                  gate_b_ref, wglu_ref, bglu_ref, ws5_ref, gng_ref, wgla_ref, wna_ref, wcv_ref, wmix_ref, o_ref):
    d = x_ref.shape[1]
    z = jax.nn.gelu(ys5_ref[...].astype(F32))
    z = z * _sigmoid(jnp.dot(z.astype(BF16), wglu_ref[...], preferred_element_type=F32) + bglu_ref[...])
    br_s5 = jnp.dot(z.astype(BF16), ws5_ref[...], preferred_element_type=F32)
    o = of_ref[...].astype(F32) + or_ref[...].astype(F32)
    r = _silu(gr_ref[...].astype(F32))
    parts = []
    for h in range(GLA_HEADS):
        oh = o[:, h * GLA_DV:(h + 1) * GLA_DV]
        oh = oh * lax.rsqrt(jnp.mean(oh * oh, axis=-1, keepdims=True) + NORM_EPS) * gng_ref[...]
        parts.append(oh * r[:, h * GLA_DV:(h + 1) * GLA_DV])
    y_gla = jnp.concatenate(parts, axis=1).astype(BF16)
    br_gla = jnp.dot(y_gla, wgla_ref[...], preferred_element_type=F32)
    br_na = jnp.dot(na_ref[...], wna_ref[...], preferred_element_type=F32)
    br_cv = jnp.dot(cv_ref[...], wcv_ref[...], preferred_element_type=F32)
    merged = jnp.zeros((x_ref.shape[0], d), F32)
    for i, br in enumerate((br_s5, br_gla, br_na, br_cv)):
        gate = _sigmoid(gt_ref[:, i * d:(i + 1) * d].astype(F32) + gate_b_ref[:, i * d:(i + 1) * d])
        merged += gate * br
    mix = jnp.dot(merged.astype(BF16), wmix_ref[...], preferred_element_type=F32)
    o_ref[...] = x_ref[...] + mod_ref[:, 2 * d:3 * d] * mix


def merge_branches(xs, modtab, gt, ys5, o_f, o_r, gr, o_na, y_cv, weights, *, nb, seq, tm):
    n, d = xs.shape
    n_lat_tiles = nb * seq // tm
    tpb = seq // tm

    def row(i):
        return (i, 0)

    acts = (gt, ys5, o_f, o_r, gr, o_na, y_cv)
    return pl.pallas_call(
        _merge_kernel,
        grid=(n // tm,),
        in_specs=[pl.BlockSpec((tm, d), row), _mod_row_spec(n_lat_tiles, tpb, nb, modtab.shape[-1])]
        + [pl.BlockSpec((tm, a.shape[1]), row) for a in acts]
        + [_resident(w.shape) for w in weights],
        out_specs=pl.BlockSpec((tm, d), row),
        out_shape=jax.ShapeDtypeStruct((n, d), F32),
        compiler_params=_cparams(("arbitrary",)),
        name="merge_mix",
    )(xs, modtab, *acts, *weights)


def _moe_kernel(x_ref, mod_ref, g_ref, wr_ref, br_ref, w1_ref, w3_ref, w2_ref, o_ref, h_ref, comb_ref, acc_ref):
    e = pl.program_id(1)
    d = x_ref.shape[1]

    @pl.when(e == 0)
    def _():
        mod = mod_ref[...]
        h = _modulated_norm(x_ref[...], g_ref[...], mod[:, 3 * d:4 * d], mod[:, 4 * d:5 * d])
        h_ref[...] = h.astype(BF16)
        logits = jnp.dot(h, wr_ref[...], preferred_element_type=F32, precision=HIGHEST) + br_ref[...]
        lane = lax.broadcasted_iota(jnp.int32, logits.shape, 1)
        big = jnp.int32(1 << 20)
        is_g = lane < MOE_GROUPS
        gl = jnp.where(is_g, logits, -jnp.inf)
        gmax = jnp.max(gl, axis=1, keepdims=True)
        gidx = jnp.min(jnp.where(gl == gmax, lane, big), axis=1, keepdims=True)
        group_p = 1.0 / jnp.sum(jnp.where(is_g, jnp.exp(logits - gmax), 0.0), axis=1, keepdims=True)
        eid = lane - MOE_GROUPS
        in_group = jnp.logical_and(eid >= 0, eid // MOE_EXPERTS_PER_GROUP == gidx)
        in_group = jnp.logical_and(in_group, eid < N_EXPERTS)
        el = jnp.where(in_group, logits, -jnp.inf)
        v1 = jnp.max(el, axis=1, keepdims=True)
        i1 = jnp.min(jnp.where(el == v1, lane, big), axis=1, keepdims=True)
        el2 = jnp.where(lane == i1, -jnp.inf, el)
        v2 = jnp.max(el2, axis=1, keepdims=True)
        i2 = jnp.min(jnp.where(el2 == v2, lane, big), axis=1, keepdims=True)
        t = jnp.exp(v2 - v1)
        w1 = group_p / (1.0 + t)
        w2 = group_p * t / (1.0 + t)
        comb_ref[...] = jnp.where(lane == i1, w1, 0.0) + jnp.where(lane == i2, w2, 0.0)
        acc_ref[...] = jnp.zeros_like(acc_ref)

    lane = lax.broadcasted_iota(jnp.int32, comb_ref.shape, 1)
    cw = jnp.sum(jnp.where(lane == e + MOE_GROUPS, comb_ref[...], 0.0), axis=1, keepdims=True)
    h = h_ref[...]
    a = jnp.dot(h, w1_ref[0], preferred_element_type=F32)
    b = jnp.dot(h, w3_ref[0], preferred_element_type=F32)
    act = (_silu(a) * b * cw).astype(BF16)
    acc_ref[...] += jnp.dot(act, w2_ref[0], preferred_element_type=F32)

    @pl.when(e == pl.num_programs(1) - 1)
    def _():
        o_ref[...] = x_ref[...] + mod_ref[:, 5 * d:6 * d] * acc_ref[...]


def moe_layer(xs, modtab, norm_g, w_router, b_router, w1, w3, w2, *, nb, seq, tm):
    n, d = xs.shape
    n_lat_tiles = nb * seq // tm
    tpb = seq // tm
    hid = w1.shape[-1]
    return pl.pallas_call(
        _moe_kernel,
        grid=(n // tm, N_EXPERTS),
        in_specs=[pl.BlockSpec((tm, d), lambda i, e: (i, 0)),
                  _mod_row_spec(n_lat_tiles, tpb, nb, modtab.shape[-1]),
                  _resident((1, d)), _resident(w_router.shape), _resident(b_router.shape),
                  pl.BlockSpec((1, d, hid), lambda i, e: (e, 0, 0)),
                  pl.BlockSpec((1, d, hid), lambda i, e: (e, 0, 0)),
                  pl.BlockSpec((1, hid, d), lambda i, e: (e, 0, 0))],
        out_specs=pl.BlockSpec((tm, d), lambda i, e: (i, 0)),
        out_shape=jax.ShapeDtypeStruct((n, d), F32),
        scratch_shapes=[pltpu.VMEM((tm, d), BF16), pltpu.VMEM((tm, LANES), F32), pltpu.VMEM((tm, d), F32)],
        compiler_params=_cparams(("arbitrary", "arbitrary")),
        name="moe",
    )(xs, modtab, norm_g.reshape(1, d), w_router, b_router, w1, w3, w2)


def _final_norm_kernel(x_ref, g_ref, o_ref):
    x = x_ref[...]
    o_ref[...] = x * lax.rsqrt(jnp.mean(x * x, axis=-1, keepdims=True) + NORM_EPS) * g_ref[...]


def final_norm(xs, g, n_rows, tm):
    d = xs.shape[1]
    return pl.pallas_call(
        _final_norm_kernel,
        grid=(n_rows // tm,),
        in_specs=[pl.BlockSpec((tm, d), lambda i: (i, 0)), _resident((1, d))],
        out_specs=pl.BlockSpec((tm, d), lambda i: (i, 0)),
        out_shape=jax.ShapeDtypeStruct((n_rows, d), F32),
        compiler_params=_cparams(("arbitrary",)),
        name="final_norm",
    )(xs, g.reshape(1, d).astype(F32))


def rope_tables(seq, pad_rows):
    t = jnp.arange(seq, dtype=jnp.int32)
    row = (t // GRID_W).astype(F32)
    colp = (t % GRID_W).astype(F32)
    half = GLA_DK // 2
    inv_freq = ROPE_BASE ** (-jnp.arange(0, half, 2, dtype=F32) / half)
    ang_r = row[:, None] * inv_freq
    ang_c = colp[:, None] * inv_freq
    dd = jnp.arange(GLA_DK)
    ang = jnp.where((dd < half)[None, :], ang_r[:, dd % (half // 2)], ang_c[:, dd % (half // 2)])
    sign = jnp.where((dd % half) < half // 2, -1.0, 1.0).astype(F32)
    cos = jnp.tile(jnp.cos(ang), (1, GLA_HEADS))
    sin = jnp.tile(jnp.sin(ang) * sign[None, :], (1, GLA_HEADS))
    cos = jnp.concatenate([cos, jnp.ones((pad_rows, cos.shape[1]), F32)], axis=0)
    sin = jnp.concatenate([sin, jnp.zeros((pad_rows, sin.shape[1]), F32)], axis=0)
    return cos, sin


def split_in_weights(w_in, d):
    widths = (S5_WIDTH, GLA_HEADS * GLA_DK, GLA_HEADS * GLA_DK, GLA_HEADS * GLA_DV, GLA_HEADS * GLA_DV,
              2 * GLA_GATE_RANK, NA_HEADS * NA_HEAD_DIM, NA_HEADS * NA_HEAD_DIM, NA_HEADS * NA_HEAD_DIM,
              2 * CONV_WIDTH, N_BRANCHES * d)
    names = ('u', 'gq', 'gk', 'gv', 'gr', 'ga', 'nq', 'nk', 'nv', 'cv', 'gt')
    parts = {}
    col = 0
    for nme, w in zip(names, widths):
        parts[nme] = w_in[:, col:col + w]
        col += w
    swap = jnp.arange(GLA_HEADS * GLA_DK) ^ (GLA_DK // 4)
    gq = parts['gq'] * (GLA_DK ** -0.5)
    wqk = jnp.concatenate([gq, gq[:, swap], parts['gk'], parts['gk'][:, swap]], axis=1)
    ga = jnp.pad(parts['ga'], ((0, 0), (0, LANES - 2 * GLA_GATE_RANK)))
    wmisc = jnp.concatenate([parts['u'], parts['nq'] * (NA_HEAD_DIM ** -0.5), parts['nk'], parts['nv'],
                             parts['gv'], parts['gr'], parts['cv'], ga], axis=1)
    return wqk.astype(BF16), wmisc.astype(BF16), parts['gt'].astype(BF16)


def _to_group_major(u, nb, seq, ctx_len):
    ch = S5_CHUNK
    lat = u[:nb * seq].reshape(nb, seq // ch, ch, S5_GROUPS, S5_GROUP_SIZE)
    ctx = u[nb * seq:].reshape(nb, ctx_len // ch, ch, S5_GROUPS, S5_GROUP_SIZE)
    a = jnp.concatenate([ctx, lat], axis=1)
    a = jnp.transpose(a, (3, 0, 1, 2, 4))
    return a.reshape(S5_GROUPS, nb, (seq + ctx_len) // ch, ch * S5_GROUP_SIZE)


def _from_group_major(y, nb, seq, ctx_len):
    ch = S5_CHUNK
    nc = (seq + ctx_len) // ch
    a = y.reshape(S5_GROUPS, nb, nc, ch, S5_GROUP_SIZE)
    a = jnp.transpose(a, (1, 2, 3, 0, 4)).reshape(nb, nc * ch, S5_WIDTH)
    return jnp.concatenate([a[:, ctx_len:].reshape(nb * seq, S5_WIDTH),
                            a[:, :ctx_len].reshape(nb * ctx_len, S5_WIDTH)], axis=0)


def kernel(x, c, ctx, c_ctx, norm1_g, norm2_g, w_mod, b_mod, w_in, gate_b, w_mix_out, s5_lam_re, s5_lam_im, s5_log_dt, s5_b_re, s5_b_im, s5_c_re, s5_c_im, s5_d, s5_w_glu, s5_b_glu, s5_w_out, gla_w_a2, gla_b_a, gla_norm_g, gla_w_out, na_rpb, na_w_out, conv_dw, conv_dw_b, conv_ln_g, conv_ln_b, conv_w_out, moe_w_group, moe_b_group, moe_w_expert, moe_b_expert, moe_w1, moe_w3, moe_w2, final_norm_g):
    nb, seq, d = x.shape
    ctx_len = ctx.shape[1]
    depth = w_mod.shape[0]
    n_lat = nb * seq
    tm = 512
    tm_moe = math.gcd(1024, nb * ctx_len)
    assert ctx_len == SEQ_BLOCK and seq % tm_moe == 0 and (nb * ctx_len) % tm_moe == 0 and nb < MOD_ROWS

    xs = jnp.concatenate([x.reshape(n_lat, d), ctx.reshape(nb * ctx_len, d)], axis=0).astype(F32)
    c_rows = jnp.zeros((MOD_ROWS, d), F32).at[:nb].set(c.astype(F32)).at[nb].set(c_ctx.astype(F32))
    modtab = modulation_table(c_rows, w_mod.astype(F32), b_mod.astype(F32))
    modtab = modtab.reshape(depth, MOD_ROWS, 1, 6 * d)
    cos_tab, sin_tab = rope_tables(seq, tm)

    for i in range(depth):
        wqk, wmisc, wgate = split_in_weights(w_in[i], d)
        gq, gk, u, nq, nk, nv, gv, gr, cv, ga, gt = in_projection(
            xs, modtab[i], norm1_g[i].astype(F32), cos_tab, sin_tab, wqk, wmisc, wgate, nb=nb, seq=seq, tm=tm)

        mats = s5_matrices(s5_lam_re[i], s5_lam_im[i], s5_log_dt[i], s5_b_re[i], s5_b_im[i],
                           s5_c_re[i], s5_c_im[i], s5_d[i], nb)
        y_gm = s5_scan(_to_group_major(u, nb, seq, ctx_len), mats, n_ctx_chunks=ctx_len // S5_CHUNK)
        ys5 = _from_group_major(y_gm, nb, seq, ctx_len)

        hk = GLA_HEADS * GLA_DK
        wa = jnp.zeros((LANES, 2 * hk), F32)
        wa = wa.at[:GLA_GATE_RANK, :hk].set(gla_w_a2[i, 0].astype(F32))
        wa = wa.at[GLA_GATE_RANK:2 * GLA_GATE_RANK, hk:].set(gla_w_a2[i, 1].astype(F32))
        ba = gla_b_a[i].astype(F32).reshape(1, 2 * hk)
        o_f, o_r = gla_scan(gq, gk, gv, ga, wa, ba, nb=nb, seq=seq, ctx_len=ctx_len)

        o_na = neighbourhood_attention(nq, nk, nv, na_bias_tables(na_rpb[i]), nb=nb, seq=seq, ctx_len=ctx_len)
        y_cv = conv_branch(cv, conv_dw[i], conv_dw_b[i], conv_ln_g[i], conv_ln_b[i], nb=nb, seq=seq)

        weights = (gate_b[i].astype(F32).reshape(1, N_BRANCHES * d), s5_w_glu[i].astype(BF16),
                   s5_b_glu[i].astype(F32).reshape(1, S5_WIDTH), s5_w_out[i].astype(BF16),
                   gla_norm_g[i].astype(F32).reshape(1, GLA_DV), gla_w_out[i].astype(BF16),
                   na_w_out[i].astype(BF16), conv_w_out[i].astype(BF16), w_mix_out[i].astype(BF16))
        xs = merge_branches(xs, modtab[i], gt, ys5, o_f, o_r, gr, o_na, y_cv, weights, nb=nb, seq=seq, tm=tm)

        n_router = MOE_GROUPS + N_EXPERTS
        w_router = jnp.pad(jnp.concatenate([moe_w_group[i], moe_w_expert[i]], axis=1).astype(F32),
                           ((0, 0), (0, LANES - n_router)))
        b_router = jnp.pad(jnp.concatenate([moe_b_group[i], moe_b_expert[i]]).astype(F32),
                           (0, LANES - n_router)).reshape(1, LANES)
        hid = moe_w1.shape[-1]
        xs = moe_layer(xs, modtab[i], norm2_g[i].astype(F32), w_router, b_router,
                       moe_w1[i].reshape(N_EXPERTS, d, hid).astype(BF16),
                       moe_w3[i].reshape(N_EXPERTS, d, hid).astype(BF16),
                       moe_w2[i].reshape(N_EXPERTS, hid, d).astype(BF16), nb=nb, seq=seq, tm=tm_moe)

    out = final_norm(xs, final_norm_g, n_lat, tm)
    return out.reshape(nb, seq, d).astype(x.dtype)
```

```python
import functools
import math

import jax
import jax.numpy as jnp
from jax import lax
from jax.experimental import pallas as pl
from jax.experimental.pallas import tpu as pltpu

F32 = jnp.float32
BF16 = jnp.bfloat16
HIGHEST = lax.Precision.HIGHEST

GRID_W = 64
NORM_EPS = 1e-6
N_BRANCHES = 4
S5_WIDTH = 256
S5_GROUP_SIZE = 16
S5_GROUPS = 16
S5_STATE = 64
GLA_HEADS = 4
GLA_DK = 64
GLA_DV = 128
GLA_GATE_RANK = 16
GLA_TAU = 16.0
GLA_CHUNK = 64
ROPE_BASE = 10000.0
NA_HEADS = 4
NA_HEAD_DIM = 64
NA_WIN_ROWS = 8
NA_WIN_COLS = 16
CONV_WIDTH = 256
CONV_KERNEL = 31
MOE_GROUPS = 4
MOE_EXPERTS_PER_GROUP = 8
MOE_HIDDEN = 256
N_EXPERTS = MOE_GROUPS * MOE_EXPERTS_PER_GROUP

LANES = 128
MOD_ROWS = 8
VMEM_LIMIT = 56 * 1024 * 1024
S5_CHUNK = 32
SEQ_BLOCK = 256
CONV_HALO = 16
NEG_BIG = -1e30


def _cparams(sem):
    return pltpu.CompilerParams(dimension_semantics=sem, vmem_limit_bytes=VMEM_LIMIT)


def _resident(shape):
    nd = len(shape)
    return pl.BlockSpec(shape, lambda *_: (0,) * nd, pipeline_mode=pl.Buffered(1))


def _sigmoid(x):
    return 1.0 / (1.0 + jnp.exp(-x))


def _silu(x):
    return x * _sigmoid(x)


def _mod_kernel(c_ref, w_ref, b_ref, o_ref):
    c = c_ref[...]
    o_ref[0] = jnp.dot(_silu(c), w_ref[0], preferred_element_type=F32, precision=HIGHEST) + b_ref[0]


def modulation_table(c_rows, w_mod, b_mod):
    depth, d, n6 = w_mod.shape
    tn = 1024
    return pl.pallas_call(
        _mod_kernel,
        grid=(depth, n6 // tn),
        in_specs=[
            pl.BlockSpec((MOD_ROWS, d), lambda l, j: (0, 0)),
            pl.BlockSpec((1, d, tn), lambda l, j: (l, 0, j)),
            pl.BlockSpec((1, 1, tn), lambda l, j: (l, 0, j)),
        ],
        out_specs=pl.BlockSpec((1, MOD_ROWS, tn), lambda l, j: (l, 0, j)),
        out_shape=jax.ShapeDtypeStruct((depth, MOD_ROWS, n6), F32),
        compiler_params=_cparams(("arbitrary", "arbitrary")),
        name="mod_table",
    )(c_rows, w_mod, b_mod.reshape(depth, 1, n6))


def _mod_row_spec(n_lat_tiles, tiles_per_batch, nb, width):
    def imap(i, *_):
        return (jnp.where(i < n_lat_tiles, i // tiles_per_batch, nb), 0, 0)
    return pl.BlockSpec((None, 1, width), imap)


def _modulated_norm(x, g, shift, scale):
    y = x * lax.rsqrt(jnp.mean(x * x, axis=-1, keepdims=True) + NORM_EPS)
    return (y * g) * (1.0 + scale) + shift


IN_CHUNK = 512


def _x_pair_specs(tm, d, n_lat_tiles, ctx_tile0):
    return [pl.BlockSpec((tm, d), lambda i: (jnp.minimum(i, n_lat_tiles - 1), 0)),
            pl.BlockSpec((tm, d), lambda i: (ctx_tile0 + jnp.maximum(i - n_lat_tiles, 0), 0))]


def _x_pair_tile(xa_ref, xb_ref, n_lat_tiles):
    return jnp.where(pl.program_id(0) < n_lat_tiles, xa_ref[...], xb_ref[...])


def _inproj_kernel(xa_ref, xb_ref, mod_ref, g_ref, cos_ref, sin_ref, wqk_ref, wmisc_ref, wgate_ref,
                   gq_ref, gk_ref, u_ref, nq_ref, nk_ref, nv_ref, gv_ref, gr_ref, cv_ref, ga_ref, gt_ref,
                   *, n_lat_tiles):
    d = xa_ref.shape[1]
    mod = mod_ref[...]
    x = _x_pair_tile(xa_ref, xb_ref, n_lat_tiles)
    h = _modulated_norm(x, g_ref[...], mod[:, 0:d], mod[:, d:2 * d]).astype(BF16)
    cos = cos_ref[...]
    sin = sin_ref[...]
    for j, o_ref in enumerate((gq_ref, gk_ref)):
        y = jnp.dot(h, wqk_ref[:, j * 512:(j + 1) * 512], preferred_element_type=F32)
        o_ref[...] = (y[:, :256] * cos + y[:, 256:] * sin).astype(o_ref.dtype)
    col = 0
    for o_ref in (u_ref, nq_ref, nk_ref, nv_ref, gv_ref, gr_ref, cv_ref, ga_ref):
        w = o_ref.shape[1]
        o_ref[...] = jnp.dot(h, wmisc_ref[:, col:col + w], preferred_element_type=F32).astype(o_ref.dtype)
        col += w
    for j in range(gt_ref.shape[1] // IN_CHUNK):
        sl = slice(j * IN_CHUNK, (j + 1) * IN_CHUNK)
        gt_ref[:, sl] = jnp.dot(h, wgate_ref[:, sl], preferred_element_type=F32).astype(gt_ref.dtype)


def in_projection(x_lat, x_ctx, ctx_tile0, modtab, norm_g, cos_tab, sin_tab, wqk, wmisc, wgate, *, n, nb, seq, tm):
    d = x_lat.shape[1]
    n_lat_tiles = nb * seq // tm
    tpb = seq // tm
    widths = (256, 256, 256, 256, 256, 256, 512, 512, 512, LANES, N_BRANCHES * d)
    dtypes = (BF16,) * 9 + (F32, BF16)

    def row(i):
        return (i, 0)

    def rope_row(i):
        return (jnp.where(i < n_lat_tiles, i % tpb, tpb), 0)

    return pl.pallas_call(
        functools.partial(_inproj_kernel, n_lat_tiles=n_lat_tiles),
        grid=(n // tm,),
        in_specs=_x_pair_specs(tm, d, n_lat_tiles, ctx_tile0) + [
            _mod_row_spec(n_lat_tiles, tpb, nb, modtab.shape[-1]),
            _resident((1, d)),
            pl.BlockSpec((tm, 256), rope_row),
            pl.BlockSpec((tm, 256), rope_row),
            _resident(wqk.shape),
            _resident(wmisc.shape),
            _resident(wgate.shape),
        ],
        out_specs=[pl.BlockSpec((tm, w), row) for w in widths],
        out_shape=[jax.ShapeDtypeStruct((n, w), dt) for w, dt in zip(widths, dtypes)],
        compiler_params=_cparams(("arbitrary",)),
        name="in_proj",
    )(x_lat, x_ctx, modtab, norm_g.reshape(1, d), cos_tab, sin_tab, wqk, wmisc, wgate)


def _s5_state_kernel(u_ref, w_ref, s_ref):
    s_ref[0] = jnp.dot(u_ref[0, 0], w_ref[0], preferred_element_type=F32)


def _s5_scan_kernel(s_ref, a_ref, h_ref, *, n_ctx_chunks):
    d = pl.program_id(0)
    nc = s_ref.shape[1]
    a1 = a_ref[0, 0]
    a2 = a_ref[0, 1]

    def step(s, h):
        fwd_row = s
        rev_row = jnp.where(s < n_ctx_chunks, n_ctx_chunks - 1 - s, nc - 1 - (s - n_ctx_chunks))
        r = jnp.where(d == 0, fwd_row, rev_row)
        h_ref[0, r] = h
        return h * a1 + pltpu.roll(h, S5_STATE, axis=1) * a2 + s_ref[0, r]

    lax.fori_loop(0, nc, step, jnp.zeros(h_ref.shape[2:], F32))


def _s5_out_kernel(u_ref, h_ref, t_ref, v_ref, y_ref):
    y = jnp.dot(u_ref[0, 0], t_ref[0], preferred_element_type=F32)
    y += jnp.dot(h_ref[0].astype(BF16), v_ref[0], preferred_element_type=F32)
    y_ref[0, 0] = y.astype(y_ref.dtype)


def s5_scan(u_gm, mats, *, n_ctx_chunks):
    g, nb, nc, cw = u_gm.shape
    t_sum, w_cat, v_cat, a12 = mats
    sw = 4 * S5_STATE
    s = pl.pallas_call(
        _s5_state_kernel,
        grid=(g, nb),
        in_specs=[pl.BlockSpec((1, 1, nc, cw), lambda i, b: (i, b, 0, 0)),
                  pl.BlockSpec((1, cw, sw), lambda i, b: (i, 0, 0))],
        out_specs=pl.BlockSpec((1, nc, sw), lambda i, b: (i * nb + b, 0, 0)),
        out_shape=jax.ShapeDtypeStruct((g * nb, nc, sw), F32),
        compiler_params=_cparams(("arbitrary", "arbitrary")),
        name="s5_chunk_state",
    )(u_gm, w_cat)
    s_t = jnp.transpose(s.reshape(g * nb, nc, 2, 2 * S5_STATE), (2, 1, 0, 3))
    pb = 16
    h_t = pl.pallas_call(
        functools.partial(_s5_scan_kernel, n_ctx_chunks=n_ctx_chunks),
        grid=(2, g * nb // pb),
        in_specs=[pl.BlockSpec((1, nc, pb, 2 * S5_STATE), lambda d, j: (d, 0, j, 0)),
                  pl.BlockSpec((1, 2, pb, 2 * S5_STATE), lambda d, j: (d, 0, j, 0))],
        out_specs=pl.BlockSpec((1, nc, pb, 2 * S5_STATE), lambda d, j: (d, 0, j, 0)),
        out_shape=jax.ShapeDtypeStruct(s_t.shape, F32),
        compiler_params=_cparams(("arbitrary", "arbitrary")),
        name="s5_chunk_scan",
    )(s_t, a12)
    h = jnp.transpose(h_t, (2, 1, 0, 3)).reshape(g * nb, nc, sw)
    return pl.pallas_call(
        _s5_out_kernel,
        grid=(g, nb),
        in_specs=[pl.BlockSpec((1, 1, nc, cw), lambda i, b: (i, b, 0, 0)),
                  pl.BlockSpec((1, nc, sw), lambda i, b: (i * nb + b, 0, 0)),
                  pl.BlockSpec((1, cw, cw), lambda i, b: (i, 0, 0)),
                  pl.BlockSpec((1, sw, cw), lambda i, b: (i, 0, 0))],
        out_specs=pl.BlockSpec((1, 1, nc, cw), lambda i, b: (i, b, 0, 0)),
        out_shape=jax.ShapeDtypeStruct(u_gm.shape, BF16),
        compiler_params=_cparams(("arbitrary", "arbitrary")),
        name="s5_readout",
    )(u_gm, h, t_sum, v_cat)


def s5_matrices(lam_re, lam_im, log_dt, b_re, b_im, c_re, c_im, d_skip, nb):
    ch = S5_CHUNK
    gsz = S5_GROUP_SIZE
    dt = jnp.exp(log_dt.astype(F32))[..., None]
    lr = lam_re.astype(F32)
    li = lam_im.astype(F32)

    def power(n):
        n = n.astype(F32)[:, None, None, None]
        mag = jnp.exp(lr * dt * n)
        return mag * jnp.cos(li * dt * n), mag * jnp.sin(li * dt * n)

    ab_re, ab_im = power(jnp.ones((1,), F32))
    ab_re, ab_im = ab_re[0], ab_im[0]
    den = lr * lr + li * li
    nr = ab_re - 1.0
    ni = ab_im
    coef_re = (nr * lr + ni * li) / den
    coef_im = (ni * lr - nr * li) / den
    br = b_re.astype(F32)
    bi = b_im.astype(F32)
    bb_re = coef_re[..., None] * br - coef_im[..., None] * bi
    bb_im = coef_re[..., None] * bi + coef_im[..., None] * br
    cr = c_re.astype(F32)
    ci = c_im.astype(F32)

    p_re, p_im = power(jnp.arange(ch + 1))
    ca_re = cr[None] * p_re[:, :, :, None, :] - ci[None] * p_im[:, :, :, None, :]
    ca_im = cr[None] * p_im[:, :, :, None, :] + ci[None] * p_re[:, :, :, None, :]
    kmat = (jnp.einsum('ndgip,dgpj->ndgij', ca_re[:ch], bb_re, precision=HIGHEST)
            - jnp.einsum('ndgip,dgpj->ndgij', ca_im[:ch], bb_im, precision=HIGHEST))
    kf = jnp.moveaxis(kmat[:, 0], 0, -1)
    kr = jnp.moveaxis(kmat[:, 1], 0, -1)
    k2 = jnp.concatenate([kf[..., :1] + kr[..., :1], kf[..., 1:], jnp.zeros_like(kf[..., :1]),
                          jnp.flip(kr[..., 1:], axis=-1)], axis=-1)
    toep = jnp.tile(k2, ch)[..., :ch * (2 * ch - 1)].reshape(k2.shape[:-1] + (ch, 2 * ch - 1))[..., :ch]
    t_sum = jnp.transpose(toep, (0, 3, 2, 4, 1))
    skip = jnp.eye(ch * gsz, dtype=F32)[None] * jnp.tile(d_skip.astype(F32).reshape(S5_GROUPS, 1, gsz), (1, ch, 1)).reshape(S5_GROUPS, 1, ch * gsz)
    t_sum = t_sum.reshape(S5_GROUPS, ch * gsz, ch * gsz) + skip

    def w_dir(d, pr, pi):
        wr = pr[..., None] * bb_re[d][None] - pi[..., None] * bb_im[d][None]
        wi = pr[..., None] * bb_im[d][None] + pi[..., None] * bb_re[d][None]
        w = jnp.concatenate([wr, wi], axis=2)
        return jnp.transpose(w, (1, 0, 3, 2)).reshape(S5_GROUPS, ch * gsz, 2 * S5_STATE)
    w_cat = jnp.concatenate([w_dir(0, jnp.flip(p_re[:ch, 0], 0), jnp.flip(p_im[:ch, 0], 0)),
                             w_dir(1, p_re[:ch, 1], p_im[:ch, 1])], axis=-1)

    def v_dir(vr, vi):
        v = jnp.concatenate([vr, vi], axis=-1)
        return jnp.transpose(v, (1, 3, 0, 2)).reshape(S5_GROUPS, 2 * S5_STATE, ch * gsz)
    v_cat = jnp.concatenate([v_dir(ca_re[1:, 0], -ca_im[1:, 0]),
                             v_dir(jnp.flip(ca_re[1:, 1], 0), -jnp.flip(ca_im[1:, 1], 0))], axis=1)

    a1 = jnp.concatenate([p_re[ch], p_re[ch]], axis=-1)
    a2 = jnp.concatenate([-p_im[ch], p_im[ch]], axis=-1)
    a12 = jnp.stack([a1, a2], axis=1)
    a12 = jnp.repeat(a12, nb, axis=2)
    return t_sum.astype(BF16), w_cat.astype(BF16), v_cat.astype(BF16), a12


def _gla_direction(q, k, v, z, s_ref, reverse, consts):
    tri_f, tri_r, head_mask, block_mask, ones_cols = consts
    c = q.shape[0]
    g = (jnp.minimum(z, 0.0) - jnp.log(1.0 + jnp.exp(-jnp.abs(z)))) * (1.0 / GLA_TAU)
    tri = tri_r if reverse else tri_f
    b = jnp.dot(tri, g, preferred_element_type=F32, precision=HIGHEST)
    b_last = b[0:1] if reverse else b[c - 1:c]
    q_t = q * jnp.exp(b)
    k_t = (k * jnp.exp(-b)).astype(BF16)
    k_end = (k * jnp.exp(b_last - b)).astype(BF16)
    s_old = s_ref[...]
    o = jnp.dot(q_t.astype(BF16), s_old.astype(BF16), preferred_element_type=F32)
    outs = []
    for h in range(GLA_HEADS):
        qh = (q_t * head_mask[h:h + 1]).astype(BF16)
        att = lax.dot_general(qh, k_t, (((1,), (1,)), ((), ())), preferred_element_type=F32)
        att = att * tri
        vh = v[:, h * GLA_DV:(h + 1) * GLA_DV]
        outs.append(jnp.dot(att.astype(BF16), vh, preferred_element_type=F32))
    o = o + jnp.concatenate(outs, axis=1)
    kv = lax.dot_general(k_end, v, (((0,), (0,)), ((), ())), preferred_element_type=F32)
    tot = lax.dot_general(g, ones_cols, (((0,), (0,)), ((), ())), preferred_element_type=F32, precision=HIGHEST)
    decay = jnp.exp(tot)
    decay = jnp.concatenate([decay] * (s_old.shape[1] // LANES), axis=1)
    s_ref[...] = (decay * s_old + kv) * block_mask
    return o


def _gla_kernel(qf_ref, kf_ref, vf_ref, af_ref, qr_ref, kr_ref, vr_ref, ar_ref, wa_ref, ba_ref,
                of_ref, or_ref, sf_ref, sr_ref):
    @pl.when(pl.program_id(1) == 0)
    def _():
        sf_ref[...] = jnp.zeros_like(sf_ref)
        sr_ref[...] = jnp.zeros_like(sr_ref)

    c = GLA_CHUNK
    hk = GLA_HEADS * GLA_DK
    hv = GLA_HEADS * GLA_DV
    ri = lax.broadcasted_iota(jnp.int32, (c, c), 0)
    ci = lax.broadcasted_iota(jnp.int32, (c, c), 1)
    tri_f = (ri >= ci).astype(F32)
    tri_r = (ri <= ci).astype(F32)
    lane = lax.broadcasted_iota(jnp.int32, (GLA_HEADS, hk), 1)
    hrow = lax.broadcasted_iota(jnp.int32, (GLA_HEADS, hk), 0)
    head_mask = (lane // GLA_DK == hrow).astype(F32)
    br = lax.broadcasted_iota(jnp.int32, (hk, hv), 0) // GLA_DK
    bc = lax.broadcasted_iota(jnp.int32, (hk, hv), 1) // GLA_DV
    block_mask = (br == bc).astype(F32)
    ones_cols = jnp.ones((c, LANES), F32)
    consts = (tri_f, tri_r, head_mask, block_mask, ones_cols)
    wa = wa_ref[...]
    ba = ba_ref[...]
    nchunks = qf_ref.shape[0] // c
    for j in range(nchunks):
        sl = slice(j * c, (j + 1) * c)
        z = jnp.dot(af_ref[sl, :], wa[:, :hk], preferred_element_type=F32, precision=HIGHEST) + ba[:, :hk]
        o = _gla_direction(qf_ref[sl, :].astype(F32), kf_ref[sl, :].astype(F32), vf_ref[sl, :], z,
                           sf_ref, False, consts)
        of_ref[sl, :] = o.astype(of_ref.dtype)
    for j in reversed(range(nchunks)):
        sl = slice(j * c, (j + 1) * c)
        z = jnp.dot(ar_ref[sl, :], wa[:, hk:], preferred_element_type=F32, precision=HIGHEST) + ba[:, hk:]
        o = _gla_direction(qr_ref[sl, :].astype(F32), kr_ref[sl, :].astype(F32), vr_ref[sl, :], z,
                           sr_ref, True, consts)
        or_ref[sl, :] = o.astype(or_ref.dtype)


def gla_scan(gq, gk, gv, ga, wa, ba, *, nb, seq, ctx_len):
    n = gq.shape[0]
    blk = SEQ_BLOCK
    assert ctx_len == blk
    lpb = seq // blk
    ctx0 = nb * lpb

    def fwd(b, s):
        return (jnp.where(s == 0, ctx0 + b, b * lpb + s - 1), 0)

    def rev(b, s):
        return (jnp.where(s == 0, ctx0 + b, b * lpb + lpb - s), 0)

    hk = GLA_HEADS * GLA_DK
    hv = GLA_HEADS * GLA_DV
    specs = []
    for imap in (fwd, rev):
        specs += [pl.BlockSpec((blk, hk), imap), pl.BlockSpec((blk, hk), imap),
                  pl.BlockSpec((blk, hv), imap), pl.BlockSpec((blk, LANES), imap)]
    specs += [_resident(wa.shape), _resident(ba.shape)]
    return pl.pallas_call(
        _gla_kernel,
        grid=(nb, lpb + 1),
        in_specs=specs,
        out_specs=[pl.BlockSpec((blk, hv), fwd), pl.BlockSpec((blk, hv), rev)],
        out_shape=[jax.ShapeDtypeStruct((n, hv), BF16)] * 2,
        scratch_shapes=[pltpu.VMEM((hk, hv), F32), pltpu.VMEM((hk, hv), F32)],
        compiler_params=_cparams(("arbitrary", "arbitrary")),
        name="gla_scan",
    )(gq, gk, gv, ga, gq, gk, gv, ga, wa, ba)


NA_ROWS_PER_STEP = 4


def _na_kernel(q_ref, k_ref, v_ref, kc_ref, vc_ref, bias_ref, o_ref, *, n_rows):
    step = pl.program_id(1)
    hd = NA_HEADS * NA_HEAD_DIM
    lane = lax.broadcasted_iota(jnp.int32, (NA_HEADS, hd), 1)
    hrow = lax.broadcasted_iota(jnp.int32, (NA_HEADS, hd), 0)
    head_mask = (lane // NA_HEAD_DIM == hrow).astype(F32)
    kc = kc_ref[...]
    vc = vc_ref[...]
    win = NA_WIN_ROWS * GRID_W
    for i in range(NA_ROWS_PER_STEP):
        r = step * NA_ROWS_PER_STEP + i
        rs = jnp.clip(r - NA_WIN_ROWS // 2, 0, n_rows - NA_WIN_ROWS)
        delta = rs - r + NA_WIN_ROWS - 1
        start = pl.multiple_of(rs * GRID_W, GRID_W)
        kw = k_ref[pl.ds(start, win), :]
        vw = v_ref[pl.ds(start, win), :]
        q = q_ref[i * GRID_W:(i + 1) * GRID_W, :].astype(F32)
        acc = jnp.zeros((GRID_W, hd), F32)
        for h in range(NA_HEADS):
            m_h = head_mask[h:h + 1]
            qh = (q * m_h).astype(BF16)
            s_lat = lax.dot_general(qh, kw, (((1,), (1,)), ((), ())), preferred_element_type=F32)
            s_lat = s_lat + bias_ref[delta, h]
            s_ctx = lax.dot_general(qh, kc, (((1,), (1,)), ((), ())), preferred_element_type=F32)
            m = jnp.maximum(jnp.max(s_lat, axis=1, keepdims=True), jnp.max(s_ctx, axis=1, keepdims=True))
            p_lat = jnp.exp(s_lat - m)
            p_ctx = jnp.exp(s_ctx - m)
            den = jnp.sum(p_lat, axis=1, keepdims=True) + jnp.sum(p_ctx, axis=1, keepdims=True)
            o = jnp.dot(p_lat.astype(BF16), vw, preferred_element_type=F32)
            o += jnp.dot(p_ctx.astype(BF16), vc, preferred_element_type=F32)
            acc += (o / den) * m_h
        o_ref[i * GRID_W:(i + 1) * GRID_W, :] = acc.astype(o_ref.dtype)


def _na_ctx_kernel(q_ref, k_ref, v_ref, o_ref):
    hd = NA_HEADS * NA_HEAD_DIM
    lane = lax.broadcasted_iota(jnp.int32, (NA_HEADS, hd), 1)
    hrow = lax.broadcasted_iota(jnp.int32, (NA_HEADS, hd), 0)
    head_mask = (lane // NA_HEAD_DIM == hrow).astype(F32)
    q = q_ref[...].astype(F32)
    k = k_ref[...]
    v = v_ref[...]
    acc = jnp.zeros(q.shape, F32)
    for h in range(NA_HEADS):
        m_h = head_mask[h:h + 1]
        s = lax.dot_general((q * m_h).astype(BF16), k, (((1,), (1,)), ((), ())), preferred_element_type=F32)
        p = jnp.exp(s - jnp.max(s, axis=1, keepdims=True))
        o = jnp.dot(p.astype(BF16), v, preferred_element_type=F32) / jnp.sum(p, axis=1, keepdims=True)
        acc += o * m_h
    o_ref[...] = acc.astype(o_ref.dtype)


def neighbourhood_attention(nq, nk, nv, bias, *, nb, seq, ctx_len):
    n, hd = nq.shape
    n_rows = seq // GRID_W
    qb = NA_ROWS_PER_STEP * GRID_W
    steps = seq // qb
    ctx0 = nb * seq // ctx_len
    o_lat = pl.pallas_call(
        functools.partial(_na_kernel, n_rows=n_rows),
        grid=(nb, steps),
        in_specs=[pl.BlockSpec((qb, hd), lambda b, s: (b * steps + s, 0)),
                  pl.BlockSpec((seq, hd), lambda b, s: (b, 0)),
                  pl.BlockSpec((seq, hd), lambda b, s: (b, 0)),
                  pl.BlockSpec((ctx_len, hd), lambda b, s: (ctx0 + b, 0)),
                  pl.BlockSpec((ctx_len, hd), lambda b, s: (ctx0 + b, 0)),
                  _resident(bias.shape)],
        out_specs=pl.BlockSpec((qb, hd), lambda b, s: (b * steps + s, 0)),
        out_shape=jax.ShapeDtypeStruct((nb * seq, hd), BF16),
        compiler_params=_cparams(("arbitrary", "arbitrary")),
        name="na_latent",
    )(nq, nk, nv, nk, nv, bias)
    o_ctx = pl.pallas_call(
        _na_ctx_kernel,
        grid=(nb,),
        in_specs=[pl.BlockSpec((ctx_len, hd), lambda b: (ctx0 + b, 0))] * 3,
        out_specs=pl.BlockSpec((ctx_len, hd), lambda b: (b, 0)),
        out_shape=jax.ShapeDtypeStruct((nb * ctx_len, hd), BF16),
        compiler_params=_cparams(("arbitrary",)),
        name="na_context",
    )(nq, nk, nv)
    return jnp.concatenate([o_lat, o_ctx], axis=0)


def na_bias_tables(rpb):
    rpb = rpb.astype(F32)
    c_idx = jnp.arange(GRID_W)
    col_start = jnp.clip(c_idx - NA_WIN_COLS // 2, 0, GRID_W - NA_WIN_COLS)
    col_in = (c_idx[None, :] >= col_start[:, None]) & (c_idx[None, :] < col_start[:, None] + NA_WIN_COLS)
    b = jnp.stack([rpb[:, dl:dl + NA_WIN_ROWS] for dl in range(NA_WIN_ROWS)], axis=1)
    wc = NA_WIN_COLS
    pad = jnp.zeros(b.shape[:-1] + (2 * GRID_W - (2 * wc - 1),), F32)
    table = jnp.concatenate([b[..., wc - 1:], pad, b[..., :wc - 1]], axis=-1)
    b = jnp.tile(table, GRID_W)[..., :GRID_W * (2 * GRID_W - 1)]
    b = b.reshape(table.shape[:-1] + (GRID_W, 2 * GRID_W - 1))[..., :GRID_W]
    b = jnp.where(col_in[None, None, None], b, NEG_BIG)
    b = jnp.transpose(b, (1, 0, 3, 2, 4))
    return b.reshape(NA_WIN_ROWS, NA_HEADS, GRID_W, NA_WIN_ROWS * GRID_W)


def _conv_kernel(prev_ref, main_ref, next_ref, dw_ref, dwb_ref, lng_ref, lnb_ref, o_ref, buf_ref,
                 *, n_lat_tiles, tiles_per_batch):
    i = pl.program_id(0)
    j = i % tiles_per_batch
    is_lat = i < n_lat_tiles
    has_prev = jnp.logical_and(is_lat, j > 0)
    has_next = jnp.logical_and(is_lat, j < tiles_per_batch - 1)
    cw = CONV_WIDTH
    tl = main_ref.shape[0]

    def glu(a):
        a = a.astype(F32)
        return a[:, :cw] * _sigmoid(a[:, cw:])

    buf_ref[0:CONV_HALO, :] = glu(prev_ref[...]) * has_prev.astype(F32)
    buf_ref[CONV_HALO:CONV_HALO + tl, :] = glu(main_ref[...])
    buf_ref[CONV_HALO + tl:, :] = glu(next_ref[...]) * has_next.astype(F32)
    dw = dw_ref[...]
    acc = jnp.zeros((tl, cw), F32) + dwb_ref[...]
    base = CONV_HALO - CONV_KERNEL // 2
    for k in range(CONV_KERNEL):
        acc += buf_ref[base + k:base + k + tl, :] * dw[k:k + 1, :]
    mu = jnp.mean(acc, axis=-1, keepdims=True)
    xc = acc - mu
    y = xc * lax.rsqrt(jnp.mean(xc * xc, axis=-1, keepdims=True) + NORM_EPS)
    y = y * lng_ref[...] + lnb_ref[...]
    o_ref[...] = _silu(y).astype(o_ref.dtype)


def conv_branch(cv, dw, dw_b, ln_g, ln_b, *, nb, seq):
    n = cv.shape[0]
    tl = SEQ_BLOCK
    hb = tl // CONV_HALO
    n_tiles = n // tl
    cw = CONV_WIDTH
    return pl.pallas_call(
        functools.partial(_conv_kernel, n_lat_tiles=nb * seq // tl, tiles_per_batch=seq // tl),
        grid=(n_tiles,),
        in_specs=[pl.BlockSpec((CONV_HALO, 2 * cw), lambda i: (jnp.maximum(i * hb - 1, 0), 0)),
                  pl.BlockSpec((tl, 2 * cw), lambda i: (i, 0)),
                  pl.BlockSpec((CONV_HALO, 2 * cw), lambda i: (jnp.minimum((i + 1) * hb, n_tiles * hb - 1), 0)),
                  _resident((CONV_KERNEL, cw)), _resident((1, cw)), _resident((1, cw)), _resident((1, cw))],
        out_specs=pl.BlockSpec((tl, cw), lambda i: (i, 0)),
        out_shape=jax.ShapeDtypeStruct((n, cw), BF16),
        scratch_shapes=[pltpu.VMEM((tl + 2 * CONV_HALO, cw), F32)],
        compiler_params=_cparams(("arbitrary",)),
        name="conv_branch",
    )(cv, cv, cv, dw.astype(F32), dw_b.reshape(1, cw).astype(F32), ln_g.reshape(1, cw).astype(F32),
      ln_b.reshape(1, cw).astype(F32))


def _merge_kernel(xa_ref, xb_ref, mod_ref, gt_ref, ys5_ref, of_ref, or_ref, gr_ref, na_ref, cv_ref,
                  gate_b_ref, wglu_ref, bglu_ref, ws5_ref, gng_ref, wgla_ref, wna_ref, wcv_ref, wmix_ref, o_ref,
                  *, n_lat_tiles):
    d = xa_ref.shape[1]
    z = jax.nn.gelu(ys5_ref[...].astype(F32))
    z = z * _sigmoid(jnp.dot(z.astype(BF16), wglu_ref[...], preferred_element_type=F32) + bglu_ref[...])
    br_s5 = jnp.dot(z.astype(BF16), ws5_ref[...], preferred_element_type=F32)
    o = of_ref[...].astype(F32) + or_ref[...].astype(F32)
    r = _silu(gr_ref[...].astype(F32))
    parts = []
    for h in range(GLA_HEADS):
        oh = o[:, h * GLA_DV:(h + 1) * GLA_DV]
        oh = oh * lax.rsqrt(jnp.mean(oh * oh, axis=-1, keepdims=True) + NORM_EPS) * gng_ref[...]
        parts.append(oh * r[:, h * GLA_DV:(h + 1) * GLA_DV])
    y_gla = jnp.concatenate(parts, axis=1).astype(BF16)
    br_gla = jnp.dot(y_gla, wgla_ref[...], preferred_element_type=F32)
    br_na = jnp.dot(na_ref[...], wna_ref[...], preferred_element_type=F32)
    br_cv = jnp.dot(cv_ref[...], wcv_ref[...], preferred_element_type=F32)
    merged = jnp.zeros((xa_ref.shape[0], d), F32)
    for i, br in enumerate((br_s5, br_gla, br_na, br_cv)):
        gate = _sigmoid(gt_ref[:, i * d:(i + 1) * d].astype(F32) + gate_b_ref[:, i * d:(i + 1) * d])
        merged += gate * br
    mix = jnp.dot(merged.astype(BF16), wmix_ref[...], preferred_element_type=F32)
    o_ref[...] = _x_pair_tile(xa_ref, xb_ref, n_lat_tiles) + mod_ref[:, 2 * d:3 * d] * mix


def merge_branches(x_lat, x_ctx, ctx_tile0, modtab, gt, ys5, o_f, o_r, gr, o_na, y_cv, weights,
                   *, n_rows, nb, seq, tm):
    d = x_lat.shape[1]
    n_lat_tiles = nb * seq // tm
    tpb = seq // tm

    def row(i):
        return (i, 0)

    acts = (gt, ys5, o_f, o_r, gr, o_na, y_cv)
    return pl.pallas_call(
        functools.partial(_merge_kernel, n_lat_tiles=n_lat_tiles),
        grid=(n_rows // tm,),
        in_specs=_x_pair_specs(tm, d, n_lat_tiles, ctx_tile0)
        + [_mod_row_spec(n_lat_tiles, tpb, nb, modtab.shape[-1])]
        + [pl.BlockSpec((tm, a.shape[1]), row) for a in acts]
        + [_resident(w.shape) for w in weights],
        out_specs=pl.BlockSpec((tm, d), row),
        out_shape=jax.ShapeDtypeStruct((n_rows, d), F32),
        compiler_params=_cparams(("arbitrary",)),
        name="merge_mix",
    )(x_lat, x_ctx, modtab, *acts, *weights)


MOE_BLOCK = 128


def _route_kernel(x_ref, mod_ref, g_ref, wr_ref, br_ref, ltri_ref, h_ref, comb_ref, meta_ref, cnt_ref):
    d = x_ref.shape[1]
    mod = mod_ref[...]
    h = _modulated_norm(x_ref[...], g_ref[...], mod[:, 3 * d:4 * d], mod[:, 4 * d:5 * d])
    h_ref[...] = h.astype(BF16)
    logits = jnp.dot(h, wr_ref[...], preferred_element_type=F32, precision=HIGHEST) + br_ref[...]
    lane = lax.broadcasted_iota(jnp.int32, logits.shape, 1)
    big = jnp.int32(1 << 20)
    is_g = lane < MOE_GROUPS
    gl = jnp.where(is_g, logits, -jnp.inf)
    gmax = jnp.max(gl, axis=1, keepdims=True)
    gidx = jnp.min(jnp.where(gl == gmax, lane, big), axis=1, keepdims=True)
    group_p = 1.0 / jnp.sum(jnp.where(is_g, jnp.exp(logits - gmax), 0.0), axis=1, keepdims=True)
    first = MOE_GROUPS + gidx * MOE_EXPERTS_PER_GROUP
    in_group = jnp.logical_and(lane >= first, lane < first + MOE_EXPERTS_PER_GROUP)
    el = jnp.where(in_group, logits, -jnp.inf)
    v1 = jnp.max(el, axis=1, keepdims=True)
    i1 = jnp.min(jnp.where(el == v1, lane, big), axis=1, keepdims=True)
    el2 = jnp.where(lane == i1, -jnp.inf, el)
    v2 = jnp.max(el2, axis=1, keepdims=True)
    i2 = jnp.min(jnp.where(el2 == v2, lane, big), axis=1, keepdims=True)
    t = jnp.exp(v2 - v1)
    w1 = group_p / (1.0 + t)
    w2 = group_p * t / (1.0 + t)
    k1 = i1 - first
    k2 = i2 - first
    epg = MOE_EXPERTS_PER_GROUP
    comb = (jnp.where(jnp.logical_or(lane == k1, lane == k1 + epg), w1, 0.0)
            + jnp.where(jnp.logical_or(lane == k2, lane == k2 + epg), w2, 0.0))
    head = comb.astype(BF16).astype(F32)
    comb_ref[...] = jnp.where(lane < epg, head, comb - head).astype(BF16)
    onehot = (lane == gidx).astype(BF16)
    rank_all = jnp.dot(ltri_ref[...], onehot, preferred_element_type=F32)
    rank = jnp.sum(jnp.where(lane == gidx, rank_all, 0.0), axis=1, keepdims=True)
    packed = jnp.where(lane == 0, gidx.astype(F32), jnp.where(lane == 1, rank, 0.0))
    meta_ref[0] = jnp.transpose(packed)[0:8, :]
    counts = jnp.sum(onehot.astype(F32), axis=0, keepdims=True)
    cnt_ref[0] = jnp.broadcast_to(counts, cnt_ref.shape[1:]).astype(jnp.int32)


def _moe_kernel(cnt_ref, x_ref, mod_ref, h_ref, comb_ref, meta_ref, w1_ref, w3_ref, w2_ref, fg_ref, o_ref,
                *, final):
    t = pl.program_id(0)
    g = pl.program_id(1)
    tm, d = x_ref.shape

    @pl.when(g == 0)
    def _():
        o_ref[...] = jnp.zeros_like(o_ref)

    n_tok = cnt_ref[t * MOE_GROUPS + g]
    gid_row = meta_ref[0, 0:1, :]
    rank_row = meta_ref[0, 1:2, :]
    in_grp = gid_row == g.astype(F32)
    row = lax.broadcasted_iota(jnp.int32, (MOE_BLOCK, tm), 0).astype(F32)

    def block(blk, carry):
        sel = jnp.logical_and(in_grp, rank_row == row + (blk * MOE_BLOCK).astype(F32))
        p = sel.astype(BF16)
        xg = jnp.dot(p, h_ref[...], preferred_element_type=F32).astype(BF16)
        cw = jnp.dot(p, comb_ref[...], preferred_element_type=F32)
        yg = jnp.zeros((MOE_BLOCK, d), F32)
        for e in range(MOE_EXPERTS_PER_GROUP):
            a = jnp.dot(xg, w1_ref[0, e], preferred_element_type=F32)
            b = jnp.dot(xg, w3_ref[0, e], preferred_element_type=F32)
            cw_e = cw[:, e:e + 1] + cw[:, MOE_EXPERTS_PER_GROUP + e:MOE_EXPERTS_PER_GROUP + e + 1]
            act = (_silu(a) * b * cw_e).astype(BF16)
            yg += jnp.dot(act, w2_ref[0, e], preferred_element_type=F32)
        o_ref[...] += lax.dot_general(p, yg.astype(BF16), (((0,), (0,)), ((), ())), preferred_element_type=F32)
        return carry

    lax.fori_loop(0, (n_tok + MOE_BLOCK - 1) // MOE_BLOCK, block, 0)

    @pl.when(g == pl.num_programs(1) - 1)
    def _():
        y = x_ref[...] + mod_ref[:, 5 * d:6 * d] * o_ref[...]
        if final:
            y = y * lax.rsqrt(jnp.mean(y * y, axis=-1, keepdims=True) + NORM_EPS) * fg_ref[...]
        o_ref[...] = y


def moe_layer(xs, modtab, norm_g, w_router, b_router, w1, w3, w2, final_g, *, n_rows, nb, seq, tm, final):
    d = xs.shape[1]
    n_lat_tiles = nb * seq // tm
    tpb = seq // tm
    n_tiles = n_rows // tm
    ltri = (lax.broadcasted_iota(jnp.int32, (tm, tm), 0) > lax.broadcasted_iota(jnp.int32, (tm, tm), 1)).astype(BF16)
    mod_spec = _mod_row_spec(n_lat_tiles, tpb, nb, modtab.shape[-1])
    h, comb, meta, cnt = pl.pallas_call(
        _route_kernel,
        grid=(n_tiles,),
        in_specs=[pl.BlockSpec((tm, d), lambda i: (i, 0)), mod_spec,
                  _resident((1, d)), _resident(w_router.shape), _resident(b_router.shape), _resident(ltri.shape)],
        out_specs=[pl.BlockSpec((tm, d), lambda i: (i, 0)), pl.BlockSpec((tm, LANES), lambda i: (i, 0)),
                   pl.BlockSpec((1, 8, tm), lambda i: (i, 0, 0)), pl.BlockSpec((1, 8, LANES), lambda i: (i, 0, 0))],
        out_shape=[jax.ShapeDtypeStruct((n_rows, d), BF16), jax.ShapeDtypeStruct((n_rows, LANES), BF16),
                   jax.ShapeDtypeStruct((n_tiles, 8, tm), F32), jax.ShapeDtypeStruct((n_tiles, 8, LANES), jnp.int32)],
        compiler_params=_cparams(("arbitrary",)),
        name="moe_route",
    )(xs, modtab, norm_g.reshape(1, d), w_router, b_router, ltri)
    counts = cnt[:, 0, :MOE_GROUPS].reshape(n_tiles * MOE_GROUPS)
    epg, hid = w1.shape[1], w1.shape[-1]
    grid_spec = pltpu.PrefetchScalarGridSpec(
        num_scalar_prefetch=1,
        grid=(n_tiles, MOE_GROUPS),
        in_specs=[pl.BlockSpec((tm, d), lambda i, g, c: (i, 0)),
                  pl.BlockSpec((None, 1, modtab.shape[-1]),
                               lambda i, g, c: (jnp.where(i < n_lat_tiles, i // tpb, nb), 0, 0)),
                  pl.BlockSpec((tm, d), lambda i, g, c: (i, 0)),
                  pl.BlockSpec((tm, LANES), lambda i, g, c: (i, 0)),
                  pl.BlockSpec((1, 8, tm), lambda i, g, c: (i, 0, 0)),
                  pl.BlockSpec((1, epg, d, hid), lambda i, g, c: (g, 0, 0, 0)),
                  pl.BlockSpec((1, epg, d, hid), lambda i, g, c: (g, 0, 0, 0)),
                  pl.BlockSpec((1, epg, hid, d), lambda i, g, c: (g, 0, 0, 0)),
                  pl.BlockSpec((1, d), lambda i, g, c: (0, 0))],
        out_specs=pl.BlockSpec((tm, d), lambda i, g, c: (i, 0)),
    )
    return pl.pallas_call(
        functools.partial(_moe_kernel, final=final),
        grid_spec=grid_spec,
        out_shape=jax.ShapeDtypeStruct((n_rows, d), F32),
        compiler_params=_cparams(("arbitrary", "arbitrary")),
        name="moe_experts",
    )(counts, xs, modtab, h, comb, meta, w1, w3, w2, final_g.reshape(1, d).astype(F32))


def rope_tables(seq, pad_rows):
    t = jnp.arange(seq, dtype=jnp.int32)
    row = (t // GRID_W).astype(F32)
    colp = (t % GRID_W).astype(F32)
    half = GLA_DK // 2
    inv_freq = ROPE_BASE ** (-jnp.arange(0, half, 2, dtype=F32) / half)
    ang_r = row[:, None] * inv_freq
    ang_c = colp[:, None] * inv_freq
    dd = jnp.arange(GLA_DK)
    ang = jnp.where((dd < half)[None, :], ang_r[:, dd % (half // 2)], ang_c[:, dd % (half // 2)])
    sign = jnp.where((dd % half) < half // 2, -1.0, 1.0).astype(F32)
    cos = jnp.tile(jnp.cos(ang), (1, GLA_HEADS))
    sin = jnp.tile(jnp.sin(ang) * sign[None, :], (1, GLA_HEADS))
    cos = jnp.concatenate([cos, jnp.ones((pad_rows, cos.shape[1]), F32)], axis=0)
    sin = jnp.concatenate([sin, jnp.zeros((pad_rows, sin.shape[1]), F32)], axis=0)
    return cos, sin


def split_in_weights(w_in, d):
    widths = (S5_WIDTH, GLA_HEADS * GLA_DK, GLA_HEADS * GLA_DK, GLA_HEADS * GLA_DV, GLA_HEADS * GLA_DV,
              2 * GLA_GATE_RANK, NA_HEADS * NA_HEAD_DIM, NA_HEADS * NA_HEAD_DIM, NA_HEADS * NA_HEAD_DIM,
              2 * CONV_WIDTH, N_BRANCHES * d)
    names = ('u', 'gq', 'gk', 'gv', 'gr', 'ga', 'nq', 'nk', 'nv', 'cv', 'gt')
    parts = {}
    col = 0
    for nme, w in zip(names, widths):
        parts[nme] = w_in[:, col:col + w]
        col += w
    swap = jnp.arange(GLA_HEADS * GLA_DK) ^ (GLA_DK // 4)
    gq = parts['gq'] * (GLA_DK ** -0.5)
    wqk = jnp.concatenate([gq, gq[:, swap], parts['gk'], parts['gk'][:, swap]], axis=1)
    ga = jnp.pad(parts['ga'], ((0, 0), (0, LANES - 2 * GLA_GATE_RANK)))
    wmisc = jnp.concatenate([parts['u'], parts['nq'] * (NA_HEAD_DIM ** -0.5), parts['nk'], parts['nv'],
                             parts['gv'], parts['gr'], parts['cv'], ga], axis=1)
    return wqk.astype(BF16), wmisc.astype(BF16), parts['gt'].astype(BF16)


def _to_group_major(u, nb, seq, ctx_len):
    ch = S5_CHUNK
    lat = u[:nb * seq].reshape(nb, seq // ch, ch, S5_GROUPS, S5_GROUP_SIZE)
    ctx = u[nb * seq:].reshape(nb, ctx_len // ch, ch, S5_GROUPS, S5_GROUP_SIZE)
    a = jnp.concatenate([ctx, lat], axis=1)
    a = jnp.transpose(a, (3, 0, 1, 2, 4))
    return a.reshape(S5_GROUPS, nb, (seq + ctx_len) // ch, ch * S5_GROUP_SIZE)


def _from_group_major(y, nb, seq, ctx_len):
    ch = S5_CHUNK
    nc = (seq + ctx_len) // ch
    a = y.reshape(S5_GROUPS, nb, nc, ch, S5_GROUP_SIZE)
    a = jnp.transpose(a, (1, 2, 3, 0, 4)).reshape(nb, nc * ch, S5_WIDTH)
    return jnp.concatenate([a[:, ctx_len:].reshape(nb * seq, S5_WIDTH),
                            a[:, :ctx_len].reshape(nb * ctx_len, S5_WIDTH)], axis=0)


def kernel(x, c, ctx, c_ctx, norm1_g, norm2_g, w_mod, b_mod, w_in, gate_b, w_mix_out, s5_lam_re, s5_lam_im, s5_log_dt, s5_b_re, s5_b_im, s5_c_re, s5_c_im, s5_d, s5_w_glu, s5_b_glu, s5_w_out, gla_w_a2, gla_b_a, gla_norm_g, gla_w_out, na_rpb, na_w_out, conv_dw, conv_dw_b, conv_ln_g, conv_ln_b, conv_w_out, moe_w_group, moe_b_group, moe_w_expert, moe_b_expert, moe_w1, moe_w3, moe_w2, final_norm_g):
    nb, seq, d = x.shape
    ctx_len = ctx.shape[1]
    depth = w_mod.shape[0]
    n_lat = nb * seq
    tm = 512
    tm_moe = math.gcd(1024, nb * ctx_len)
    assert ctx_len == SEQ_BLOCK and seq % tm_moe == 0 and (nb * ctx_len) % tm_moe == 0 and nb < MOD_ROWS

    n_all = n_lat + nb * ctx_len
    x_lat, x_ctx, ctx_tile0 = x.reshape(n_lat, d).astype(F32), ctx.reshape(nb * ctx_len, d).astype(F32), 0
    c_rows = jnp.zeros((MOD_ROWS, d), F32).at[:nb].set(c.astype(F32)).at[nb].set(c_ctx.astype(F32))
    modtab = modulation_table(c_rows, w_mod.astype(F32), b_mod.astype(F32))
    modtab = modtab.reshape(depth, MOD_ROWS, 1, 6 * d)
    cos_tab, sin_tab = rope_tables(seq, tm)

    for i in range(depth):
        last = i == depth - 1
        n_rows = n_lat if last else n_all
        wqk, wmisc, wgate = split_in_weights(w_in[i], d)
        gq, gk, u, nq, nk, nv, gv, gr, cv, ga, gt = in_projection(
            x_lat, x_ctx, ctx_tile0, modtab[i], norm1_g[i].astype(F32), cos_tab, sin_tab, wqk, wmisc, wgate,
            n=n_all, nb=nb, seq=seq, tm=tm)

        mats = s5_matrices(s5_lam_re[i], s5_lam_im[i], s5_log_dt[i], s5_b_re[i], s5_b_im[i],
                           s5_c_re[i], s5_c_im[i], s5_d[i], nb)
        y_gm = s5_scan(_to_group_major(u, nb, seq, ctx_len), mats, n_ctx_chunks=ctx_len // S5_CHUNK)
        ys5 = _from_group_major(y_gm, nb, seq, ctx_len)

        hk = GLA_HEADS * GLA_DK
        wa = jnp.zeros((LANES, 2 * hk), F32)
        wa = wa.at[:GLA_GATE_RANK, :hk].set(gla_w_a2[i, 0].astype(F32))
        wa = wa.at[GLA_GATE_RANK:2 * GLA_GATE_RANK, hk:].set(gla_w_a2[i, 1].astype(F32))
        ba = gla_b_a[i].astype(F32).reshape(1, 2 * hk)
        o_f, o_r = gla_scan(gq, gk, gv, ga, wa, ba, nb=nb, seq=seq, ctx_len=ctx_len)

        o_na = neighbourhood_attention(nq, nk, nv, na_bias_tables(na_rpb[i]), nb=nb, seq=seq, ctx_len=ctx_len)
        y_cv = conv_branch(cv, conv_dw[i], conv_dw_b[i], conv_ln_g[i], conv_ln_b[i], nb=nb, seq=seq)

        weights = (gate_b[i].astype(F32).reshape(1, N_BRANCHES * d), s5_w_glu[i].astype(BF16),
                   s5_b_glu[i].astype(F32).reshape(1, S5_WIDTH), s5_w_out[i].astype(BF16),
                   gla_norm_g[i].astype(F32).reshape(1, GLA_DV), gla_w_out[i].astype(BF16),
                   na_w_out[i].astype(BF16), conv_w_out[i].astype(BF16), w_mix_out[i].astype(BF16))
        xs = merge_branches(x_lat, x_ctx, ctx_tile0, modtab[i], gt, ys5, o_f, o_r, gr, o_na, y_cv, weights,
                            n_rows=n_rows, nb=nb, seq=seq, tm=tm)

        n_router = MOE_GROUPS + N_EXPERTS
        w_router = jnp.pad(jnp.concatenate([moe_w_group[i], moe_w_expert[i]], axis=1).astype(F32),
                           ((0, 0), (0, LANES - n_router)))
        b_router = jnp.pad(jnp.concatenate([moe_b_group[i], moe_b_expert[i]]).astype(F32),
                           (0, LANES - n_router)).reshape(1, LANES)
        xs = moe_layer(xs, modtab[i], norm2_g[i].astype(F32), w_router, b_router,
                       moe_w1[i].astype(BF16), moe_w3[i].astype(BF16), moe_w2[i].astype(BF16), final_norm_g,
                       n_rows=n_rows, nb=nb, seq=seq, tm=tm_moe, final=last)
        x_lat, x_ctx, ctx_tile0 = xs, xs, n_lat // tm

    return xs.reshape(nb, seq, d).astype(x.dtype)
```

```python
import functools
import math

import jax
import jax.numpy as jnp
from jax import lax
from jax.experimental import pallas as pl
from jax.experimental.pallas import tpu as pltpu

F32 = jnp.float32
BF16 = jnp.bfloat16
HIGHEST = lax.Precision.HIGHEST

GRID_W = 64
NORM_EPS = 1e-6
N_BRANCHES = 4
S5_WIDTH = 256
S5_GROUP_SIZE = 16
S5_GROUPS = 16
S5_STATE = 64
GLA_HEADS = 4
GLA_DK = 64
GLA_DV = 128
GLA_GATE_RANK = 16
GLA_TAU = 16.0
GLA_CHUNK = 64
ROPE_BASE = 10000.0
NA_HEADS = 4
NA_HEAD_DIM = 64
NA_WIN_ROWS = 8
NA_WIN_COLS = 16
CONV_WIDTH = 256
CONV_KERNEL = 31
MOE_GROUPS = 4
MOE_EXPERTS_PER_GROUP = 8
MOE_HIDDEN = 256
N_EXPERTS = MOE_GROUPS * MOE_EXPERTS_PER_GROUP

LANES = 128
MOD_ROWS = 8
VMEM_LIMIT = 56 * 1024 * 1024
S5_CHUNK = 32
SEQ_BLOCK = 256
CONV_HALO = 16
NEG_BIG = -1e30


def _cparams(sem):
    return pltpu.CompilerParams(dimension_semantics=sem, vmem_limit_bytes=VMEM_LIMIT)


def _resident(shape):
    nd = len(shape)
    return pl.BlockSpec(shape, lambda *_: (0,) * nd, pipeline_mode=pl.Buffered(1))


def _sigmoid(x):
    return 1.0 / (1.0 + jnp.exp(-x))


def _silu(x):
    return x * _sigmoid(x)


def _mod_kernel(c_ref, w_ref, b_ref, o_ref):
    c = c_ref[...]
    o_ref[0] = jnp.dot(_silu(c), w_ref[0], preferred_element_type=F32, precision=HIGHEST) + b_ref[0]


def modulation_table(c_rows, w_mod, b_mod):
    depth, d, n6 = w_mod.shape
    tn = 1024
    return pl.pallas_call(
        _mod_kernel,
        grid=(depth, n6 // tn),
        in_specs=[
            pl.BlockSpec((MOD_ROWS, d), lambda l, j: (0, 0)),
            pl.BlockSpec((1, d, tn), lambda l, j: (l, 0, j)),
            pl.BlockSpec((1, 1, tn), lambda l, j: (l, 0, j)),
        ],
        out_specs=pl.BlockSpec((1, MOD_ROWS, tn), lambda l, j: (l, 0, j)),
        out_shape=jax.ShapeDtypeStruct((depth, MOD_ROWS, n6), F32),
        compiler_params=_cparams(("arbitrary", "arbitrary")),
        name="mod_table",
    )(c_rows, w_mod, b_mod.reshape(depth, 1, n6))


def _mod_row_spec(n_lat_tiles, tiles_per_batch, nb, width):
    def imap(i, *_):
        return (jnp.where(i < n_lat_tiles, i // tiles_per_batch, nb), 0, 0)
    return pl.BlockSpec((None, 1, width), imap)


def _modulated_norm(x, g, shift, scale):
    y = x * lax.rsqrt(jnp.mean(x * x, axis=-1, keepdims=True) + NORM_EPS)
    return (y * g) * (1.0 + scale) + shift


IN_CHUNK = 512


def _x_pair_specs(tm, d, n_lat_tiles, ctx_tile0):
    return [pl.BlockSpec((tm, d), lambda i: (jnp.minimum(i, n_lat_tiles - 1), 0)),
            pl.BlockSpec((tm, d), lambda i: (ctx_tile0 + jnp.maximum(i - n_lat_tiles, 0), 0))]


def _x_pair_tile(xa_ref, xb_ref, n_lat_tiles):
    return jnp.where(pl.program_id(0) < n_lat_tiles, xa_ref[...], xb_ref[...])


def _inproj_kernel(xa_ref, xb_ref, mod_ref, g_ref, cos_ref, sin_ref, wqk_ref, wmisc_ref, wgate_ref,
                   gq_ref, gk_ref, u_ref, nq_ref, nk_ref, nv_ref, gv_ref, gr_ref, cv_ref, ga_ref, gt_ref,
                   *, n_lat_tiles):
    d = xa_ref.shape[1]
    mod = mod_ref[...]
    x = _x_pair_tile(xa_ref, xb_ref, n_lat_tiles)
    h = _modulated_norm(x, g_ref[...], mod[:, 0:d], mod[:, d:2 * d]).astype(BF16)
    cos = cos_ref[...]
    sin = sin_ref[...]
    for j, o_ref in enumerate((gq_ref, gk_ref)):
        y = jnp.dot(h, wqk_ref[:, j * 512:(j + 1) * 512], preferred_element_type=F32)
        o_ref[...] = (y[:, :256] * cos + y[:, 256:] * sin).astype(o_ref.dtype)
    col = 0
    for o_ref in (u_ref, nq_ref, nk_ref, nv_ref, gv_ref, gr_ref, cv_ref, ga_ref):
        w = o_ref.shape[1]
        o_ref[...] = jnp.dot(h, wmisc_ref[:, col:col + w], preferred_element_type=F32).astype(o_ref.dtype)
        col += w
    for j in range(gt_ref.shape[1] // IN_CHUNK):
        sl = slice(j * IN_CHUNK, (j + 1) * IN_CHUNK)
        gt_ref[:, sl] = jnp.dot(h, wgate_ref[:, sl], preferred_element_type=F32).astype(gt_ref.dtype)


def in_projection(x_lat, x_ctx, ctx_tile0, modtab, norm_g, cos_tab, sin_tab, wqk, wmisc, wgate, *, n, nb, seq, tm):
    d = x_lat.shape[1]
    n_lat_tiles = nb * seq // tm
    tpb = seq // tm
    widths = (256, 256, 256, 256, 256, 256, 512, 512, 512, LANES, N_BRANCHES * d)
    dtypes = (BF16, BF16, F32) + (BF16,) * 6 + (F32, BF16)

    def row(i):
        return (i, 0)

    def rope_row(i):
        return (jnp.where(i < n_lat_tiles, i % tpb, tpb), 0)

    return pl.pallas_call(
        functools.partial(_inproj_kernel, n_lat_tiles=n_lat_tiles),
        grid=(n // tm,),
        in_specs=_x_pair_specs(tm, d, n_lat_tiles, ctx_tile0) + [
            _mod_row_spec(n_lat_tiles, tpb, nb, modtab.shape[-1]),
            _resident((1, d)),
            pl.BlockSpec((tm, 256), rope_row),
            pl.BlockSpec((tm, 256), rope_row),
            _resident(wqk.shape),
            _resident(wmisc.shape),
            _resident(wgate.shape),
        ],
        out_specs=[pl.BlockSpec((tm, w), row) for w in widths],
        out_shape=[jax.ShapeDtypeStruct((n, w), dt) for w, dt in zip(widths, dtypes)],
        compiler_params=_cparams(("arbitrary",)),
        name="in_proj",
    )(x_lat, x_ctx, modtab, norm_g.reshape(1, d), cos_tab, sin_tab, wqk, wmisc, wgate)


S5_TAUS_PER_TILE = LANES // S5_GROUP_SIZE
S5_PERM = S5_TAUS_PER_TILE * S5_WIDTH


def _s5_state_kernel(uc_ref, ul_ref, perm_ref, w_ref, x_ref, s_ref):
    halves = S5_WIDTH // LANES
    stride = halves * S5_CHUNK
    n_ctx, n_lat = uc_ref.shape[0] // stride, ul_ref.shape[0] // stride
    for v in range(S5_CHUNK // S5_TAUS_PER_TILE):
        pieces = []
        for w in range(S5_TAUS_PER_TILE):
            tau = v * S5_TAUS_PER_TILE + w
            for hf in range(halves):
                pieces.append(jnp.concatenate([uc_ref[pl.ds(halves * tau + hf, n_ctx, stride=stride), :],
                                               ul_ref[pl.ds(halves * tau + hf, n_lat, stride=stride), :]], axis=0))
        z = jnp.concatenate(pieces, axis=1).astype(BF16)
        xv = jnp.dot(z, perm_ref[...], preferred_element_type=F32).astype(BF16)
        for g in range(S5_GROUPS):
            col = g * S5_CHUNK * S5_GROUP_SIZE + v * LANES
            x_ref[0, :, col:col + LANES] = xv[:, g * LANES:(g + 1) * LANES]
    cw = S5_CHUNK * S5_GROUP_SIZE
    for g in range(S5_GROUPS):
        s_ref[0, g] = jnp.dot(x_ref[0, :, g * cw:(g + 1) * cw], w_ref[g], preferred_element_type=F32)


def _s5_scan_kernel(s_ref, a_ref, h_ref, *, n_ctx_chunks):
    d = pl.program_id(0)
    nc = s_ref.shape[1]
    a1 = a_ref[0, 0]
    a2 = a_ref[0, 1]

    def step(s, h):
        fwd_row = s
        rev_row = jnp.where(s < n_ctx_chunks, n_ctx_chunks - 1 - s, nc - 1 - (s - n_ctx_chunks))
        r = jnp.where(d == 0, fwd_row, rev_row)
        h_ref[0, r] = h
        return h * a1 + pltpu.roll(h, S5_STATE, axis=1) * a2 + s_ref[0, r]

    lax.fori_loop(0, nc, step, jnp.zeros(h_ref.shape[2:], F32))


def _s5_out_kernel(x_ref, h_ref, perm_ref, t_ref, v_ref, yc_ref, yl_ref, y_scr):
    cw = S5_CHUNK * S5_GROUP_SIZE
    halves = S5_WIDTH // LANES
    stride = halves * S5_CHUNK
    n_ctx = yc_ref.shape[0] // stride
    for g in range(S5_GROUPS):
        y = jnp.dot(x_ref[0, :, g * cw:(g + 1) * cw], t_ref[g], preferred_element_type=F32)
        y += jnp.dot(h_ref[0, g].astype(BF16), v_ref[g], preferred_element_type=F32)
        y_scr[:, g * cw:(g + 1) * cw] = y.astype(BF16)
    for v in range(S5_CHUNK // S5_TAUS_PER_TILE):
        yv = jnp.concatenate([y_scr[:, g * cw + v * LANES:g * cw + (v + 1) * LANES] for g in range(S5_GROUPS)],
                             axis=1)
        zv = lax.dot_general(yv, perm_ref[...], (((1,), (1,)), ((), ())), preferred_element_type=F32)
        for w in range(S5_TAUS_PER_TILE):
            tau = v * S5_TAUS_PER_TILE + w
            for hf in range(halves):
                col = w * S5_WIDTH + hf * LANES
                piece = zv[:, col:col + LANES]
                yc_ref[pl.ds(halves * tau + hf, n_ctx, stride=stride), :] = piece[:n_ctx]
                yl_ref[pl.ds(halves * tau + hf, piece.shape[0] - n_ctx, stride=stride), :] = piece[n_ctx:]


def s5_mixer(u, mats, *, nb, seq, ctx_len):
    t_sum, w_cat, v_cat, a12 = mats
    g = S5_GROUPS
    nc = (seq + ctx_len) // S5_CHUNK
    n_ctx_chunks = ctx_len // S5_CHUNK
    cw = S5_CHUNK * S5_GROUP_SIZE
    sw = 4 * S5_STATE
    ctx0 = nb * seq // ctx_len
    halves = S5_WIDTH // LANES
    u2 = u.reshape(halves * u.shape[0], LANES)
    src =(lax.broadcasted_iota(jnp.int32, (S5_PERM, S5_PERM), 0))
    dst = (lax.broadcasted_iota(jnp.int32, (S5_PERM, S5_PERM), 1))
    src_as_dst = ((src % S5_WIDTH) // S5_GROUP_SIZE) * LANES + (src // S5_WIDTH) * S5_GROUP_SIZE + src % S5_GROUP_SIZE
    perm = (src_as_dst == dst).astype(BF16)
    x_gm, s = pl.pallas_call(
        _s5_state_kernel,
        grid=(nb,),
        in_specs=[pl.BlockSpec((halves * ctx_len, LANES), lambda b: (ctx0 + b, 0)),
                  pl.BlockSpec((halves * seq, LANES), lambda b: (b, 0)),
                  _resident(perm.shape), _resident(w_cat.shape)],
        out_specs=[pl.BlockSpec((1, nc, g * cw), lambda b: (b, 0, 0)),
                   pl.BlockSpec((1, g, nc, sw), lambda b: (b, 0, 0, 0))],
        out_shape=[jax.ShapeDtypeStruct((nb, nc, g * cw), BF16), jax.ShapeDtypeStruct((nb, g, nc, sw), F32)],
        compiler_params=_cparams(("arbitrary",)),
        name="s5_chunk_state",
    )(u2, u2, perm, w_cat)
    s_t = jnp.transpose(s.reshape(nb * g, nc, 2, 2 * S5_STATE), (2, 1, 0, 3))
    pb = 16
    h_t = pl.pallas_call(
        functools.partial(_s5_scan_kernel, n_ctx_chunks=n_ctx_chunks),
        grid=(2, g * nb // pb),
        in_specs=[pl.BlockSpec((1, nc, pb, 2 * S5_STATE), lambda d, j: (d, 0, j, 0)),
                  pl.BlockSpec((1, 2, pb, 2 * S5_STATE), lambda d, j: (d, 0, j, 0))],
        out_specs=pl.BlockSpec((1, nc, pb, 2 * S5_STATE), lambda d, j: (d, 0, j, 0)),
        out_shape=jax.ShapeDtypeStruct(s_t.shape, F32),
        compiler_params=_cparams(("arbitrary", "arbitrary")),
        name="s5_chunk_scan",
    )(s_t, a12)
    h = jnp.transpose(h_t, (2, 1, 0, 3)).reshape(nb, g, nc, sw)
    y_ctx, y_lat = pl.pallas_call(
        _s5_out_kernel,
        grid=(nb,),
        in_specs=[pl.BlockSpec((1, nc, g * cw), lambda b: (b, 0, 0), pipeline_mode=pl.Buffered(1)),
                  pl.BlockSpec((1, g, nc, sw), lambda b: (b, 0, 0, 0), pipeline_mode=pl.Buffered(1)),
                  _resident(perm.shape), _resident(t_sum.shape), _resident(v_cat.shape)],
        out_specs=[pl.BlockSpec((halves * ctx_len, LANES), lambda b: (b, 0)),
                   pl.BlockSpec((halves * seq, LANES), lambda b: (b, 0))],
        out_shape=[jax.ShapeDtypeStruct((halves * nb * ctx_len, LANES), F32),
                   jax.ShapeDtypeStruct((halves * nb * seq, LANES), F32)],
        scratch_shapes=[pltpu.VMEM((nc, g * cw), BF16)],
        compiler_params=_cparams(("arbitrary",)),
        name="s5_readout",
    )(x_gm, h, perm, t_sum, v_cat)
    return jnp.concatenate([y_lat, y_ctx], axis=0).reshape(u.shape)


def s5_matrices(lam_re, lam_im, log_dt, b_re, b_im, c_re, c_im, d_skip, nb):
    ch = S5_CHUNK
    gsz = S5_GROUP_SIZE
    dt = jnp.exp(log_dt.astype(F32))[..., None]
    lr = lam_re.astype(F32)
    li = lam_im.astype(F32)

    def power(n):
        n = n.astype(F32)[:, None, None, None]
        mag = jnp.exp(lr * dt * n)
        return mag * jnp.cos(li * dt * n), mag * jnp.sin(li * dt * n)

    ab_re, ab_im = power(jnp.ones((1,), F32))
    ab_re, ab_im = ab_re[0], ab_im[0]
    den = lr * lr + li * li
    nr = ab_re - 1.0
    ni = ab_im
    coef_re = (nr * lr + ni * li) / den
    coef_im = (ni * lr - nr * li) / den
    br = b_re.astype(F32)
    bi = b_im.astype(F32)
    bb_re = coef_re[..., None] * br - coef_im[..., None] * bi
    bb_im = coef_re[..., None] * bi + coef_im[..., None] * br
    cr = c_re.astype(F32)
    ci = c_im.astype(F32)

    p_re, p_im = power(jnp.arange(ch + 1))
    ca_re = cr[None] * p_re[:, :, :, None, :] - ci[None] * p_im[:, :, :, None, :]
    ca_im = cr[None] * p_im[:, :, :, None, :] + ci[None] * p_re[:, :, :, None, :]
    kmat = (jnp.einsum('ndgip,dgpj->ndgij', ca_re[:ch], bb_re, precision=HIGHEST)
            - jnp.einsum('ndgip,dgpj->ndgij', ca_im[:ch], bb_im, precision=HIGHEST))
    kf = jnp.moveaxis(kmat[:, 0], 0, -1)
    kr = jnp.moveaxis(kmat[:, 1], 0, -1)
    k2 = jnp.concatenate([kf[..., :1] + kr[..., :1], kf[..., 1:], jnp.zeros_like(kf[..., :1]),
                          jnp.flip(kr[..., 1:], axis=-1)], axis=-1)
    toep = jnp.tile(k2, ch)[..., :ch * (2 * ch - 1)].reshape(k2.shape[:-1] + (ch, 2 * ch - 1))[..., :ch]
    t_sum = jnp.transpose(toep, (0, 3, 2, 4, 1))
    skip = jnp.eye(ch * gsz, dtype=F32)[None] * jnp.tile(d_skip.astype(F32).reshape(S5_GROUPS, 1, gsz), (1, ch, 1)).reshape(S5_GROUPS, 1, ch * gsz)
    t_sum = t_sum.reshape(S5_GROUPS, ch * gsz, ch * gsz) + skip

    def w_dir(d, pr, pi):
        wr = pr[..., None] * bb_re[d][None] - pi[..., None] * bb_im[d][None]
        wi = pr[..., None] * bb_im[d][None] + pi[..., None] * bb_re[d][None]
        w = jnp.concatenate([wr, wi], axis=2)
        return jnp.transpose(w, (1, 0, 3, 2)).reshape(S5_GROUPS, ch * gsz, 2 * S5_STATE)
    w_cat = jnp.concatenate([w_dir(0, jnp.flip(p_re[:ch, 0], 0), jnp.flip(p_im[:ch, 0], 0)),
                             w_dir(1, p_re[:ch, 1], p_im[:ch, 1])], axis=-1)

    def v_dir(vr, vi):
        v = jnp.concatenate([vr, vi], axis=-1)
        return jnp.transpose(v, (1, 3, 0, 2)).reshape(S5_GROUPS, 2 * S5_STATE, ch * gsz)
    v_cat = jnp.concatenate([v_dir(ca_re[1:, 0], -ca_im[1:, 0]),
                             v_dir(jnp.flip(ca_re[1:, 1], 0), -jnp.flip(ca_im[1:, 1], 0))], axis=1)

    a1 = jnp.concatenate([p_re[ch], p_re[ch]], axis=-1)
    a2 = jnp.concatenate([-p_im[ch], p_im[ch]], axis=-1)
    a12 = jnp.stack([a1, a2], axis=1)
    a12 = jnp.tile(a12, (1, 1, nb, 1))
    return t_sum.astype(BF16), w_cat.astype(BF16), v_cat.astype(BF16), a12


def _split_bf16(x):
    head = x.astype(BF16)
    return head, (x - head.astype(F32)).astype(BF16)


def _gla_direction(q_ref, k_ref, v_ref, a_ref, wa, ba, o_ref, st_ref, reverse):
    c = GLA_CHUNK
    hk = GLA_HEADS * GLA_DK
    hv = GLA_HEADS * GLA_DV
    rows = q_ref.shape[0]
    nchunks = rows // c
    grank2 = 2 * GLA_GATE_RANK

    a_head, a_rem = _split_bf16(a_ref[...])
    lane = lax.broadcasted_iota(jnp.int32, a_head.shape, 1)
    a_pack = jnp.where(jnp.logical_and(lane >= grank2, lane < 2 * grank2), a_rem, a_head)
    z = jnp.dot(a_pack, wa, preferred_element_type=F32) + ba
    g = (jnp.minimum(z, 0.0) - jnp.log(1.0 + jnp.exp(-jnp.abs(z)))) * (1.0 / GLA_TAU)
    ri = lax.broadcasted_iota(jnp.int32, (rows, rows), 0)
    ci = lax.broadcasted_iota(jnp.int32, (rows, rows), 1)
    ordered = (ri <= ci) if reverse else (ri >= ci)
    tri_bd = jnp.logical_and(ri // c == ci // c, ordered).astype(BF16)
    g_head, g_rem = _split_bf16(g)
    bb = jnp.dot(tri_bd, jnp.concatenate([g_head, g_rem], axis=1), preferred_element_type=F32)
    b = bb[:, :hk] + bb[:, hk:]
    q_t = (q_ref[...].astype(F32) * jnp.exp(b)).astype(BF16)
    k = k_ref[...].astype(F32)
    k_t = (k * jnp.exp(-b)).astype(BF16)

    hrow = lax.broadcasted_iota(jnp.int32, (GLA_HEADS, hk), 0)
    kmask = (lax.broadcasted_iota(jnp.int32, (GLA_HEADS, hk), 1) // GLA_DK == hrow).astype(BF16)
    vmask = (lax.broadcasted_iota(jnp.int32, (GLA_HEADS, hv), 1) // GLA_DV
             == lax.broadcasted_iota(jnp.int32, (GLA_HEADS, hv), 0)).astype(BF16)
    qi = lax.broadcasted_iota(jnp.int32, (c, GLA_HEADS * c), 0)
    kj = lax.broadcasted_iota(jnp.int32, (c, GLA_HEADS * c), 1) % c
    causal = ((qi <= kj) if reverse else (qi >= kj)).astype(F32)
    st_mask = (lax.broadcasted_iota(jnp.int32, (hv, hk), 0) // GLA_DV
               == lax.broadcasted_iota(jnp.int32, (hv, hk), 1) // GLA_DK).astype(F32)

    order = range(nchunks - 1, -1, -1) if reverse else range(nchunks)
    st = st_ref[...]
    for j in order:
        sl = slice(j * c, (j + 1) * c)
        b_c = b[sl]
        b_last = b_c[0:1] if reverse else b_c[c - 1:c]
        v_c = v_ref[sl, :]
        k_bd = jnp.concatenate([k_t[sl] * kmask[h:h + 1] for h in range(GLA_HEADS)], axis=0)
        v_bd = jnp.concatenate([v_c * vmask[h:h + 1] for h in range(GLA_HEADS)], axis=0)
        att = lax.dot_general(q_t[sl], k_bd, (((1,), (1,)), ((), ())), preferred_element_type=F32) * causal
        o = jnp.dot(att.astype(BF16), v_bd, preferred_element_type=F32)
        o += lax.dot_general(q_t[sl], st.astype(BF16), (((1,), (1,)), ((), ())), preferred_element_type=F32)
        o_ref[sl, :] = o.astype(o_ref.dtype)
        k_end = (k[sl] * jnp.exp(b_last - b_c)).astype(BF16)
        kv_t = lax.dot_general(v_c, k_end, (((0,), (0,)), ((), ())), preferred_element_type=F32)
        st = jnp.exp(b_last) * st + kv_t * st_mask
    st_ref[...] = st


def _gla_kernel(qf_ref, kf_ref, vf_ref, af_ref, qr_ref, kr_ref, vr_ref, ar_ref, wa_ref, ba_ref,
                of_ref, or_ref, sf_ref, sr_ref):
    @pl.when(pl.program_id(1) == 0)
    def _():
        sf_ref[...] = jnp.zeros_like(sf_ref)
        sr_ref[...] = jnp.zeros_like(sr_ref)

    hk = GLA_HEADS * GLA_DK
    _gla_direction(qf_ref, kf_ref, vf_ref, af_ref, wa_ref[:, :hk], ba_ref[:, :hk], of_ref, sf_ref, False)
    _gla_direction(qr_ref, kr_ref, vr_ref, ar_ref, wa_ref[:, hk:], ba_ref[:, hk:], or_ref, sr_ref, True)


def gla_scan(gq, gk, gv, ga, wa, ba, *, nb, seq, ctx_len):
    n = gq.shape[0]
    blk = SEQ_BLOCK
    assert ctx_len == blk
    lpb = seq // blk
    ctx0 = nb * lpb

    def fwd(b, s):
        return (jnp.where(s == 0, ctx0 + b, b * lpb + s - 1), 0)

    def rev(b, s):
        return (jnp.where(s == 0, ctx0 + b, b * lpb + lpb - s), 0)

    hk = GLA_HEADS * GLA_DK
    hv = GLA_HEADS * GLA_DV
    specs = []
    for imap in (fwd, rev):
        specs += [pl.BlockSpec((blk, hk), imap), pl.BlockSpec((blk, hk), imap),
                  pl.BlockSpec((blk, hv), imap), pl.BlockSpec((blk, LANES), imap)]
    specs += [_resident(wa.shape), _resident(ba.shape)]
    return pl.pallas_call(
        _gla_kernel,
        grid=(nb, lpb + 1),
        in_specs=specs,
        out_specs=[pl.BlockSpec((blk, hv), fwd), pl.BlockSpec((blk, hv), rev)],
        out_shape=[jax.ShapeDtypeStruct((n, hv), BF16)] * 2,
        scratch_shapes=[pltpu.VMEM((hv, hk), F32), pltpu.VMEM((hv, hk), F32)],
        compiler_params=_cparams(("arbitrary", "arbitrary")),
        name="gla_scan",
    )(gq, gk, gv, ga, gq, gk, gv, ga, wa, ba)


NA_ROWS_PER_STEP = 4
NA_UNION_ROWS = NA_WIN_ROWS + NA_ROWS_PER_STEP - 1


def _na_kernel(q_ref, k_ref, v_ref, kc_ref, vc_ref, bias_ref, o_ref, *, n_rows):
    step = pl.program_id(1)
    hd = NA_HEADS * NA_HEAD_DIM
    lane = lax.broadcasted_iota(jnp.int32, (NA_HEADS, hd), 1)
    hrow = lax.broadcasted_iota(jnp.int32, (NA_HEADS, hd), 0)
    head_mask = (lane // NA_HEAD_DIM == hrow).astype(F32)
    kc = kc_ref[...]
    vc = vc_ref[...]
    last = pl.num_programs(1) - 1
    regime = jnp.where(step == 0, 0, jnp.where(step == last, 2, 1))
    ws = jnp.clip(step * NA_ROWS_PER_STEP - NA_WIN_ROWS // 2, 0, n_rows - NA_UNION_ROWS)
    start = pl.multiple_of(ws * GRID_W, GRID_W)
    kw = k_ref[pl.ds(start, NA_UNION_ROWS * GRID_W), :]
    vw = v_ref[pl.ds(start, NA_UNION_ROWS * GRID_W), :]
    q = q_ref[...].astype(F32)
    acc = jnp.zeros(q.shape, F32)
    for h in range(NA_HEADS):
        m_h = head_mask[h:h + 1]
        qh = (q * m_h).astype(BF16)
        s_lat = lax.dot_general(qh, kw, (((1,), (1,)), ((), ())), preferred_element_type=F32)
        s_lat = s_lat + bias_ref[regime, h]
        s_ctx = lax.dot_general(qh, kc, (((1,), (1,)), ((), ())), preferred_element_type=F32)
        m = jnp.maximum(jnp.max(s_lat, axis=1, keepdims=True), jnp.max(s_ctx, axis=1, keepdims=True))
        p_lat = jnp.exp(s_lat - m)
        p_ctx = jnp.exp(s_ctx - m)
        den = jnp.sum(p_lat, axis=1, keepdims=True) + jnp.sum(p_ctx, axis=1, keepdims=True)
        o = jnp.dot(p_lat.astype(BF16), vw, preferred_element_type=F32)
        o += jnp.dot(p_ctx.astype(BF16), vc, preferred_element_type=F32)
        acc += (o / den) * m_h
    o_ref[...] = acc.astype(o_ref.dtype)


def _na_ctx_kernel(q_ref, k_ref, v_ref, o_ref):
    hd = NA_HEADS * NA_HEAD_DIM
    lane = lax.broadcasted_iota(jnp.int32, (NA_HEADS, hd), 1)
    hrow = lax.broadcasted_iota(jnp.int32, (NA_HEADS, hd), 0)
    head_mask = (lane // NA_HEAD_DIM == hrow).astype(F32)
    q = q_ref[...].astype(F32)
    k = k_ref[...]
    v = v_ref[...]
    acc = jnp.zeros(q.shape, F32)
    for h in range(NA_HEADS):
        m_h = head_mask[h:h + 1]
        s = lax.dot_general((q * m_h).astype(BF16), k, (((1,), (1,)), ((), ())), preferred_element_type=F32)
        p = jnp.exp(s - jnp.max(s, axis=1, keepdims=True))
        o = jnp.dot(p.astype(BF16), v, preferred_element_type=F32) / jnp.sum(p, axis=1, keepdims=True)
        acc += o * m_h
    o_ref[...] = acc.astype(o_ref.dtype)


def neighbourhood_attention(nq, nk, nv, bias, *, nb, seq, ctx_len):
    n, hd = nq.shape
    n_rows = seq // GRID_W
    qb = NA_ROWS_PER_STEP * GRID_W
    steps = seq // qb
    ctx0 = nb * seq // ctx_len
    o_lat = pl.pallas_call(
        functools.partial(_na_kernel, n_rows=n_rows),
        grid=(nb, steps),
        in_specs=[pl.BlockSpec((qb, hd), lambda b, s: (b * steps + s, 0)),
                  pl.BlockSpec((seq, hd), lambda b, s: (b, 0)),
                  pl.BlockSpec((seq, hd), lambda b, s: (b, 0)),
                  pl.BlockSpec((ctx_len, hd), lambda b, s: (ctx0 + b, 0)),
                  pl.BlockSpec((ctx_len, hd), lambda b, s: (ctx0 + b, 0)),
                  _resident(bias.shape)],
        out_specs=pl.BlockSpec((qb, hd), lambda b, s: (b * steps + s, 0)),
        out_shape=jax.ShapeDtypeStruct((nb * seq, hd), BF16),
        compiler_params=_cparams(("arbitrary", "arbitrary")),
        name="na_latent",
    )(nq, nk, nv, nk, nv, bias)
    o_ctx = pl.pallas_call(
        _na_ctx_kernel,
        grid=(nb,),
        in_specs=[pl.BlockSpec((ctx_len, hd), lambda b: (ctx0 + b, 0))] * 3,
        out_specs=pl.BlockSpec((ctx_len, hd), lambda b: (b, 0)),
        out_shape=jax.ShapeDtypeStruct((nb * ctx_len, hd), BF16),
        compiler_params=_cparams(("arbitrary",)),
        name="na_context",
    )(nq, nk, nv)
    return jnp.concatenate([o_lat, o_ctx], axis=0)


def na_bias_tables(rpb):
    rpb = rpb.astype(F32)
    wr, nq, nu = NA_WIN_ROWS, NA_ROWS_PER_STEP, NA_UNION_ROWS
    assert nq <= wr // 2 + 1 and wr >= nq + wr // 2
    c_idx = jnp.arange(GRID_W)
    col_start = jnp.clip(c_idx - NA_WIN_COLS // 2, 0, GRID_W - NA_WIN_COLS)
    col_in = (c_idx[None, :] >= col_start[:, None]) & (c_idx[None, :] < col_start[:, None] + NA_WIN_COLS)
    per_regime = []
    for lo_of, dr0_of in ((lambda i: 0, lambda i: wr - 1 - i),
                          (lambda i: i, lambda i: wr // 2 - 1),
                          (lambda i: nu - wr, lambda i: nu - wr - i)):
        rows = []
        for i in range(nq):
            lo, dr0 = lo_of(i), dr0_of(i)
            piece = rpb[:, dr0:dr0 + wr]
            rows.append(jnp.pad(piece, ((0, 0), (lo, nu - wr - lo), (0, 0)), constant_values=NEG_BIG))
        per_regime.append(jnp.stack(rows, axis=1))
    b = jnp.stack(per_regime, axis=1)
    wc = NA_WIN_COLS
    pad = jnp.zeros(b.shape[:-1] + (2 * GRID_W - (2 * wc - 1),), F32)
    table = jnp.concatenate([b[..., wc - 1:], pad, b[..., :wc - 1]], axis=-1)
    b = jnp.tile(table, GRID_W)[..., :GRID_W * (2 * GRID_W - 1)]
    b = b.reshape(table.shape[:-1] + (GRID_W, 2 * GRID_W - 1))[..., :GRID_W]
    b = jnp.where(col_in[None, None, None, None], b, NEG_BIG)
    b = jnp.transpose(b, (1, 0, 2, 4, 3, 5))
    return b.reshape(3, NA_HEADS, nq * GRID_W, nu * GRID_W)


def _conv_kernel(prev_ref, main_ref, next_ref, dw_ref, dwb_ref, lng_ref, lnb_ref, o_ref, buf_ref,
                 *, n_lat_tiles, tiles_per_batch):
    i = pl.program_id(0)
    j = i % tiles_per_batch
    is_lat = i < n_lat_tiles
    has_prev = jnp.logical_and(is_lat, j > 0)
    has_next = jnp.logical_and(is_lat, j < tiles_per_batch - 1)
    cw = CONV_WIDTH
    tl = main_ref.shape[0]

    def glu(a):
        a = a.astype(F32)
        return a[:, :cw] * _sigmoid(a[:, cw:])

    buf_ref[0:CONV_HALO, :] = glu(prev_ref[...]) * has_prev.astype(F32)
    buf_ref[CONV_HALO:CONV_HALO + tl, :] = glu(main_ref[...])
    buf_ref[CONV_HALO + tl:, :] = glu(next_ref[...]) * has_next.astype(F32)
    dw = dw_ref[...]
    acc = jnp.zeros((tl, cw), F32) + dwb_ref[...]
    base = CONV_HALO - CONV_KERNEL // 2
    for k in range(CONV_KERNEL):
        acc += buf_ref[base + k:base + k + tl, :] * dw[k:k + 1, :]
    mu = jnp.mean(acc, axis=-1, keepdims=True)
    xc = acc - mu
    y = xc * lax.rsqrt(jnp.mean(xc * xc, axis=-1, keepdims=True) + NORM_EPS)
    y = y * lng_ref[...] + lnb_ref[...]
    o_ref[...] = _silu(y).astype(o_ref.dtype)


def conv_branch(cv, dw, dw_b, ln_g, ln_b, *, nb, seq):
    n = cv.shape[0]
    tl = SEQ_BLOCK
    hb = tl // CONV_HALO
    n_tiles = n // tl
    cw = CONV_WIDTH
    return pl.pallas_call(
        functools.partial(_conv_kernel, n_lat_tiles=nb * seq // tl, tiles_per_batch=seq // tl),
        grid=(n_tiles,),
        in_specs=[pl.BlockSpec((CONV_HALO, 2 * cw), lambda i: (jnp.maximum(i * hb - 1, 0), 0)),
                  pl.BlockSpec((tl, 2 * cw), lambda i: (i, 0)),
                  pl.BlockSpec((CONV_HALO, 2 * cw), lambda i: (jnp.minimum((i + 1) * hb, n_tiles * hb - 1), 0)),
                  _resident((CONV_KERNEL, cw)), _resident((1, cw)), _resident((1, cw)), _resident((1, cw))],
        out_specs=pl.BlockSpec((tl, cw), lambda i: (i, 0)),
        out_shape=jax.ShapeDtypeStruct((n, cw), BF16),
        scratch_shapes=[pltpu.VMEM((tl + 2 * CONV_HALO, cw), F32)],
        compiler_params=_cparams(("arbitrary",)),
        name="conv_branch",
    )(cv, cv, cv, dw.astype(F32), dw_b.reshape(1, cw).astype(F32), ln_g.reshape(1, cw).astype(F32),
      ln_b.reshape(1, cw).astype(F32))


def _merge_kernel(xa_ref, xb_ref, mod_ref, gt_ref, ys5_ref, of_ref, or_ref, gr_ref, na_ref, cv_ref,
                  gate_b_ref, wglu_ref, bglu_ref, ws5_ref, gng_ref, wgla_ref, wna_ref, wcv_ref, wmix_ref, o_ref,
                  *, n_lat_tiles):
    d = xa_ref.shape[1]
    z = jax.nn.gelu(ys5_ref[...].astype(F32))
    z = z * _sigmoid(jnp.dot(z.astype(BF16), wglu_ref[...], preferred_element_type=F32) + bglu_ref[...])
    br_s5 = jnp.dot(z.astype(BF16), ws5_ref[...], preferred_element_type=F32)
    o = of_ref[...].astype(F32) + or_ref[...].astype(F32)
    r = _silu(gr_ref[...].astype(F32))
    parts = []
    for h in range(GLA_HEADS):
        oh = o[:, h * GLA_DV:(h + 1) * GLA_DV]
        oh = oh * lax.rsqrt(jnp.mean(oh * oh, axis=-1, keepdims=True) + NORM_EPS) * gng_ref[...]
        parts.append(oh * r[:, h * GLA_DV:(h + 1) * GLA_DV])
    y_gla = jnp.concatenate(parts, axis=1).astype(BF16)
    br_gla = jnp.dot(y_gla, wgla_ref[...], preferred_element_type=F32)
    br_na = jnp.dot(na_ref[...], wna_ref[...], preferred_element_type=F32)
    br_cv = jnp.dot(cv_ref[...], wcv_ref[...], preferred_element_type=F32)
    merged = jnp.zeros((xa_ref.shape[0], d), F32)
    for i, br in enumerate((br_s5, br_gla, br_na, br_cv)):
        gate = _sigmoid(gt_ref[:, i * d:(i + 1) * d].astype(F32) + gate_b_ref[:, i * d:(i + 1) * d])
        merged += gate * br
    mix = jnp.dot(merged.astype(BF16), wmix_ref[...], preferred_element_type=F32)
    o_ref[...] = _x_pair_tile(xa_ref, xb_ref, n_lat_tiles) + mod_ref[:, 2 * d:3 * d] * mix


def merge_branches(x_lat, x_ctx, ctx_tile0, modtab, gt, ys5, o_f, o_r, gr, o_na, y_cv, weights,
                   *, n_rows, nb, seq, tm):
    d = x_lat.shape[1]
    n_lat_tiles = nb * seq // tm
    tpb = seq // tm

    def row(i):
        return (i, 0)

    acts = (gt, ys5, o_f, o_r, gr, o_na, y_cv)
    return pl.pallas_call(
        functools.partial(_merge_kernel, n_lat_tiles=n_lat_tiles),
        grid=(n_rows // tm,),
        in_specs=_x_pair_specs(tm, d, n_lat_tiles, ctx_tile0)
        + [_mod_row_spec(n_lat_tiles, tpb, nb, modtab.shape[-1])]
        + [pl.BlockSpec((tm, a.shape[1]), row) for a in acts]
        + [_resident(w.shape) for w in weights],
        out_specs=pl.BlockSpec((tm, d), row),
        out_shape=jax.ShapeDtypeStruct((n_rows, d), F32),
        compiler_params=_cparams(("arbitrary",)),
        name="merge_mix",
    )(x_lat, x_ctx, modtab, *acts, *weights)


MOE_BLOCK = 128


def _route_kernel(x_ref, mod_ref, g_ref, wr_ref, br_ref, ltri_ref, h_ref, comb_ref, meta_ref, cnt_ref):
    d = x_ref.shape[1]
    mod = mod_ref[...]
    h = _modulated_norm(x_ref[...], g_ref[...], mod[:, 3 * d:4 * d], mod[:, 4 * d:5 * d])
    h_ref[...] = h.astype(BF16)
    logits = jnp.dot(h, wr_ref[...], preferred_element_type=F32, precision=HIGHEST) + br_ref[...]
    lane = lax.broadcasted_iota(jnp.int32, logits.shape, 1)
    big = jnp.int32(1 << 20)
    is_g = lane < MOE_GROUPS
    gl = jnp.where(is_g, logits, -jnp.inf)
    gmax = jnp.max(gl, axis=1, keepdims=True)
    gidx = jnp.min(jnp.where(gl == gmax, lane, big), axis=1, keepdims=True)
    group_p = 1.0 / jnp.sum(jnp.where(is_g, jnp.exp(logits - gmax), 0.0), axis=1, keepdims=True)
    first = MOE_GROUPS + gidx * MOE_EXPERTS_PER_GROUP
    in_group = jnp.logical_and(lane >= first, lane < first + MOE_EXPERTS_PER_GROUP)
    el = jnp.where(in_group, logits, -jnp.inf)
    v1 = jnp.max(el, axis=1, keepdims=True)
    i1 = jnp.min(jnp.where(el == v1, lane, big), axis=1, keepdims=True)
    el2 = jnp.where(lane == i1, -jnp.inf, el)
    v2 = jnp.max(el2, axis=1, keepdims=True)
    i2 = jnp.min(jnp.where(el2 == v2, lane, big), axis=1, keepdims=True)
    t = jnp.exp(v2 - v1)
    w1 = group_p / (1.0 + t)
    w2 = group_p * t / (1.0 + t)
    k1 = i1 - first
    k2 = i2 - first
    epg = MOE_EXPERTS_PER_GROUP
    comb = (jnp.where(jnp.logical_or(lane == k1, lane == k1 + epg), w1, 0.0)
            + jnp.where(jnp.logical_or(lane == k2, lane == k2 + epg), w2, 0.0))
    head = comb.astype(BF16).astype(F32)
    comb_ref[...] = jnp.where(lane < epg, head, comb - head).astype(BF16)
    onehot = (lane == gidx).astype(BF16)
    rank_all = jnp.dot(ltri_ref[...], onehot, preferred_element_type=F32)
    rank = jnp.sum(jnp.where(lane == gidx, rank_all, 0.0), axis=1, keepdims=True)
    packed = jnp.where(lane == 0, gidx.astype(F32), jnp.where(lane == 1, rank, 0.0))
    meta_ref[0] = jnp.transpose(packed)[0:8, :]
    counts = jnp.sum(onehot.astype(F32), axis=0, keepdims=True)
    cnt_ref[0] = jnp.broadcast_to(counts, cnt_ref.shape[1:]).astype(jnp.int32)


def _moe_kernel(cnt_ref, x_ref, mod_ref, h_ref, comb_ref, meta_ref, w1_ref, w3_ref, w2_ref, fg_ref, o_ref,
                *, final):
    t = pl.program_id(0)
    g = pl.program_id(1)
    tm, d = x_ref.shape

    @pl.when(g == 0)
    def _():
        o_ref[...] = jnp.zeros_like(o_ref)

    n_tok = cnt_ref[t * MOE_GROUPS + g]
    gid_row = meta_ref[0, 0:1, :]
    rank_row = meta_ref[0, 1:2, :]
    in_grp = gid_row == g.astype(F32)
    row = lax.broadcasted_iota(jnp.int32, (MOE_BLOCK, tm), 0).astype(F32)

    def block(blk, carry):
        sel = jnp.logical_and(in_grp, rank_row == row + (blk * MOE_BLOCK).astype(F32))
        p = sel.astype(BF16)
        xg = jnp.dot(p, h_ref[...], preferred_element_type=F32).astype(BF16)
        cw = jnp.dot(p, comb_ref[...], preferred_element_type=F32)
        yg = jnp.zeros((MOE_BLOCK, d), F32)
        for e in range(MOE_EXPERTS_PER_GROUP):
            a = jnp.dot(xg, w1_ref[0, e], preferred_element_type=F32)
            b = jnp.dot(xg, w3_ref[0, e], preferred_element_type=F32)
            cw_e = cw[:, e:e + 1] + cw[:, MOE_EXPERTS_PER_GROUP + e:MOE_EXPERTS_PER_GROUP + e + 1]
            act = (_silu(a) * b * cw_e).astype(BF16)
            yg += jnp.dot(act, w2_ref[0, e], preferred_element_type=F32)
        o_ref[...] += lax.dot_general(p, yg.astype(BF16), (((0,), (0,)), ((), ())), preferred_element_type=F32)
        return carry

    lax.fori_loop(0, (n_tok + MOE_BLOCK - 1) // MOE_BLOCK, block, 0)

    @pl.when(g == pl.num_programs(1) - 1)
    def _():
        y = x_ref[...] + mod_ref[:, 5 * d:6 * d] * o_ref[...]
        if final:
            y = y * lax.rsqrt(jnp.mean(y * y, axis=-1, keepdims=True) + NORM_EPS) * fg_ref[...]
        o_ref[...] = y


def moe_layer(xs, modtab, norm_g, w_router, b_router, w1, w3, w2, final_g, *, n_rows, nb, seq, tm, final):
    d = xs.shape[1]
    n_lat_tiles = nb * seq // tm
    tpb = seq // tm
    n_tiles = n_rows // tm
    ltri = (lax.broadcasted_iota(jnp.int32, (tm, tm), 0) > lax.broadcasted_iota(jnp.int32, (tm, tm), 1)).astype(BF16)
    mod_spec = _mod_row_spec(n_lat_tiles, tpb, nb, modtab.shape[-1])
    h, comb, meta, cnt = pl.pallas_call(
        _route_kernel,
        grid=(n_tiles,),
        in_specs=[pl.BlockSpec((tm, d), lambda i: (i, 0)), mod_spec,
                  _resident((1, d)), _resident(w_router.shape), _resident(b_router.shape), _resident(ltri.shape)],
        out_specs=[pl.BlockSpec((tm, d), lambda i: (i, 0)), pl.BlockSpec((tm, LANES), lambda i: (i, 0)),
                   pl.BlockSpec((1, 8, tm), lambda i: (i, 0, 0)), pl.BlockSpec((1, 8, LANES), lambda i: (i, 0, 0))],
        out_shape=[jax.ShapeDtypeStruct((n_rows, d), BF16), jax.ShapeDtypeStruct((n_rows, LANES), BF16),
                   jax.ShapeDtypeStruct((n_tiles, 8, tm), F32), jax.ShapeDtypeStruct((n_tiles, 8, LANES), jnp.int32)],
        compiler_params=_cparams(("arbitrary",)),
        name="moe_route",
    )(xs, modtab, norm_g.reshape(1, d), w_router, b_router, ltri)
    counts = cnt[:, 0, :MOE_GROUPS].reshape(n_tiles * MOE_GROUPS)
    epg, hid = w1.shape[1], w1.shape[-1]
    grid_spec = pltpu.PrefetchScalarGridSpec(
        num_scalar_prefetch=1,
        grid=(n_tiles, MOE_GROUPS),
        in_specs=[pl.BlockSpec((tm, d), lambda i, g, c: (i, 0)),
                  pl.BlockSpec((None, 1, modtab.shape[-1]),
                               lambda i, g, c: (jnp.where(i < n_lat_tiles, i // tpb, nb), 0, 0)),
                  pl.BlockSpec((tm, d), lambda i, g, c: (i, 0)),
                  pl.BlockSpec((tm, LANES), lambda i, g, c: (i, 0)),
                  pl.BlockSpec((1, 8, tm), lambda i, g, c: (i, 0, 0)),
                  pl.BlockSpec((1, epg, d, hid), lambda i, g, c: (g, 0, 0, 0)),
                  pl.BlockSpec((1, epg, d, hid), lambda i, g, c: (g, 0, 0, 0)),
                  pl.BlockSpec((1, epg, hid, d), lambda i, g, c: (g, 0, 0, 0)),
                  pl.BlockSpec((1, d), lambda i, g, c: (0, 0))],
        out_specs=pl.BlockSpec((tm, d), lambda i, g, c: (i, 0)),
    )
    return pl.pallas_call(
        functools.partial(_moe_kernel, final=final),
        grid_spec=grid_spec,
        out_shape=jax.ShapeDtypeStruct((n_rows, d), F32),
        compiler_params=_cparams(("arbitrary", "arbitrary")),
        name="moe_experts",
    )(counts, xs, modtab, h, comb, meta, w1, w3, w2, final_g.reshape(1, d).astype(F32))


def rope_tables(seq, pad_rows):
    t = jnp.arange(seq, dtype=jnp.int32)
    row = (t // GRID_W).astype(F32)
    colp = (t % GRID_W).astype(F32)
    half = GLA_DK // 2
    inv_freq = ROPE_BASE ** (-jnp.arange(0, half, 2, dtype=F32) / half)
    ang_r = row[:, None] * inv_freq
    ang_c = colp[:, None] * inv_freq
    dd = jnp.arange(GLA_DK)
    ang = jnp.where((dd < half)[None, :], ang_r[:, dd % (half // 2)], ang_c[:, dd % (half // 2)])
    sign = jnp.where((dd % half) < half // 2, -1.0, 1.0).astype(F32)
    cos = jnp.tile(jnp.cos(ang), (1, GLA_HEADS))
    sin = jnp.tile(jnp.sin(ang) * sign[None, :], (1, GLA_HEADS))
    cos = jnp.concatenate([cos, jnp.ones((pad_rows, cos.shape[1]), F32)], axis=0)
    sin = jnp.concatenate([sin, jnp.zeros((pad_rows, sin.shape[1]), F32)], axis=0)
    return cos, sin


def split_in_weights(w_in, d):
    widths = (S5_WIDTH, GLA_HEADS * GLA_DK, GLA_HEADS * GLA_DK, GLA_HEADS * GLA_DV, GLA_HEADS * GLA_DV,
              2 * GLA_GATE_RANK, NA_HEADS * NA_HEAD_DIM, NA_HEADS * NA_HEAD_DIM, NA_HEADS * NA_HEAD_DIM,
              2 * CONV_WIDTH, N_BRANCHES * d)
    names = ('u', 'gq', 'gk', 'gv', 'gr', 'ga', 'nq', 'nk', 'nv', 'cv', 'gt')
    parts = {}
    col = 0
    for nme, w in zip(names, widths):
        parts[nme] = w_in[:, col:col + w]
        col += w
    swap = jnp.arange(GLA_HEADS * GLA_DK) ^ (GLA_DK // 4)
    gq = parts['gq'] * (GLA_DK ** -0.5)
    wqk = jnp.concatenate([gq, gq[:, swap], parts['gk'], parts['gk'][:, swap]], axis=1)
    ga = jnp.concatenate([parts['ga']] * 3 + [jnp.zeros((d, LANES - 6 * GLA_GATE_RANK), w_in.dtype)], axis=1)
    wmisc = jnp.concatenate([parts['u'], parts['nq'] * (NA_HEAD_DIM ** -0.5), parts['nk'], parts['nv'],
                             parts['gv'], parts['gr'], parts['cv'], ga], axis=1)
    return wqk.astype(BF16), wmisc.astype(BF16), parts['gt'].astype(BF16)


def kernel(x, c, ctx, c_ctx, norm1_g, norm2_g, w_mod, b_mod, w_in, gate_b, w_mix_out, s5_lam_re, s5_lam_im, s5_log_dt, s5_b_re, s5_b_im, s5_c_re, s5_c_im, s5_d, s5_w_glu, s5_b_glu, s5_w_out, gla_w_a2, gla_b_a, gla_norm_g, gla_w_out, na_rpb, na_w_out, conv_dw, conv_dw_b, conv_ln_g, conv_ln_b, conv_w_out, moe_w_group, moe_b_group, moe_w_expert, moe_b_expert, moe_w1, moe_w3, moe_w2, final_norm_g):
    nb, seq, d = x.shape
    ctx_len = ctx.shape[1]
    depth = w_mod.shape[0]
    n_lat = nb * seq
    tm = 512
    tm_moe = math.gcd(1024, nb * ctx_len)
    assert ctx_len == SEQ_BLOCK and seq % tm_moe == 0 and (nb * ctx_len) % tm_moe == 0 and nb < MOD_ROWS

    n_all = n_lat + nb * ctx_len
    x_lat, x_ctx, ctx_tile0 = x.reshape(n_lat, d).astype(F32), ctx.reshape(nb * ctx_len, d).astype(F32), 0
    c_rows = jnp.zeros((MOD_ROWS, d), F32).at[:nb].set(c.astype(F32)).at[nb].set(c_ctx.astype(F32))
    modtab = modulation_table(c_rows, w_mod.astype(F32), b_mod.astype(F32))
    modtab = modtab.reshape(depth, MOD_ROWS, 1, 6 * d)
    cos_tab, sin_tab = rope_tables(seq, tm)

    for i in range(depth):
        last = i == depth - 1
        n_rows = n_lat if last else n_all
        wqk, wmisc, wgate = split_in_weights(w_in[i], d)
        gq, gk, u, nq, nk, nv, gv, gr, cv, ga, gt = in_projection(
            x_lat, x_ctx, ctx_tile0, modtab[i], norm1_g[i].astype(F32), cos_tab, sin_tab, wqk, wmisc, wgate,
            n=n_all, nb=nb, seq=seq, tm=tm)

        mats = s5_matrices(s5_lam_re[i], s5_lam_im[i], s5_log_dt[i], s5_b_re[i], s5_b_im[i],
                           s5_c_re[i], s5_c_im[i], s5_d[i], nb)
        ys5 = s5_mixer(u, mats, nb=nb, seq=seq, ctx_len=ctx_len)

        hk = GLA_HEADS * GLA_DK
        zero = jnp.zeros((GLA_GATE_RANK, hk), F32)
        wd = jnp.concatenate([jnp.concatenate([gla_w_a2[i, 0].astype(F32), zero], axis=1),
                              jnp.concatenate([zero, gla_w_a2[i, 1].astype(F32)], axis=1)], axis=0)
        wd_head, wd_rem = _split_bf16(wd)
        wa = jnp.concatenate([wd_head, wd_head, wd_rem,
                              jnp.zeros((LANES - 6 * GLA_GATE_RANK, 2 * hk), BF16)], axis=0)
        ba = gla_b_a[i].astype(F32).reshape(1, 2 * hk)
        o_f, o_r = gla_scan(gq, gk, gv, ga, wa, ba, nb=nb, seq=seq, ctx_len=ctx_len)

        o_na = neighbourhood_attention(nq, nk, nv, na_bias_tables(na_rpb[i]), nb=nb, seq=seq, ctx_len=ctx_len)
        y_cv = conv_branch(cv, conv_dw[i], conv_dw_b[i], conv_ln_g[i], conv_ln_b[i], nb=nb, seq=seq)

        weights = (gate_b[i].astype(F32).reshape(1, N_BRANCHES * d), s5_w_glu[i].astype(BF16),
                   s5_b_glu[i].astype(F32).reshape(1, S5_WIDTH), s5_w_out[i].astype(BF16),
                   gla_norm_g[i].astype(F32).reshape(1, GLA_DV), gla_w_out[i].astype(BF16),
                   na_w_out[i].astype(BF16), conv_w_out[i].astype(BF16), w_mix_out[i].astype(BF16))
        xs = merge_branches(x_lat, x_ctx, ctx_tile0, modtab[i], gt, ys5, o_f, o_r, gr, o_na, y_cv, weights,
                            n_rows=n_rows, nb=nb, seq=seq, tm=tm)

        n_router = MOE_GROUPS + N_EXPERTS
        w_router = jnp.pad(jnp.concatenate([moe_w_group[i], moe_w_expert[i]], axis=1).astype(F32),
                           ((0, 0), (0, LANES - n_router)))
        b_router = jnp.pad(jnp.concatenate([moe_b_group[i], moe_b_expert[i]]).astype(F32),
                           (0, LANES - n_router)).reshape(1, LANES)
        xs = moe_layer(xs, modtab[i], norm2_g[i].astype(F32), w_router, b_router,
                       moe_w1[i].astype(BF16), moe_w3[i].astype(BF16), moe_w2[i].astype(BF16), final_norm_g,
                       n_rows=n_rows, nb=nb, seq=seq, tm=tm_moe, final=last)
        x_lat, x_ctx, ctx_tile0 = xs, xs, n_lat // tm

    return xs.reshape(nb, seq, d).astype(x.dtype)
```

```python
import functools
import math

import jax
import jax.numpy as jnp
from jax import lax
from jax.experimental import pallas as pl
from jax.experimental.pallas import tpu as pltpu

F32 = jnp.float32
BF16 = jnp.bfloat16
HIGHEST = lax.Precision.HIGHEST

GRID_W = 64
NORM_EPS = 1e-6
N_BRANCHES = 4
S5_WIDTH = 256
S5_GROUP_SIZE = 16
S5_GROUPS = 16
S5_STATE = 64
GLA_HEADS = 4
GLA_DK = 64
GLA_DV = 128
GLA_GATE_RANK = 16
GLA_TAU = 16.0
GLA_CHUNK = 64
ROPE_BASE = 10000.0
NA_HEADS = 4
NA_HEAD_DIM = 64
NA_WIN_ROWS = 8
NA_WIN_COLS = 16
CONV_WIDTH = 256
CONV_KERNEL = 31
MOE_GROUPS = 4
MOE_EXPERTS_PER_GROUP = 8
MOE_HIDDEN = 256
N_EXPERTS = MOE_GROUPS * MOE_EXPERTS_PER_GROUP

LANES = 128
SUBLANES = 8
MOD_ROWS = 8
VMEM_LIMIT = 56 * 1024 * 1024
S5_CHUNK = 32
SEQ_BLOCK = 256
CONV_HALO = 16
NEG_BIG = -1e30


def _cparams(sem):
    return pltpu.CompilerParams(dimension_semantics=sem, vmem_limit_bytes=VMEM_LIMIT)


def _resident(shape):
    nd = len(shape)
    return pl.BlockSpec(shape, lambda *_: (0,) * nd, pipeline_mode=pl.Buffered(1))


def _sigmoid(x):
    return 0.5 * jnp.tanh(0.5 * x) + 0.5


def _silu(x):
    return x * _sigmoid(x)


def _mod_kernel(c_ref, w_ref, b_ref, o_ref):
    c = c_ref[...]
    o_ref[0] = jnp.dot(_silu(c), w_ref[0], preferred_element_type=F32, precision=HIGHEST) + b_ref[0]


def modulation_table(c_rows, w_mod, b_mod):
    depth, d, n6 = w_mod.shape
    tn = 1024
    return pl.pallas_call(
        _mod_kernel,
        grid=(depth, n6 // tn),
        in_specs=[
            pl.BlockSpec((MOD_ROWS, d), lambda l, j: (0, 0)),
            pl.BlockSpec((1, d, tn), lambda l, j: (l, 0, j)),
            pl.BlockSpec((1, 1, tn), lambda l, j: (l, 0, j)),
        ],
        out_specs=pl.BlockSpec((1, MOD_ROWS, tn), lambda l, j: (l, 0, j)),
        out_shape=jax.ShapeDtypeStruct((depth, MOD_ROWS, n6), F32),
        compiler_params=_cparams(("arbitrary", "arbitrary")),
        name="mod_table",
    )(c_rows, w_mod, b_mod.reshape(depth, 1, n6))


def _mod_row_spec(n_lat_tiles, tiles_per_batch, nb, width):
    def imap(i, *_):
        return (jnp.where(i < n_lat_tiles, i // tiles_per_batch, nb), 0, 0)
    return pl.BlockSpec((None, 1, width), imap)


def _modulated_norm(x, g, shift, scale):
    y = x * lax.rsqrt(jnp.mean(x * x, axis=-1, keepdims=True) + NORM_EPS)
    return (y * g) * (1.0 + scale) + shift


IN_CHUNK = 512


def _x_pair_specs(tm, d, n_lat_tiles, ctx_tile0):
    return [pl.BlockSpec((tm, d), lambda i: (jnp.minimum(i, n_lat_tiles - 1), 0)),
            pl.BlockSpec((tm, d), lambda i: (ctx_tile0 + jnp.maximum(i - n_lat_tiles, 0), 0))]


def _x_pair_tile(xa_ref, xb_ref, n_lat_tiles):
    return jnp.where(pl.program_id(0) < n_lat_tiles, xa_ref[...], xb_ref[...])


def _inproj_kernel(xa_ref, xb_ref, mod_ref, g_ref, cos_ref, sin_ref, wqk_ref, wmisc_ref, wgate_ref,
                   gq_ref, gk_ref, ua_ref, ub_ref, nq_ref, nk_ref, nv_ref, gv_ref, gr_ref, cv_ref, ga_ref, gt_ref,
                   *, n_lat_tiles):
    d = xa_ref.shape[1]
    mod = mod_ref[...]
    x = _x_pair_tile(xa_ref, xb_ref, n_lat_tiles)
    h = _modulated_norm(x, g_ref[...], mod[:, 0:d], mod[:, d:2 * d]).astype(BF16)
    cos = cos_ref[...]
    sin = sin_ref[...]
    for j, o_ref in enumerate((gq_ref, gk_ref)):
        y = jnp.dot(h, wqk_ref[:, j * 512:(j + 1) * 512], preferred_element_type=F32)
        o_ref[...] = (y[:, :256] * cos + y[:, 256:] * sin).astype(o_ref.dtype)
    col = 0
    for o_ref in (ua_ref, ub_ref, nq_ref, nk_ref, nv_ref, gv_ref, gr_ref, cv_ref, ga_ref):
        w = o_ref.shape[1]
        o_ref[...] = jnp.dot(h, wmisc_ref[:, col:col + w], preferred_element_type=F32).astype(o_ref.dtype)
        col += w
    for j in range(gt_ref.shape[1] // IN_CHUNK):
        sl = slice(j * IN_CHUNK, (j + 1) * IN_CHUNK)
        gt_ref[:, sl] = jnp.dot(h, wgate_ref[:, sl], preferred_element_type=F32).astype(gt_ref.dtype)


def in_projection(x_lat, x_ctx, ctx_tile0, modtab, norm_g, cos_tab, sin_tab, wqk, wmisc, wgate, *, n, nb, seq, tm):
    d = x_lat.shape[1]
    n_lat_tiles = nb * seq // tm
    tpb = seq // tm
    widths = (256, 256, LANES, LANES, 256, 256, 256, 512, 512, 512, LANES, N_BRANCHES * d)
    dtypes = (BF16, BF16, F32, F32) + (BF16,) * 6 + (F32, BF16)

    def row(i):
        return (i, 0)

    def rope_row(i):
        return (jnp.where(i < n_lat_tiles, i % tpb, tpb), 0)

    return pl.pallas_call(
        functools.partial(_inproj_kernel, n_lat_tiles=n_lat_tiles),
        grid=(n // tm,),
        in_specs=_x_pair_specs(tm, d, n_lat_tiles, ctx_tile0) + [
            _mod_row_spec(n_lat_tiles, tpb, nb, modtab.shape[-1]),
            _resident((1, d)),
            pl.BlockSpec((tm, 256), rope_row),
            pl.BlockSpec((tm, 256), rope_row),
            _resident(wqk.shape),
            _resident(wmisc.shape),
            _resident(wgate.shape),
        ],
        out_specs=[pl.BlockSpec((tm, w), row) for w in widths],
        out_shape=[jax.ShapeDtypeStruct((n, w), dt) for w, dt in zip(widths, dtypes)],
        compiler_params=_cparams(("arbitrary",)),
        name="in_proj",
    )(x_lat, x_ctx, modtab, norm_g.reshape(1, d), cos_tab, sin_tab, wqk, wmisc, wgate)


S5_TAUS_PER_TILE = LANES // S5_GROUP_SIZE
S5_PERM = S5_TAUS_PER_TILE * S5_WIDTH


def _s5_state_kernel(uca_ref, ucb_ref, ula_ref, ulb_ref, perm_ref, w_ref, x_ref, s_ref):
    n_ctx, n_lat = uca_ref.shape[0] // S5_CHUNK, ula_ref.shape[0] // S5_CHUNK
    for v in range(S5_CHUNK // S5_TAUS_PER_TILE):
        pieces = []
        for w in range(S5_TAUS_PER_TILE):
            tau = v * S5_TAUS_PER_TILE + w
            for uc_ref, ul_ref in ((uca_ref, ula_ref), (ucb_ref, ulb_ref)):
                pieces.append(jnp.concatenate([uc_ref[pl.ds(tau, n_ctx, stride=S5_CHUNK), :],
                                               ul_ref[pl.ds(tau, n_lat, stride=S5_CHUNK), :]], axis=0))
        z = jnp.concatenate(pieces, axis=1).astype(BF16)
        xv = jnp.dot(z, perm_ref[...], preferred_element_type=F32).astype(BF16)
        for g in range(S5_GROUPS):
            col = g * S5_CHUNK * S5_GROUP_SIZE + v * LANES
            x_ref[0, :, col:col + LANES] = xv[:, g * LANES:(g + 1) * LANES]
    cw = S5_CHUNK * S5_GROUP_SIZE
    for g in range(S5_GROUPS):
        s_ref[0, g] = jnp.dot(x_ref[0, :, g * cw:(g + 1) * cw], w_ref[g], preferred_element_type=F32)


def _s5_scan_kernel(s_ref, a_ref, h_ref, *, n_ctx_chunks):
    d = pl.program_id(0)
    nc = s_ref.shape[1]
    a1 = a_ref[0, 0]
    a2 = a_ref[0, 1]

    def step(s, h):
        fwd_row = s
        rev_row = jnp.where(s < n_ctx_chunks, n_ctx_chunks - 1 - s, nc - 1 - (s - n_ctx_chunks))
        r = jnp.where(d == 0, fwd_row, rev_row)
        h_ref[0, r] = h
        return h * a1 + pltpu.roll(h, S5_STATE, axis=1) * a2 + s_ref[0, r]

    lax.fori_loop(0, nc, step, jnp.zeros(h_ref.shape[2:], F32))


def _s5_out_kernel(x_ref, h_ref, perm_ref, t_ref, v_ref, yca_ref, ycb_ref, yla_ref, ylb_ref, y_scr):
    cw = S5_CHUNK * S5_GROUP_SIZE
    n_ctx = yca_ref.shape[0] // S5_CHUNK
    for g in range(S5_GROUPS):
        y = jnp.dot(x_ref[0, :, g * cw:(g + 1) * cw], t_ref[g], preferred_element_type=F32)
        y += jnp.dot(h_ref[0, g].astype(BF16), v_ref[g], preferred_element_type=F32)
        y_scr[:, g * cw:(g + 1) * cw] = y.astype(BF16)
    for v in range(S5_CHUNK // S5_TAUS_PER_TILE):
        yv = jnp.concatenate([y_scr[:, g * cw + v * LANES:g * cw + (v + 1) * LANES] for g in range(S5_GROUPS)],
                             axis=1)
        zv = lax.dot_general(yv, perm_ref[...], (((1,), (1,)), ((), ())), preferred_element_type=F32)
        for w in range(S5_TAUS_PER_TILE):
            tau = v * S5_TAUS_PER_TILE + w
            for hf, (yc_ref, yl_ref) in enumerate(((yca_ref, yla_ref), (ycb_ref, ylb_ref))):
                col = w * S5_WIDTH + hf * LANES
                piece = zv[:, col:col + LANES]
                yc_ref[pl.ds(tau, n_ctx, stride=S5_CHUNK), :] = piece[:n_ctx]
                yl_ref[pl.ds(tau, piece.shape[0] - n_ctx, stride=S5_CHUNK), :] = piece[n_ctx:]


def s5_mixer(ua, ub, mats, *, nb, seq, ctx_len):
    t_sum, w_cat, v_cat, a12 = mats
    g = S5_GROUPS
    nc = (seq + ctx_len) // S5_CHUNK
    n_ctx_chunks = ctx_len // S5_CHUNK
    cw = S5_CHUNK * S5_GROUP_SIZE
    sw = 4 * S5_STATE
    ctx0 = nb * seq // ctx_len
    src = (lax.broadcasted_iota(jnp.int32, (S5_PERM, S5_PERM), 0))
    dst = (lax.broadcasted_iota(jnp.int32, (S5_PERM, S5_PERM), 1))
    src_as_dst = ((src % S5_WIDTH) // S5_GROUP_SIZE) * LANES + (src // S5_WIDTH) * S5_GROUP_SIZE + src % S5_GROUP_SIZE
    perm = (src_as_dst == dst).astype(BF16)
    x_gm, s = pl.pallas_call(
        _s5_state_kernel,
        grid=(nb,),
        in_specs=[pl.BlockSpec((ctx_len, LANES), lambda b: (ctx0 + b, 0))] * 2
        + [pl.BlockSpec((seq, LANES), lambda b: (b, 0))] * 2
        + [_resident(perm.shape), _resident(w_cat.shape)],
        out_specs=[pl.BlockSpec((1, nc, g * cw), lambda b: (b, 0, 0)),
                   pl.BlockSpec((1, g, nc, sw), lambda b: (b, 0, 0, 0))],
        out_shape=[jax.ShapeDtypeStruct((nb, nc, g * cw), BF16), jax.ShapeDtypeStruct((nb, g, nc, sw), F32)],
        compiler_params=_cparams(("arbitrary",)),
        name="s5_chunk_state",
    )(ua, ub, ua, ub, perm, w_cat)
    s_t = jnp.transpose(s.reshape(nb * g, nc, 2, 2 * S5_STATE), (2, 1, 0, 3))
    pb = g * nb
    h_t = pl.pallas_call(
        functools.partial(_s5_scan_kernel, n_ctx_chunks=n_ctx_chunks),
        grid=(2, g * nb // pb),
        in_specs=[pl.BlockSpec((1, nc, pb, 2 * S5_STATE), lambda d, j: (d, 0, j, 0)),
                  pl.BlockSpec((1, 2, pb, 2 * S5_STATE), lambda d, j: (d, 0, j, 0))],
        out_specs=pl.BlockSpec((1, nc, pb, 2 * S5_STATE), lambda d, j: (d, 0, j, 0)),
        out_shape=jax.ShapeDtypeStruct(s_t.shape, F32),
        compiler_params=_cparams(("arbitrary", "arbitrary")),
        name="s5_chunk_scan",
    )(s_t, a12)
    h = jnp.transpose(h_t, (2, 1, 0, 3)).reshape(nb, g, nc, sw)
    yca, ycb, yla, ylb = pl.pallas_call(
        _s5_out_kernel,
        grid=(nb,),
        in_specs=[pl.BlockSpec((1, nc, g * cw), lambda b: (b, 0, 0), pipeline_mode=pl.Buffered(1)),
                  pl.BlockSpec((1, g, nc, sw), lambda b: (b, 0, 0, 0), pipeline_mode=pl.Buffered(1)),
                  _resident(perm.shape), _resident(t_sum.shape), _resident(v_cat.shape)],
        out_specs=[pl.BlockSpec((ctx_len, LANES), lambda b: (b, 0))] * 2
        + [pl.BlockSpec((seq, LANES), lambda b: (b, 0))] * 2,
        out_shape=[jax.ShapeDtypeStruct((nb * ctx_len, LANES), F32)] * 2
        + [jax.ShapeDtypeStruct((nb * seq, LANES), F32)] * 2,
        scratch_shapes=[pltpu.VMEM((nc, g * cw), BF16)],
        compiler_params=_cparams(("arbitrary",)),
        name="s5_readout",
    )(x_gm, h, perm, t_sum, v_cat)
    return jnp.concatenate([yla, yca], axis=0), jnp.concatenate([ylb, ycb], axis=0)


def s5_matrices(lam_re, lam_im, log_dt, b_re, b_im, c_re, c_im, d_skip, nb):
    ch = S5_CHUNK
    gsz = S5_GROUP_SIZE
    dt = jnp.exp(log_dt.astype(F32))[..., None]
    lr = lam_re.astype(F32)
    li = lam_im.astype(F32)

    def power(n):
        n = n.astype(F32)[:, None, None, None]
        mag = jnp.exp(lr * dt * n)
        return mag * jnp.cos(li * dt * n), mag * jnp.sin(li * dt * n)

    ab_re, ab_im = power(jnp.ones((1,), F32))
    ab_re, ab_im = ab_re[0], ab_im[0]
    den = lr * lr + li * li
    nr = ab_re - 1.0
    ni = ab_im
    coef_re = (nr * lr + ni * li) / den
    coef_im = (ni * lr - nr * li) / den
    br = b_re.astype(F32)
    bi = b_im.astype(F32)
    bb_re = coef_re[..., None] * br - coef_im[..., None] * bi
    bb_im = coef_re[..., None] * bi + coef_im[..., None] * br
    cr = c_re.astype(F32)
    ci = c_im.astype(F32)

    p_re, p_im = power(jnp.arange(ch + 1))
    ca_re = cr[None] * p_re[:, :, :, None, :] - ci[None] * p_im[:, :, :, None, :]
    ca_im = cr[None] * p_im[:, :, :, None, :] + ci[None] * p_re[:, :, :, None, :]
    kmat = (jnp.einsum('ndgip,dgpj->ndgij', ca_re[:ch], bb_re, precision=HIGHEST)
            - jnp.einsum('ndgip,dgpj->ndgij', ca_im[:ch], bb_im, precision=HIGHEST))
    lags = jnp.concatenate([jnp.flip(kmat[1:, 1], axis=0), kmat[:1, 0] + kmat[:1, 1], kmat[1:, 0]], axis=0)
    lag_rows = jnp.transpose(lags, (1, 3, 0, 2)).reshape(S5_GROUPS, gsz, (2 * ch - 1) * gsz)
    t_sum = jnp.concatenate([lag_rows[:, :, (ch - 1 - s) * gsz:(2 * ch - 1 - s) * gsz] for s in range(ch)], axis=1)
    skip = jnp.eye(ch * gsz, dtype=F32)[None] * jnp.tile(d_skip.astype(F32).reshape(S5_GROUPS, 1, gsz), (1, ch, 1)).reshape(S5_GROUPS, 1, ch * gsz)
    t_sum = t_sum + skip

    def w_dir(d, pr, pi):
        wr = pr[..., None] * bb_re[d][None] - pi[..., None] * bb_im[d][None]
        wi = pr[..., None] * bb_im[d][None] + pi[..., None] * bb_re[d][None]
        w = jnp.concatenate([wr, wi], axis=2)
        return jnp.transpose(w, (1, 0, 3, 2)).reshape(S5_GROUPS, ch * gsz, 2 * S5_STATE)
    w_cat = jnp.concatenate([w_dir(0, jnp.flip(p_re[:ch, 0], 0), jnp.flip(p_im[:ch, 0], 0)),
                             w_dir(1, p_re[:ch, 1], p_im[:ch, 1])], axis=-1)

    def v_dir(vr, vi):
        v = jnp.concatenate([vr, vi], axis=-1)
        return jnp.transpose(v, (1, 3, 0, 2)).reshape(S5_GROUPS, 2 * S5_STATE, ch * gsz)
    v_cat = jnp.concatenate([v_dir(ca_re[1:, 0], -ca_im[1:, 0]),
                             v_dir(jnp.flip(ca_re[1:, 1], 0), -jnp.flip(ca_im[1:, 1], 0))], axis=1)

    a1 = jnp.concatenate([p_re[ch], p_re[ch]], axis=-1)
    a2 = jnp.concatenate([-p_im[ch], p_im[ch]], axis=-1)
    a12 = jnp.stack([a1, a2], axis=1)
    a12 = jnp.tile(a12, (1, 1, nb, 1))
    return t_sum.astype(BF16), w_cat.astype(BF16), v_cat.astype(BF16), a12


def _split_bf16(x):
    head = x.astype(BF16)
    return head, (x - head.astype(F32)).astype(BF16)


def _gla_direction(q_ref, k_ref, v_ref, a_ref, wa, ba, o_ref, st_ref, reverse):
    c = GLA_CHUNK
    hk = GLA_HEADS * GLA_DK
    hv = GLA_HEADS * GLA_DV
    rows = q_ref.shape[0]
    nchunks = rows // c
    grank2 = 2 * GLA_GATE_RANK

    a_head, a_rem = _split_bf16(a_ref[...])
    lane = lax.broadcasted_iota(jnp.int32, a_head.shape, 1)
    a_pack = jnp.where(jnp.logical_and(lane >= grank2, lane < 2 * grank2), a_rem, a_head)
    z = jnp.dot(a_pack, wa, preferred_element_type=F32) + ba
    g = (jnp.minimum(z, 0.0) - jnp.log(1.0 + jnp.exp(-jnp.abs(z)))) * (1.0 / GLA_TAU)
    ri = lax.broadcasted_iota(jnp.int32, (rows, rows), 0)
    ci = lax.broadcasted_iota(jnp.int32, (rows, rows), 1)
    ordered = (ri <= ci) if reverse else (ri >= ci)
    tri_bd = jnp.logical_and(ri // c == ci // c, ordered).astype(BF16)
    g_head, g_rem = _split_bf16(g)
    bb = jnp.dot(tri_bd, jnp.concatenate([g_head, g_rem], axis=1), preferred_element_type=F32)
    b = bb[:, :hk] + bb[:, hk:]
    q_t = (q_ref[...].astype(F32) * jnp.exp(b)).astype(BF16)
    k = k_ref[...].astype(F32)
    k_t = (k * jnp.exp(-b)).astype(BF16)

    hrow = lax.broadcasted_iota(jnp.int32, (GLA_HEADS, hk), 0)
    kmask = (lax.broadcasted_iota(jnp.int32, (GLA_HEADS, hk), 1) // GLA_DK == hrow).astype(BF16)
    vmask = (lax.broadcasted_iota(jnp.int32, (GLA_HEADS, hv), 1) // GLA_DV
             == lax.broadcasted_iota(jnp.int32, (GLA_HEADS, hv), 0)).astype(BF16)
    qi = lax.broadcasted_iota(jnp.int32, (c, GLA_HEADS * c), 0)
    kj = lax.broadcasted_iota(jnp.int32, (c, GLA_HEADS * c), 1) % c
    causal = ((qi <= kj) if reverse else (qi >= kj)).astype(F32)
    st_mask = (lax.broadcasted_iota(jnp.int32, (hv, hk), 0) // GLA_DV
               == lax.broadcasted_iota(jnp.int32, (hv, hk), 1) // GLA_DK).astype(F32)

    order = range(nchunks - 1, -1, -1) if reverse else range(nchunks)
    st = st_ref[...]
    for j in order:
        sl = slice(j * c, (j + 1) * c)
        b_c = b[sl]
        b_last = b_c[0:1] if reverse else b_c[c - 1:c]
        v_c = v_ref[sl, :]
        k_bd = jnp.concatenate([k_t[sl] * kmask[h:h + 1] for h in range(GLA_HEADS)], axis=0)
        v_bd = jnp.concatenate([v_c * vmask[h:h + 1] for h in range(GLA_HEADS)], axis=0)
        att = lax.dot_general(q_t[sl], k_bd, (((1,), (1,)), ((), ())), preferred_element_type=F32) * causal
        o = jnp.dot(att.astype(BF16), v_bd, preferred_element_type=F32)
        o += lax.dot_general(q_t[sl], st.astype(BF16), (((1,), (1,)), ((), ())), preferred_element_type=F32)
        o_ref[sl, :] = o.astype(o_ref.dtype)
        k_end = (k[sl] * jnp.exp(b_last - b_c)).astype(BF16)
        kv_t = lax.dot_general(v_c, k_end, (((0,), (0,)), ((), ())), preferred_element_type=F32)
        st = jnp.exp(b_last) * st + kv_t * st_mask
    st_ref[...] = st


def _gla_kernel(qf_ref, kf_ref, vf_ref, af_ref, qr_ref, kr_ref, vr_ref, ar_ref, wa_ref, ba_ref,
                of_ref, or_ref, sf_ref, sr_ref):
    @pl.when(pl.program_id(1) == 0)
    def _():
        sf_ref[...] = jnp.zeros_like(sf_ref)
        sr_ref[...] = jnp.zeros_like(sr_ref)

    hk = GLA_HEADS * GLA_DK
    _gla_direction(qf_ref, kf_ref, vf_ref, af_ref, wa_ref[:, :hk], ba_ref[:, :hk], of_ref, sf_ref, False)
    _gla_direction(qr_ref, kr_ref, vr_ref, ar_ref, wa_ref[:, hk:], ba_ref[:, hk:], or_ref, sr_ref, True)


def gla_scan(gq, gk, gv, ga, wa, ba, *, nb, seq, ctx_len):
    n = gq.shape[0]
    blk = SEQ_BLOCK
    assert ctx_len == blk
    lpb = seq // blk
    ctx0 = nb * lpb

    def fwd(b, s):
        return (jnp.where(s == 0, ctx0 + b, b * lpb + s - 1), 0)

    def rev(b, s):
        return (jnp.where(s == 0, ctx0 + b, b * lpb + lpb - s), 0)

    hk = GLA_HEADS * GLA_DK
    hv = GLA_HEADS * GLA_DV
    specs = []
    for imap in (fwd, rev):
        specs += [pl.BlockSpec((blk, hk), imap), pl.BlockSpec((blk, hk), imap),
                  pl.BlockSpec((blk, hv), imap), pl.BlockSpec((blk, LANES), imap)]
    specs += [_resident(wa.shape), _resident(ba.shape)]
    return pl.pallas_call(
        _gla_kernel,
        grid=(nb, lpb + 1),
        in_specs=specs,
        out_specs=[pl.BlockSpec((blk, hv), fwd), pl.BlockSpec((blk, hv), rev)],
        out_shape=[jax.ShapeDtypeStruct((n, hv), BF16)] * 2,
        scratch_shapes=[pltpu.VMEM((hv, hk), F32), pltpu.VMEM((hv, hk), F32)],
        compiler_params=_cparams(("arbitrary", "arbitrary")),
        name="gla_scan",
    )(gq, gk, gv, ga, gq, gk, gv, ga, wa, ba)


NA_ROWS_PER_STEP = 4
NA_UNION_ROWS = NA_WIN_ROWS + NA_ROWS_PER_STEP - 1


def _na_kernel(q_ref, k_ref, v_ref, kc_ref, vc_ref, bias_ref, o_ref, *, n_rows):
    step = pl.program_id(1)
    hd = NA_HEADS * NA_HEAD_DIM
    lane = lax.broadcasted_iota(jnp.int32, (NA_HEADS, hd), 1)
    hrow = lax.broadcasted_iota(jnp.int32, (NA_HEADS, hd), 0)
    head_mask = (lane // NA_HEAD_DIM == hrow).astype(F32)
    kc = kc_ref[...]
    vc = vc_ref[...]
    last = pl.num_programs(1) - 1
    regime = jnp.where(step == 0, 0, jnp.where(step == last, 2, 1))
    ws = jnp.clip(step * NA_ROWS_PER_STEP - NA_WIN_ROWS // 2, 0, n_rows - NA_UNION_ROWS)
    start = pl.multiple_of(ws * GRID_W, GRID_W)
    kw = k_ref[pl.ds(start, NA_UNION_ROWS * GRID_W), :]
    vw = v_ref[pl.ds(start, NA_UNION_ROWS * GRID_W), :]
    q = q_ref[...].astype(F32)
    acc = jnp.zeros(q.shape, F32)
    for h in range(NA_HEADS):
        m_h = head_mask[h:h + 1]
        qh = (q * m_h).astype(BF16)
        s_lat = lax.dot_general(qh, kw, (((1,), (1,)), ((), ())), preferred_element_type=F32)
        s_lat = s_lat + bias_ref[regime, h]
        s_ctx = lax.dot_general(qh, kc, (((1,), (1,)), ((), ())), preferred_element_type=F32)
        m = jnp.maximum(jnp.max(s_lat, axis=1, keepdims=True), jnp.max(s_ctx, axis=1, keepdims=True))
        p_lat = jnp.exp(s_lat - m)
        p_ctx = jnp.exp(s_ctx - m)
        den = jnp.sum(p_lat, axis=1, keepdims=True) + jnp.sum(p_ctx, axis=1, keepdims=True)
        o = jnp.dot(p_lat.astype(BF16), vw, preferred_element_type=F32)
        o += jnp.dot(p_ctx.astype(BF16), vc, preferred_element_type=F32)
        acc += (o / den) * m_h
    o_ref[...] = acc.astype(o_ref.dtype)


def _na_ctx_kernel(q_ref, k_ref, v_ref, o_ref):
    hd = NA_HEADS * NA_HEAD_DIM
    lane = lax.broadcasted_iota(jnp.int32, (NA_HEADS, hd), 1)
    hrow = lax.broadcasted_iota(jnp.int32, (NA_HEADS, hd), 0)
    head_mask = (lane // NA_HEAD_DIM == hrow).astype(F32)
    q = q_ref[...].astype(F32)
    k = k_ref[...]
    v = v_ref[...]
    acc = jnp.zeros(q.shape, F32)
    for h in range(NA_HEADS):
        m_h = head_mask[h:h + 1]
        s = lax.dot_general((q * m_h).astype(BF16), k, (((1,), (1,)), ((), ())), preferred_element_type=F32)
        p = jnp.exp(s - jnp.max(s, axis=1, keepdims=True))
        o = jnp.dot(p.astype(BF16), v, preferred_element_type=F32) / jnp.sum(p, axis=1, keepdims=True)
        acc += o * m_h
    o_ref[...] = acc.astype(o_ref.dtype)


def neighbourhood_attention(nq, nk, nv, bias, *, nb, seq, ctx_len):
    n, hd = nq.shape
    n_rows = seq // GRID_W
    qb = NA_ROWS_PER_STEP * GRID_W
    steps = seq // qb
    ctx0 = nb * seq // ctx_len
    o_lat = pl.pallas_call(
        functools.partial(_na_kernel, n_rows=n_rows),
        grid=(nb, steps),
        in_specs=[pl.BlockSpec((qb, hd), lambda b, s: (b * steps + s, 0)),
                  pl.BlockSpec((seq, hd), lambda b, s: (b, 0)),
                  pl.BlockSpec((seq, hd), lambda b, s: (b, 0)),
                  pl.BlockSpec((ctx_len, hd), lambda b, s: (ctx0 + b, 0)),
                  pl.BlockSpec((ctx_len, hd), lambda b, s: (ctx0 + b, 0)),
                  _resident(bias.shape)],
        out_specs=pl.BlockSpec((qb, hd), lambda b, s: (b * steps + s, 0)),
        out_shape=jax.ShapeDtypeStruct((nb * seq, hd), BF16),
        compiler_params=_cparams(("arbitrary", "arbitrary")),
        name="na_latent",
    )(nq, nk, nv, nk, nv, bias)
    o_ctx = pl.pallas_call(
        _na_ctx_kernel,
        grid=(nb,),
        in_specs=[pl.BlockSpec((ctx_len, hd), lambda b: (ctx0 + b, 0))] * 3,
        out_specs=pl.BlockSpec((ctx_len, hd), lambda b: (b, 0)),
        out_shape=jax.ShapeDtypeStruct((nb * ctx_len, hd), BF16),
        compiler_params=_cparams(("arbitrary",)),
        name="na_context",
    )(nq, nk, nv)
    return jnp.concatenate([o_lat, o_ctx], axis=0)


def na_bias_tables(rpb):
    rpb = rpb.astype(F32)
    wr, nq, nu = NA_WIN_ROWS, NA_ROWS_PER_STEP, NA_UNION_ROWS
    assert nq <= wr // 2 + 1 and wr >= nq + wr // 2
    c_idx = jnp.arange(GRID_W)
    col_start = jnp.clip(c_idx - NA_WIN_COLS // 2, 0, GRID_W - NA_WIN_COLS)
    col_in = (c_idx[None, :] >= col_start[:, None]) & (c_idx[None, :] < col_start[:, None] + NA_WIN_COLS)
    per_regime = []
    for lo_of, dr0_of in ((lambda i: 0, lambda i: wr - 1 - i),
                          (lambda i: i, lambda i: wr // 2 - 1),
                          (lambda i: nu - wr, lambda i: nu - wr - i)):
        rows = []
        for i in range(nq):
            lo, dr0 = lo_of(i), dr0_of(i)
            piece = rpb[:, dr0:dr0 + wr]
            rows.append(jnp.pad(piece, ((0, 0), (lo, nu - wr - lo), (0, 0)), constant_values=NEG_BIG))
        per_regime.append(jnp.stack(rows, axis=1))
    b = jnp.stack(per_regime, axis=1)
    wc = NA_WIN_COLS
    pad = jnp.zeros(b.shape[:-1] + (2 * GRID_W - (2 * wc - 1),), F32)
    table = jnp.concatenate([b[..., wc - 1:], pad, b[..., :wc - 1]], axis=-1)
    b = jnp.tile(table, GRID_W)[..., :GRID_W * (2 * GRID_W - 1)]
    b = b.reshape(table.shape[:-1] + (GRID_W, 2 * GRID_W - 1))[..., :GRID_W]
    b = jnp.where(col_in[None, None, None, None], b, NEG_BIG)
    b = jnp.transpose(b, (1, 0, 2, 4, 3, 5))
    return b.reshape(3, NA_HEADS, nq * GRID_W, nu * GRID_W)


def _conv_kernel(prev_ref, main_ref, next_ref, dw_ref, dwb_ref, lng_ref, lnb_ref, o_ref, buf_ref, shift_ref,
                 *, n_lat_tiles, tiles_per_batch):
    i = pl.program_id(0)
    j = i % tiles_per_batch
    is_lat = i < n_lat_tiles
    has_prev = jnp.logical_and(is_lat, j > 0)
    has_next = jnp.logical_and(is_lat, j < tiles_per_batch - 1)
    cw = CONV_WIDTH
    tl = main_ref.shape[0]

    def glu(a):
        a = a.astype(F32)
        return a[:, :cw] * _sigmoid(a[:, cw:])

    buf_ref[0:CONV_HALO, :] = glu(prev_ref[...]) * has_prev.astype(F32)
    buf_ref[CONV_HALO:CONV_HALO + tl, :] = glu(main_ref[...])
    buf_ref[CONV_HALO + tl:, :] = glu(next_ref[...]) * has_next.astype(F32)
    dw = dw_ref[...]
    acc = jnp.zeros((tl, cw), F32) + dwb_ref[...]
    base = CONV_HALO - CONV_KERNEL // 2
    span = tl + 2 * CONV_HALO - SUBLANES
    for r in range(SUBLANES):
        shift_ref[r] = buf_ref[r:r + span, :]
    for k in range(CONV_KERNEL):
        q, r = divmod(base + k, SUBLANES)
        acc += shift_ref[r, q * SUBLANES:q * SUBLANES + tl, :] * dw[k:k + 1, :]
    mu = jnp.mean(acc, axis=-1, keepdims=True)
    xc = acc - mu
    y = xc * lax.rsqrt(jnp.mean(xc * xc, axis=-1, keepdims=True) + NORM_EPS)
    y = y * lng_ref[...] + lnb_ref[...]
    o_ref[...] = _silu(y).astype(o_ref.dtype)


def conv_branch(cv, dw, dw_b, ln_g, ln_b, *, nb, seq):
    n = cv.shape[0]
    tl = SEQ_BLOCK
    hb = tl // CONV_HALO
    n_tiles = n // tl
    cw = CONV_WIDTH
    return pl.pallas_call(
        functools.partial(_conv_kernel, n_lat_tiles=nb * seq // tl, tiles_per_batch=seq // tl),
        grid=(n_tiles,),
        in_specs=[pl.BlockSpec((CONV_HALO, 2 * cw), lambda i: (jnp.maximum(i * hb - 1, 0), 0)),
                  pl.BlockSpec((tl, 2 * cw), lambda i: (i, 0)),
                  pl.BlockSpec((CONV_HALO, 2 * cw), lambda i: (jnp.minimum((i + 1) * hb, n_tiles * hb - 1), 0)),
                  _resident((CONV_KERNEL, cw)), _resident((1, cw)), _resident((1, cw)), _resident((1, cw))],
        out_specs=pl.BlockSpec((tl, cw), lambda i: (i, 0)),
        out_shape=jax.ShapeDtypeStruct((n, cw), BF16),
        scratch_shapes=[pltpu.VMEM((tl + 2 * CONV_HALO, cw), F32),
                        pltpu.VMEM((SUBLANES, tl + 2 * CONV_HALO - SUBLANES, cw), F32)],
        compiler_params=_cparams(("arbitrary",)),
        name="conv_branch",
    )(cv, cv, cv, dw.astype(F32), dw_b.reshape(1, cw).astype(F32), ln_g.reshape(1, cw).astype(F32),
      ln_b.reshape(1, cw).astype(F32))


def _merge_kernel(xa_ref, xb_ref, mod_ref, gt_ref, ys5a_ref, ys5b_ref, of_ref, or_ref, gr_ref, na_ref, cv_ref,
                  gate_b_ref, wglu_ref, bglu_ref, ws5_ref, gng_ref, wgla_ref, wna_ref, wcv_ref, wmix_ref, o_ref,
                  *, n_lat_tiles):
    d = xa_ref.shape[1]
    z = jax.nn.gelu(jnp.concatenate([ys5a_ref[...], ys5b_ref[...]], axis=1))
    z = z * _sigmoid(jnp.dot(z.astype(BF16), wglu_ref[...], preferred_element_type=F32) + bglu_ref[...])
    br_s5 = jnp.dot(z.astype(BF16), ws5_ref[...], preferred_element_type=F32)
    o = of_ref[...].astype(F32) + or_ref[...].astype(F32)
    r = _silu(gr_ref[...].astype(F32))
    parts = []
    for h in range(GLA_HEADS):
        oh = o[:, h * GLA_DV:(h + 1) * GLA_DV]
        oh = oh * lax.rsqrt(jnp.mean(oh * oh, axis=-1, keepdims=True) + NORM_EPS) * gng_ref[...]
        parts.append(oh * r[:, h * GLA_DV:(h + 1) * GLA_DV])
    y_gla = jnp.concatenate(parts, axis=1).astype(BF16)
    br_gla = jnp.dot(y_gla, wgla_ref[...], preferred_element_type=F32)
    br_na = jnp.dot(na_ref[...], wna_ref[...], preferred_element_type=F32)
    br_cv = jnp.dot(cv_ref[...], wcv_ref[...], preferred_element_type=F32)
    merged = jnp.zeros((xa_ref.shape[0], d), F32)
    for i, br in enumerate((br_s5, br_gla, br_na, br_cv)):
        pre = gt_ref[:, i * d:(i + 1) * d] + gate_b_ref[:, i * d:(i + 1) * d].astype(BF16)
        gate = 0.5 * jnp.tanh(0.5 * pre).astype(F32) + 0.5
        merged += gate * br
    mix = jnp.dot(merged.astype(BF16), wmix_ref[...], preferred_element_type=F32)
    o_ref[...] = _x_pair_tile(xa_ref, xb_ref, n_lat_tiles) + mod_ref[:, 2 * d:3 * d] * mix


def merge_branches(x_lat, x_ctx, ctx_tile0, modtab, gt, ys5a, ys5b, o_f, o_r, gr, o_na, y_cv, weights,
                   *, n_rows, nb, seq, tm):
    d = x_lat.shape[1]
    n_lat_tiles = nb * seq // tm
    tpb = seq // tm

    def row(i):
        return (i, 0)

    acts = (gt, ys5a, ys5b, o_f, o_r, gr, o_na, y_cv)
    return pl.pallas_call(
        functools.partial(_merge_kernel, n_lat_tiles=n_lat_tiles),
        grid=(n_rows // tm,),
        in_specs=_x_pair_specs(tm, d, n_lat_tiles, ctx_tile0)
        + [_mod_row_spec(n_lat_tiles, tpb, nb, modtab.shape[-1])]
        + [pl.BlockSpec((tm, a.shape[1]), row) for a in acts]
        + [_resident(w.shape) for w in weights],
        out_specs=pl.BlockSpec((tm, d), row),
        out_shape=jax.ShapeDtypeStruct((n_rows, d), F32),
        compiler_params=_cparams(("arbitrary",)),
        name="merge_mix",
    )(x_lat, x_ctx, modtab, *acts, *weights)


MOE_BLOCK = 128


def _route_kernel(x_ref, mod_ref, g_ref, wr_ref, br_ref, ltri_ref, h_ref, comb_ref, meta_ref, cnt_ref):
    d = x_ref.shape[1]
    mod = mod_ref[...]
    h = _modulated_norm(x_ref[...], g_ref[...], mod[:, 3 * d:4 * d], mod[:, 4 * d:5 * d])
    h_ref[...] = h.astype(BF16)
    logits = jnp.dot(h, wr_ref[...], preferred_element_type=F32, precision=HIGHEST) + br_ref[...]
    lane = lax.broadcasted_iota(jnp.int32, logits.shape, 1)
    big = jnp.int32(1 << 20)
    is_g = lane < MOE_GROUPS
    gl = jnp.where(is_g, logits, -jnp.inf)
    gmax = jnp.max(gl, axis=1, keepdims=True)
    gidx = jnp.min(jnp.where(gl == gmax, lane, big), axis=1, keepdims=True)
    group_p = 1.0 / jnp.sum(jnp.where(is_g, jnp.exp(logits - gmax), 0.0), axis=1, keepdims=True)
    first = MOE_GROUPS + gidx * MOE_EXPERTS_PER_GROUP
    in_group = jnp.logical_and(lane >= first, lane < first + MOE_EXPERTS_PER_GROUP)
    el = jnp.where(in_group, logits, -jnp.inf)
    v1 = jnp.max(el, axis=1, keepdims=True)
    i1 = jnp.min(jnp.where(el == v1, lane, big), axis=1, keepdims=True)
    el2 = jnp.where(lane == i1, -jnp.inf, el)
    v2 = jnp.max(el2, axis=1, keepdims=True)
    i2 = jnp.min(jnp.where(el2 == v2, lane, big), axis=1, keepdims=True)
    t = jnp.exp(v2 - v1)
    w1 = group_p / (1.0 + t)
    w2 = group_p * t / (1.0 + t)
    k1 = i1 - first
    k2 = i2 - first
    epg = MOE_EXPERTS_PER_GROUP
    comb = (jnp.where(jnp.logical_or(lane == k1, lane == k1 + epg), w1, 0.0)
            + jnp.where(jnp.logical_or(lane == k2, lane == k2 + epg), w2, 0.0))
    head = comb.astype(BF16).astype(F32)
    comb_ref[...] = jnp.where(lane < epg, head, comb - head).astype(BF16)
    onehot = (lane == gidx).astype(BF16)
    rank_all = jnp.dot(ltri_ref[...], onehot, preferred_element_type=F32)
    rank = jnp.sum(jnp.where(lane == gidx, rank_all, 0.0), axis=1, keepdims=True)
    packed = jnp.where(lane == 0, gidx.astype(F32), jnp.where(lane == 1, rank, 0.0))
    meta_ref[0] = jnp.transpose(packed)[0:8, :]
    counts = jnp.sum(onehot.astype(F32), axis=0, keepdims=True)
    cnt_ref[0] = jnp.broadcast_to(counts, cnt_ref.shape[1:]).astype(jnp.int32)


def _moe_kernel(cnt_ref, x_ref, mod_ref, h_ref, comb_ref, meta_ref, w1_ref, w3_ref, w2_ref, fg_ref, o_ref,
                *, final):
    t = pl.program_id(0)
    g = pl.program_id(1)
    tm, d = x_ref.shape

    @pl.when(g == 0)
    def _():
        o_ref[...] = jnp.zeros_like(o_ref)

    n_tok = cnt_ref[t * MOE_GROUPS + g]
    gid_row = meta_ref[0, 0:1, :]
    rank_row = meta_ref[0, 1:2, :]
    in_grp = gid_row == g.astype(F32)
    row = lax.broadcasted_iota(jnp.int32, (MOE_BLOCK, tm), 0).astype(F32)

    def block(blk, carry):
        sel = jnp.logical_and(in_grp, rank_row == row + (blk * MOE_BLOCK).astype(F32))
        p = sel.astype(BF16)
        xg = jnp.dot(p, h_ref[...], preferred_element_type=F32).astype(BF16)
        cw = jnp.dot(p, comb_ref[...], preferred_element_type=F32)
        yg = jnp.zeros((MOE_BLOCK, d), F32)
        for e in range(MOE_EXPERTS_PER_GROUP):
            a = jnp.dot(xg, w1_ref[e], preferred_element_type=F32)
            b = jnp.dot(xg, w3_ref[e], preferred_element_type=F32)
            cw_e = cw[:, e:e + 1] + cw[:, MOE_EXPERTS_PER_GROUP + e:MOE_EXPERTS_PER_GROUP + e + 1]
            act = (_silu(a) * b * cw_e).astype(BF16)
            yg += jnp.dot(act, w2_ref[e], preferred_element_type=F32)
        o_ref[...] += lax.dot_general(p, yg.astype(BF16), (((0,), (0,)), ((), ())), preferred_element_type=F32)
        return carry

    lax.fori_loop(0, (n_tok + MOE_BLOCK - 1) // MOE_BLOCK, block, 0)

    @pl.when(g == pl.num_programs(1) - 1)
    def _():
        y = x_ref[...] + mod_ref[:, 5 * d:6 * d] * o_ref[...]
        if final:
            y = y * lax.rsqrt(jnp.mean(y * y, axis=-1, keepdims=True) + NORM_EPS) * fg_ref[...]
        o_ref[...] = y


def moe_layer(xs, modtab, norm_g, w_router, b_router, w1, w3, w2, final_g, *, layer, n_rows, nb, seq, tm, final):
    d = xs.shape[1]
    n_lat_tiles = nb * seq // tm
    tpb = seq // tm
    n_tiles = n_rows // tm
    ltri = (lax.broadcasted_iota(jnp.int32, (tm, tm), 0) > lax.broadcasted_iota(jnp.int32, (tm, tm), 1)).astype(BF16)
    mod_spec = _mod_row_spec(n_lat_tiles, tpb, nb, modtab.shape[-1])
    h, comb, meta, cnt = pl.pallas_call(
        _route_kernel,
        grid=(n_tiles,),
        in_specs=[pl.BlockSpec((tm, d), lambda i: (i, 0)), mod_spec,
                  _resident((1, d)), _resident(w_router.shape), _resident(b_router.shape), _resident(ltri.shape)],
        out_specs=[pl.BlockSpec((tm, d), lambda i: (i, 0)), pl.BlockSpec((tm, LANES), lambda i: (i, 0)),
                   pl.BlockSpec((1, 8, tm), lambda i: (i, 0, 0)), pl.BlockSpec((1, 8, LANES), lambda i: (i, 0, 0))],
        out_shape=[jax.ShapeDtypeStruct((n_rows, d), BF16), jax.ShapeDtypeStruct((n_rows, LANES), BF16),
                   jax.ShapeDtypeStruct((n_tiles, 8, tm), F32), jax.ShapeDtypeStruct((n_tiles, 8, LANES), jnp.int32)],
        compiler_params=_cparams(("arbitrary",)),
        name="moe_route",
    )(xs, modtab, norm_g.reshape(1, d), w_router, b_router, ltri)
    counts = cnt[:, 0, :MOE_GROUPS].reshape(n_tiles * MOE_GROUPS)
    epg, hid = w1.shape[2], w1.shape[-1]
    grid_spec = pltpu.PrefetchScalarGridSpec(
        num_scalar_prefetch=1,
        grid=(n_tiles, MOE_GROUPS),
        in_specs=[pl.BlockSpec((tm, d), lambda i, g, c: (i, 0)),
                  pl.BlockSpec((None, 1, modtab.shape[-1]),
                               lambda i, g, c: (jnp.where(i < n_lat_tiles, i // tpb, nb), 0, 0)),
                  pl.BlockSpec((tm, d), lambda i, g, c: (i, 0)),
                  pl.BlockSpec((tm, LANES), lambda i, g, c: (i, 0)),
                  pl.BlockSpec((1, 8, tm), lambda i, g, c: (i, 0, 0)),
                  pl.BlockSpec((None, None, epg, d, hid), lambda i, g, c: (layer, g, 0, 0, 0)),
                  pl.BlockSpec((None, None, epg, d, hid), lambda i, g, c: (layer, g, 0, 0, 0)),
                  pl.BlockSpec((None, None, epg, hid, d), lambda i, g, c: (layer, g, 0, 0, 0)),
                  pl.BlockSpec((1, d), lambda i, g, c: (0, 0))],
        out_specs=pl.BlockSpec((tm, d), lambda i, g, c: (i, 0)),
    )
    return pl.pallas_call(
        functools.partial(_moe_kernel, final=final),
        grid_spec=grid_spec,
        out_shape=jax.ShapeDtypeStruct((n_rows, d), F32),
        compiler_params=_cparams(("arbitrary", "arbitrary")),
        name="moe_experts",
    )(counts, xs, modtab, h, comb, meta, w1, w3, w2, final_g.reshape(1, d).astype(F32))


def rope_tables(seq, pad_rows):
    t = jnp.arange(seq, dtype=jnp.int32)
    row = (t // GRID_W).astype(F32)
    colp = (t % GRID_W).astype(F32)
    half = GLA_DK // 2
    inv_freq = ROPE_BASE ** (-jnp.arange(0, half, 2, dtype=F32) / half)
    ang_r = row[:, None] * inv_freq
    ang_c = colp[:, None] * inv_freq
    dd = jnp.arange(GLA_DK)
    ang = jnp.where((dd < half)[None, :], ang_r[:, dd % (half // 2)], ang_c[:, dd % (half // 2)])
    sign = jnp.where((dd % half) < half // 2, -1.0, 1.0).astype(F32)
    cos = jnp.tile(jnp.cos(ang), (1, GLA_HEADS))
    sin = jnp.tile(jnp.sin(ang) * sign[None, :], (1, GLA_HEADS))
    cos = jnp.concatenate([cos, jnp.ones((pad_rows, cos.shape[1]), F32)], axis=0)
    sin = jnp.concatenate([sin, jnp.zeros((pad_rows, sin.shape[1]), F32)], axis=0)
    return cos, sin


def split_in_weights(w_in, d):
    widths = (S5_WIDTH, GLA_HEADS * GLA_DK, GLA_HEADS * GLA_DK, GLA_HEADS * GLA_DV, GLA_HEADS * GLA_DV,
              2 * GLA_GATE_RANK, NA_HEADS * NA_HEAD_DIM, NA_HEADS * NA_HEAD_DIM, NA_HEADS * NA_HEAD_DIM,
              2 * CONV_WIDTH, N_BRANCHES * d)
    names = ('u', 'gq', 'gk', 'gv', 'gr', 'ga', 'nq', 'nk', 'nv', 'cv', 'gt')
    parts = {}
    col = 0
    for nme, w in zip(names, widths):
        parts[nme] = w_in[:, col:col + w]
        col += w
    swap = jnp.arange(GLA_HEADS * GLA_DK) ^ (GLA_DK // 4)
    gq = parts['gq'] * (GLA_DK ** -0.5)
    wqk = jnp.concatenate([gq, gq[:, swap], parts['gk'], parts['gk'][:, swap]], axis=1)
    ga = jnp.concatenate([parts['ga']] * 3 + [jnp.zeros((d, LANES - 6 * GLA_GATE_RANK), w_in.dtype)], axis=1)
    wmisc = jnp.concatenate([parts['u'], parts['nq'] * (NA_HEAD_DIM ** -0.5), parts['nk'], parts['nv'],
                             parts['gv'], parts['gr'], parts['cv'], ga], axis=1)
    return wqk.astype(BF16), wmisc.astype(BF16), parts['gt'].astype(BF16)


def kernel(x, c, ctx, c_ctx, norm1_g, norm2_g, w_mod, b_mod, w_in, gate_b, w_mix_out, s5_lam_re, s5_lam_im, s5_log_dt, s5_b_re, s5_b_im, s5_c_re, s5_c_im, s5_d, s5_w_glu, s5_b_glu, s5_w_out, gla_w_a2, gla_b_a, gla_norm_g, gla_w_out, na_rpb, na_w_out, conv_dw, conv_dw_b, conv_ln_g, conv_ln_b, conv_w_out, moe_w_group, moe_b_group, moe_w_expert, moe_b_expert, moe_w1, moe_w3, moe_w2, final_norm_g):
    nb, seq, d = x.shape
    ctx_len = ctx.shape[1]
    depth = w_mod.shape[0]
    n_lat = nb * seq
    tm = 512
    tm_moe = math.gcd(1024, nb * ctx_len)
    assert ctx_len == SEQ_BLOCK and seq % tm_moe == 0 and (nb * ctx_len) % tm_moe == 0 and nb < MOD_ROWS

    n_all = n_lat + nb * ctx_len
    x_lat, x_ctx, ctx_tile0 = x.reshape(n_lat, d).astype(F32), ctx.reshape(nb * ctx_len, d).astype(F32), 0
    c_rows = jnp.zeros((MOD_ROWS, d), F32).at[:nb].set(c.astype(F32)).at[nb].set(c_ctx.astype(F32))
    modtab = modulation_table(c_rows, w_mod.astype(F32), b_mod.astype(F32))
    modtab = modtab.reshape(depth, MOD_ROWS, 1, 6 * d)
    cos_tab, sin_tab = rope_tables(seq, tm)

    moe_w1_bf, moe_w3_bf, moe_w2_bf = moe_w1.astype(BF16), moe_w3.astype(BF16), moe_w2.astype(BF16)

    for i in range(depth):
        last = i == depth - 1
        n_rows = n_lat if last else n_all
        wqk, wmisc, wgate = split_in_weights(w_in[i], d)
        gq, gk, ua, ub, nq, nk, nv, gv, gr, cv, ga, gt = in_projection(
            x_lat, x_ctx, ctx_tile0, modtab[i], norm1_g[i].astype(F32), cos_tab, sin_tab, wqk, wmisc, wgate,
            n=n_all, nb=nb, seq=seq, tm=tm)

        mats = s5_matrices(s5_lam_re[i], s5_lam_im[i], s5_log_dt[i], s5_b_re[i], s5_b_im[i],
                           s5_c_re[i], s5_c_im[i], s5_d[i], nb)
        ys5a, ys5b = s5_mixer(ua, ub, mats, nb=nb, seq=seq, ctx_len=ctx_len)

        hk = GLA_HEADS * GLA_DK
        zero = jnp.zeros((GLA_GATE_RANK, hk), F32)
        wd = jnp.concatenate([jnp.concatenate([gla_w_a2[i, 0].astype(F32), zero], axis=1),
                              jnp.concatenate([zero, gla_w_a2[i, 1].astype(F32)], axis=1)], axis=0)
        wd_head, wd_rem = _split_bf16(wd)
        wa = jnp.concatenate([wd_head, wd_head, wd_rem,
                              jnp.zeros((LANES - 6 * GLA_GATE_RANK, 2 * hk), BF16)], axis=0)
        ba = gla_b_a[i].astype(F32).reshape(1, 2 * hk)
        o_f, o_r = gla_scan(gq, gk, gv, ga, wa, ba, nb=nb, seq=seq, ctx_len=ctx_len)

        o_na = neighbourhood_attention(nq, nk, nv, na_bias_tables(na_rpb[i]), nb=nb, seq=seq, ctx_len=ctx_len)
        y_cv = conv_branch(cv, conv_dw[i], conv_dw_b[i], conv_ln_g[i], conv_ln_b[i], nb=nb, seq=seq)

        weights = (gate_b[i].astype(F32).reshape(1, N_BRANCHES * d), s5_w_glu[i].astype(BF16),
                   s5_b_glu[i].astype(F32).reshape(1, S5_WIDTH), s5_w_out[i].astype(BF16),
                   gla_norm_g[i].astype(F32).reshape(1, GLA_DV), gla_w_out[i].astype(BF16),
                   na_w_out[i].astype(BF16), conv_w_out[i].astype(BF16), w_mix_out[i].astype(BF16))
        xs = merge_branches(x_lat, x_ctx, ctx_tile0, modtab[i], gt, ys5a, ys5b, o_f, o_r, gr, o_na, y_cv, weights,
                            n_rows=n_rows, nb=nb, seq=seq, tm=tm)

        n_router = MOE_GROUPS + N_EXPERTS
        w_router = jnp.pad(jnp.concatenate([moe_w_group[i], moe_w_expert[i]], axis=1).astype(F32),
                           ((0, 0), (0, LANES - n_router)))
        b_router = jnp.pad(jnp.concatenate([moe_b_group[i], moe_b_expert[i]]).astype(F32),
                           (0, LANES - n_router)).reshape(1, LANES)
        xs = moe_layer(xs, modtab[i], norm2_g[i].astype(F32), w_router, b_router,
                       moe_w1_bf, moe_w3_bf, moe_w2_bf, final_norm_g, layer=i,
                       n_rows=n_rows, nb=nb, seq=seq, tm=tm_moe, final=last)
        x_lat, x_ctx, ctx_tile0 = xs, xs, n_lat // tm

    return xs.reshape(nb, seq, d).astype(x.dtype)
```

```python
import functools
import math

import jax
import jax.numpy as jnp
from jax import lax
from jax.experimental import pallas as pl
from jax.experimental.pallas import tpu as pltpu

F32 = jnp.float32
BF16 = jnp.bfloat16
HIGHEST = lax.Precision.HIGHEST

GRID_W = 64
NORM_EPS = 1e-6
N_BRANCHES = 4
S5_WIDTH = 256
S5_GROUP_SIZE = 16
S5_GROUPS = 16
S5_STATE = 64
GLA_HEADS = 4
GLA_DK = 64
GLA_DV = 128
GLA_GATE_RANK = 16
GLA_TAU = 16.0
GLA_CHUNK = 64
ROPE_BASE = 10000.0
NA_HEADS = 4
NA_HEAD_DIM = 64
NA_WIN_ROWS = 8
NA_WIN_COLS = 16
CONV_WIDTH = 256
CONV_KERNEL = 31
MOE_GROUPS = 4
MOE_EXPERTS_PER_GROUP = 8
MOE_HIDDEN = 256
N_EXPERTS = MOE_GROUPS * MOE_EXPERTS_PER_GROUP

LANES = 128
SUBLANES = 8
MOD_ROWS = 8
VMEM_LIMIT = 56 * 1024 * 1024
S5_CHUNK = 32
SEQ_BLOCK = 256
CONV_HALO = 16
NEG_BIG = -1e30


def _cparams(sem):
    return pltpu.CompilerParams(dimension_semantics=sem, vmem_limit_bytes=VMEM_LIMIT)


def _resident(shape):
    nd = len(shape)
    return pl.BlockSpec(shape, lambda *_: (0,) * nd, pipeline_mode=pl.Buffered(1))


def _sigmoid(x):
    return 0.5 * jnp.tanh(0.5 * x) + 0.5


def _silu(x):
    return x * _sigmoid(x)


def _mod_kernel(c_ref, w_ref, b_ref, o_ref):
    c = c_ref[...]
    o_ref[0] = jnp.dot(_silu(c), w_ref[0], preferred_element_type=F32, precision=HIGHEST) + b_ref[0]


def modulation_table(c_rows, w_mod, b_mod):
    depth, d, n6 = w_mod.shape
    tn = 1024
    return pl.pallas_call(
        _mod_kernel,
        grid=(depth, n6 // tn),
        in_specs=[
            pl.BlockSpec((MOD_ROWS, d), lambda l, j: (0, 0)),
            pl.BlockSpec((1, d, tn), lambda l, j: (l, 0, j)),
            pl.BlockSpec((1, 1, tn), lambda l, j: (l, 0, j)),
        ],
        out_specs=pl.BlockSpec((1, MOD_ROWS, tn), lambda l, j: (l, 0, j)),
        out_shape=jax.ShapeDtypeStruct((depth, MOD_ROWS, n6), F32),
        compiler_params=_cparams(("arbitrary", "arbitrary")),
        name="mod_table",
    )(c_rows, w_mod, b_mod.reshape(depth, 1, n6))


def _mod_row_spec(n_lat_tiles, tiles_per_batch, nb, width):
    def imap(i, *_):
        return (jnp.where(i < n_lat_tiles, i // tiles_per_batch, nb), 0, 0)
    return pl.BlockSpec((None, 1, width), imap)


def _modulated_norm(x, g, shift, scale):
    y = x * lax.rsqrt(jnp.mean(x * x, axis=-1, keepdims=True) + NORM_EPS)
    return (y * g) * (1.0 + scale) + shift


IN_CHUNK = 512


def _x_pair_specs(tm, d, n_lat_tiles, ctx_tile0):
    return [pl.BlockSpec((tm, d), lambda i: (jnp.minimum(i, n_lat_tiles - 1), 0)),
            pl.BlockSpec((tm, d), lambda i: (ctx_tile0 + jnp.maximum(i - n_lat_tiles, 0), 0))]


def _x_pair_tile(xa_ref, xb_ref, n_lat_tiles):
    return jnp.where(pl.program_id(0) < n_lat_tiles, xa_ref[...], xb_ref[...])


def _inproj_kernel(xa_ref, xb_ref, mod_ref, g_ref, cos_ref, sin_ref, wqk_ref, wmisc_ref, wgate_ref,
                   gq_ref, gk_ref, ua_ref, ub_ref, nq_ref, nk_ref, nv_ref, gv_ref, gr_ref, cv_ref, ga_ref, gt_ref,
                   *, n_lat_tiles):
    d = xa_ref.shape[1]
    mod = mod_ref[...]
    x = _x_pair_tile(xa_ref, xb_ref, n_lat_tiles)
    h = _modulated_norm(x, g_ref[...], mod[:, 0:d], mod[:, d:2 * d]).astype(BF16)
    cos = cos_ref[...]
    sin = sin_ref[...]
    for j, o_ref in enumerate((gq_ref, gk_ref)):
        y = jnp.dot(h, wqk_ref[:, j * 512:(j + 1) * 512], preferred_element_type=F32)
        o_ref[...] = (y[:, :256] * cos + y[:, 256:] * sin).astype(o_ref.dtype)
    col = 0
    for o_ref in (ua_ref, ub_ref, nq_ref, nk_ref, nv_ref, gv_ref, gr_ref, cv_ref, ga_ref):
        w = o_ref.shape[1]
        o_ref[...] = jnp.dot(h, wmisc_ref[:, col:col + w], preferred_element_type=F32).astype(o_ref.dtype)
        col += w
    for j in range(gt_ref.shape[1] // IN_CHUNK):
        sl = slice(j * IN_CHUNK, (j + 1) * IN_CHUNK)
        gt_ref[:, sl] = jnp.dot(h, wgate_ref[:, sl], preferred_element_type=F32).astype(gt_ref.dtype)


def in_projection(x_lat, x_ctx, ctx_tile0, modtab, norm_g, cos_tab, sin_tab, wqk, wmisc, wgate, *, n, nb, seq, tm):
    d = x_lat.shape[1]
    n_lat_tiles = nb * seq // tm
    tpb = seq // tm
    widths = (256, 256, LANES, LANES, 256, 256, 256, 512, 512, 512, LANES, N_BRANCHES * d)
    dtypes = (BF16, BF16, F32, F32) + (BF16,) * 6 + (F32, BF16)

    def row(i):
        return (i, 0)

    def rope_row(i):
        return (jnp.where(i < n_lat_tiles, i % tpb, tpb), 0)

    return pl.pallas_call(
        functools.partial(_inproj_kernel, n_lat_tiles=n_lat_tiles),
        grid=(n // tm,),
        in_specs=_x_pair_specs(tm, d, n_lat_tiles, ctx_tile0) + [
            _mod_row_spec(n_lat_tiles, tpb, nb, modtab.shape[-1]),
            _resident((1, d)),
            pl.BlockSpec((tm, 256), rope_row),
            pl.BlockSpec((tm, 256), rope_row),
            _resident(wqk.shape),
            _resident(wmisc.shape),
            _resident(wgate.shape),
        ],
        out_specs=[pl.BlockSpec((tm, w), row) for w in widths],
        out_shape=[jax.ShapeDtypeStruct((n, w), dt) for w, dt in zip(widths, dtypes)],
        compiler_params=_cparams(("arbitrary",)),
        name="in_proj",
    )(x_lat, x_ctx, modtab, norm_g.reshape(1, d), cos_tab, sin_tab, wqk, wmisc, wgate)


S5_TAUS_PER_TILE = LANES // S5_GROUP_SIZE
S5_PERM = S5_TAUS_PER_TILE * S5_WIDTH


def _s5_state_kernel(uca_ref, ucb_ref, ula_ref, ulb_ref, perm_ref, w_ref, x_ref, s_ref):
    n_ctx, n_lat = uca_ref.shape[0] // S5_CHUNK, ula_ref.shape[0] // S5_CHUNK
    for v in range(S5_CHUNK // S5_TAUS_PER_TILE):
        pieces = []
        for w in range(S5_TAUS_PER_TILE):
            tau = v * S5_TAUS_PER_TILE + w
            for uc_ref, ul_ref in ((uca_ref, ula_ref), (ucb_ref, ulb_ref)):
                pieces.append(jnp.concatenate([uc_ref[pl.ds(tau, n_ctx, stride=S5_CHUNK), :],
                                               ul_ref[pl.ds(tau, n_lat, stride=S5_CHUNK), :]], axis=0))
        z = jnp.concatenate(pieces, axis=1).astype(BF16)
        xv = jnp.dot(z, perm_ref[...], preferred_element_type=F32).astype(BF16)
        for g in range(S5_GROUPS):
            col = g * S5_CHUNK * S5_GROUP_SIZE + v * LANES
            x_ref[0, :, col:col + LANES] = xv[:, g * LANES:(g + 1) * LANES]
    cw = S5_CHUNK * S5_GROUP_SIZE
    for g in range(S5_GROUPS):
        s = jnp.dot(x_ref[0, :, g * cw:(g + 1) * cw], w_ref[g], preferred_element_type=F32)
        s_ref[0, 0, g] = s[:, :2 * S5_STATE]
        s_ref[1, 0, g] = s[:, 2 * S5_STATE:]


def _s5_scan_kernel(s_ref, a_ref, h_ref, *, n_chunks, n_ctx_chunks):
    d = pl.program_id(0)
    nc = n_chunks
    n_pairs = s_ref.shape[1] // nc
    a1 = a_ref[0, 0]
    a2 = a_ref[0, 1]

    def step(s, h):
        fwd_row = s
        rev_row = jnp.where(s < n_ctx_chunks, n_ctx_chunks - 1 - s, nc - 1 - (s - n_ctx_chunks))
        rows = pl.ds(jnp.where(d == 0, fwd_row, rev_row), n_pairs, stride=nc)
        h_ref[0, rows, :] = h
        return h * a1 + pltpu.roll(h, S5_STATE, axis=1) * a2 + s_ref[0, rows, :]

    lax.fori_loop(0, nc, step, jnp.zeros((n_pairs, s_ref.shape[2]), F32))


def _s5_out_kernel(x_ref, h_ref, perm_ref, t_ref, v_ref, yca_ref, ycb_ref, yla_ref, ylb_ref, y_scr):
    cw = S5_CHUNK * S5_GROUP_SIZE
    n_ctx = yca_ref.shape[0] // S5_CHUNK
    for g in range(S5_GROUPS):
        y = jnp.dot(x_ref[0, :, g * cw:(g + 1) * cw], t_ref[g], preferred_element_type=F32)
        h = jnp.concatenate([h_ref[0, 0, g], h_ref[1, 0, g]], axis=1)
        y += jnp.dot(h.astype(BF16), v_ref[g], preferred_element_type=F32)
        y_scr[:, g * cw:(g + 1) * cw] = y.astype(BF16)
    for v in range(S5_CHUNK // S5_TAUS_PER_TILE):
        yv = jnp.concatenate([y_scr[:, g * cw + v * LANES:g * cw + (v + 1) * LANES] for g in range(S5_GROUPS)],
                             axis=1)
        zv = lax.dot_general(yv, perm_ref[...], (((1,), (1,)), ((), ())), preferred_element_type=F32)
        for w in range(S5_TAUS_PER_TILE):
            tau = v * S5_TAUS_PER_TILE + w
            for hf, (yc_ref, yl_ref) in enumerate(((yca_ref, yla_ref), (ycb_ref, ylb_ref))):
                col = w * S5_WIDTH + hf * LANES
                piece = zv[:, col:col + LANES]
                yc_ref[pl.ds(tau, n_ctx, stride=S5_CHUNK), :] = piece[:n_ctx]
                yl_ref[pl.ds(tau, piece.shape[0] - n_ctx, stride=S5_CHUNK), :] = piece[n_ctx:]


def s5_mixer(ua, ub, mats, *, nb, seq, ctx_len):
    t_sum, w_cat, v_cat, a12 = mats
    g = S5_GROUPS
    nc = (seq + ctx_len) // S5_CHUNK
    n_ctx_chunks = ctx_len // S5_CHUNK
    cw = S5_CHUNK * S5_GROUP_SIZE
    sl = 2 * S5_STATE
    ctx0 = nb * seq // ctx_len
    src = (lax.broadcasted_iota(jnp.int32, (S5_PERM, S5_PERM), 0))
    dst = (lax.broadcasted_iota(jnp.int32, (S5_PERM, S5_PERM), 1))
    src_as_dst = ((src % S5_WIDTH) // S5_GROUP_SIZE) * LANES + (src // S5_WIDTH) * S5_GROUP_SIZE + src % S5_GROUP_SIZE
    perm = (src_as_dst == dst).astype(BF16)
    x_gm, s = pl.pallas_call(
        _s5_state_kernel,
        grid=(nb,),
        in_specs=[pl.BlockSpec((ctx_len, LANES), lambda b: (ctx0 + b, 0))] * 2
        + [pl.BlockSpec((seq, LANES), lambda b: (b, 0))] * 2
        + [_resident(perm.shape), _resident(w_cat.shape)],
        out_specs=[pl.BlockSpec((1, nc, g * cw), lambda b: (b, 0, 0)),
                   pl.BlockSpec((2, 1, g, nc, sl), lambda b: (0, b, 0, 0, 0))],
        out_shape=[jax.ShapeDtypeStruct((nb, nc, g * cw), BF16), jax.ShapeDtypeStruct((2, nb, g, nc, sl), F32)],
        compiler_params=_cparams(("arbitrary",)),
        name="s5_chunk_state",
    )(ua, ub, ua, ub, perm, w_cat)
    h = pl.pallas_call(
        functools.partial(_s5_scan_kernel, n_chunks=nc, n_ctx_chunks=n_ctx_chunks),
        grid=(2,),
        in_specs=[pl.BlockSpec((1, nb * g * nc, sl), lambda d: (d, 0, 0)),
                  pl.BlockSpec((1, 2, nb * g, sl), lambda d: (d, 0, 0, 0))],
        out_specs=pl.BlockSpec((1, nb * g * nc, sl), lambda d: (d, 0, 0)),
        out_shape=jax.ShapeDtypeStruct((2, nb * g * nc, sl), F32),
        compiler_params=_cparams(("arbitrary",)),
        name="s5_chunk_scan",
    )(s.reshape(2, nb * g * nc, sl), a12).reshape(2, nb, g, nc, sl)
    yca, ycb, yla, ylb = pl.pallas_call(
        _s5_out_kernel,
        grid=(nb,),
        in_specs=[pl.BlockSpec((1, nc, g * cw), lambda b: (b, 0, 0), pipeline_mode=pl.Buffered(1)),
                  pl.BlockSpec((2, 1, g, nc, sl), lambda b: (0, b, 0, 0, 0), pipeline_mode=pl.Buffered(1)),
                  _resident(perm.shape), _resident(t_sum.shape), _resident(v_cat.shape)],
        out_specs=[pl.BlockSpec((ctx_len, LANES), lambda b: (b, 0))] * 2
        + [pl.BlockSpec((seq, LANES), lambda b: (b, 0))] * 2,
        out_shape=[jax.ShapeDtypeStruct((nb * ctx_len, LANES), F32)] * 2
        + [jax.ShapeDtypeStruct((nb * seq, LANES), F32)] * 2,
        scratch_shapes=[pltpu.VMEM((nc, g * cw), BF16)],
        compiler_params=_cparams(("arbitrary",)),
        name="s5_readout",
    )(x_gm, h, perm, t_sum, v_cat)
    return jnp.concatenate([yla, yca], axis=0), jnp.concatenate([ylb, ycb], axis=0)


def s5_matrices(lam_re, lam_im, log_dt, b_re, b_im, c_re, c_im, d_skip, nb):
    ch = S5_CHUNK
    gsz = S5_GROUP_SIZE
    dt = jnp.exp(log_dt.astype(F32))[..., None]
    lr = lam_re.astype(F32)
    li = lam_im.astype(F32)

    def power(n):
        n = n.astype(F32)[:, None, None, None]
        mag = jnp.exp(lr * dt * n)
        return mag * jnp.cos(li * dt * n), mag * jnp.sin(li * dt * n)

    ab_re, ab_im = power(jnp.ones((1,), F32))
    ab_re, ab_im = ab_re[0], ab_im[0]
    den = lr * lr + li * li
    nr = ab_re - 1.0
    ni = ab_im
    coef_re = (nr * lr + ni * li) / den
    coef_im = (ni * lr - nr * li) / den
    br = b_re.astype(F32)
    bi = b_im.astype(F32)
    bb_re = coef_re[..., None] * br - coef_im[..., None] * bi
    bb_im = coef_re[..., None] * bi + coef_im[..., None] * br
    cr = c_re.astype(F32)
    ci = c_im.astype(F32)

    p_re, p_im = power(jnp.arange(ch + 1))
    ca_re = cr[None] * p_re[:, :, :, None, :] - ci[None] * p_im[:, :, :, None, :]
    ca_im = cr[None] * p_im[:, :, :, None, :] + ci[None] * p_re[:, :, :, None, :]
    kmat = jnp.einsum('ndgip,dgpj->ndgij', jnp.concatenate([ca_re[:ch], -ca_im[:ch]], axis=-1),
                      jnp.concatenate([bb_re, bb_im], axis=-2), precision=HIGHEST)
    lags = jnp.concatenate([jnp.flip(kmat[1:, 1], axis=0), kmat[:1, 0] + kmat[:1, 1], kmat[1:, 0]], axis=0)
    lag_rows = jnp.transpose(lags, (1, 3, 0, 2)).reshape(S5_GROUPS, gsz, (2 * ch - 1) * gsz)
    t_sum = jnp.concatenate([lag_rows[:, :, (ch - 1 - s) * gsz:(2 * ch - 1 - s) * gsz] for s in range(ch)], axis=1)
    skip = jnp.eye(ch * gsz, dtype=F32)[None] * jnp.tile(d_skip.astype(F32).reshape(S5_GROUPS, 1, gsz), (1, ch, 1)).reshape(S5_GROUPS, 1, ch * gsz)
    t_sum = t_sum + skip

    def w_dir(d, pr, pi):
        wr = pr[..., None] * bb_re[d][None] - pi[..., None] * bb_im[d][None]
        wi = pr[..., None] * bb_im[d][None] + pi[..., None] * bb_re[d][None]
        w = jnp.concatenate([wr, wi], axis=2)
        return jnp.transpose(w, (1, 0, 3, 2)).reshape(S5_GROUPS, ch * gsz, 2 * S5_STATE)
    w_cat = jnp.concatenate([w_dir(0, jnp.flip(p_re[:ch, 0], 0), jnp.flip(p_im[:ch, 0], 0)),
                             w_dir(1, p_re[:ch, 1], p_im[:ch, 1])], axis=-1)

    def v_dir(vr, vi):
        v = jnp.concatenate([vr, vi], axis=-1)
        return jnp.transpose(v, (1, 3, 0, 2)).reshape(S5_GROUPS, 2 * S5_STATE, ch * gsz)
    v_cat = jnp.concatenate([v_dir(ca_re[1:, 0], -ca_im[1:, 0]),
                             v_dir(jnp.flip(ca_re[1:, 1], 0), -jnp.flip(ca_im[1:, 1], 0))], axis=1)

    a1 = jnp.concatenate([p_re[ch], p_re[ch]], axis=-1)
    a2 = jnp.concatenate([-p_im[ch], p_im[ch]], axis=-1)
    a12 = jnp.stack([a1, a2], axis=1)
    a12 = jnp.tile(a12, (1, 1, nb, 1))
    return t_sum.astype(BF16), w_cat.astype(BF16), v_cat.astype(BF16), a12


def _split_bf16(x):
    head = x.astype(BF16)
    return head, (x - head.astype(F32)).astype(BF16)


def _gla_direction(q_ref, k_ref, v_ref, a_ref, wa, ba, o_ref, st_ref, reverse):
    c = GLA_CHUNK
    hk = GLA_HEADS * GLA_DK
    hv = GLA_HEADS * GLA_DV
    rows = q_ref.shape[0]
    nchunks = rows // c
    grank2 = 2 * GLA_GATE_RANK

    a_head, a_rem = _split_bf16(a_ref[...])
    lane = lax.broadcasted_iota(jnp.int32, a_head.shape, 1)
    a_pack = jnp.where(jnp.logical_and(lane >= grank2, lane < 2 * grank2), a_rem, a_head)
    z = jnp.dot(a_pack, wa, preferred_element_type=F32) + ba
    g = (jnp.minimum(z, 0.0) - jnp.log(1.0 + jnp.exp(-jnp.abs(z)))) * (1.0 / GLA_TAU)
    ri = lax.broadcasted_iota(jnp.int32, (rows, rows), 0)
    ci = lax.broadcasted_iota(jnp.int32, (rows, rows), 1)
    ordered = (ri <= ci) if reverse else (ri >= ci)
    tri_bd = jnp.logical_and(ri // c == ci // c, ordered).astype(BF16)
    g_head, g_rem = _split_bf16(g)
    bb = jnp.dot(tri_bd, jnp.concatenate([g_head, g_rem], axis=1), preferred_element_type=F32)
    b = bb[:, :hk] + bb[:, hk:]
    q_t = (q_ref[...].astype(F32) * jnp.exp(b)).astype(BF16)
    k = k_ref[...].astype(F32)
    k_t = (k * jnp.exp(-b)).astype(BF16)

    hrow = lax.broadcasted_iota(jnp.int32, (GLA_HEADS, hk), 0)
    kmask = (lax.broadcasted_iota(jnp.int32, (GLA_HEADS, hk), 1) // GLA_DK == hrow).astype(BF16)
    vmask = (lax.broadcasted_iota(jnp.int32, (GLA_HEADS, hv), 1) // GLA_DV
             == lax.broadcasted_iota(jnp.int32, (GLA_HEADS, hv), 0)).astype(BF16)
    qi = lax.broadcasted_iota(jnp.int32, (c, GLA_HEADS * c), 0)
    kj = lax.broadcasted_iota(jnp.int32, (c, GLA_HEADS * c), 1) % c
    causal = ((qi <= kj) if reverse else (qi >= kj)).astype(F32)
    st_mask = (lax.broadcasted_iota(jnp.int32, (hv, hk), 0) // GLA_DV
               == lax.broadcasted_iota(jnp.int32, (hv, hk), 1) // GLA_DK).astype(F32)

    order = range(nchunks - 1, -1, -1) if reverse else range(nchunks)
    st = st_ref[...]
    for j in order:
        sl = slice(j * c, (j + 1) * c)
        b_c = b[sl]
        b_last = b_c[0:1] if reverse else b_c[c - 1:c]
        v_c = v_ref[sl, :]
        k_bd = jnp.concatenate([k_t[sl] * kmask[h:h + 1] for h in range(GLA_HEADS)], axis=0)
        v_bd = jnp.concatenate([v_c * vmask[h:h + 1] for h in range(GLA_HEADS)], axis=0)
        att = lax.dot_general(q_t[sl], k_bd, (((1,), (1,)), ((), ())), preferred_element_type=F32) * causal
        o = jnp.dot(att.astype(BF16), v_bd, preferred_element_type=F32)
        o += lax.dot_general(q_t[sl], st.astype(BF16), (((1,), (1,)), ((), ())), preferred_element_type=F32)
        o_ref[sl, :] = o.astype(o_ref.dtype)
        k_end = (k[sl] * jnp.exp(b_last - b_c)).astype(BF16)
        kv_t = lax.dot_general(v_c, k_end, (((0,), (0,)), ((), ())), preferred_element_type=F32)
        st = jnp.exp(b_last) * st + kv_t * st_mask
    st_ref[...] = st


def _gla_kernel(qf_ref, kf_ref, vf_ref, af_ref, qr_ref, kr_ref, vr_ref, ar_ref, wa_ref, ba_ref,
                of_ref, or_ref, sf_ref, sr_ref):
    @pl.when(pl.program_id(1) == 0)
    def _():
        sf_ref[...] = jnp.zeros_like(sf_ref)
        sr_ref[...] = jnp.zeros_like(sr_ref)

    hk = GLA_HEADS * GLA_DK
    _gla_direction(qf_ref, kf_ref, vf_ref, af_ref, wa_ref[:, :hk], ba_ref[:, :hk], of_ref, sf_ref, False)
    _gla_direction(qr_ref, kr_ref, vr_ref, ar_ref, wa_ref[:, hk:], ba_ref[:, hk:], or_ref, sr_ref, True)


def gla_scan(gq, gk, gv, ga, wa, ba, *, nb, seq, ctx_len):
    n = gq.shape[0]
    blk = SEQ_BLOCK
    assert ctx_len == blk
    lpb = seq // blk
    ctx0 = nb * lpb

    def fwd(b, s):
        return (jnp.where(s == 0, ctx0 + b, b * lpb + s - 1), 0)

    def rev(b, s):
        return (jnp.where(s == 0, ctx0 + b, b * lpb + lpb - s), 0)

    hk = GLA_HEADS * GLA_DK
    hv = GLA_HEADS * GLA_DV
    specs = []
    for imap in (fwd, rev):
        specs += [pl.BlockSpec((blk, hk), imap), pl.BlockSpec((blk, hk), imap),
                  pl.BlockSpec((blk, hv), imap), pl.BlockSpec((blk, LANES), imap)]
    specs += [_resident(wa.shape), _resident(ba.shape)]
    return pl.pallas_call(
        _gla_kernel,
        grid=(nb, lpb + 1),
        in_specs=specs,
        out_specs=[pl.BlockSpec((blk, hv), fwd), pl.BlockSpec((blk, hv), rev)],
        out_shape=[jax.ShapeDtypeStruct((n, hv), BF16)] * 2,
        scratch_shapes=[pltpu.VMEM((hv, hk), F32), pltpu.VMEM((hv, hk), F32)],
        compiler_params=_cparams(("arbitrary", "arbitrary")),
        name="gla_scan",
    )(gq, gk, gv, ga, gq, gk, gv, ga, wa, ba)


NA_ROWS_PER_STEP = 4
NA_UNION_ROWS = NA_WIN_ROWS + NA_ROWS_PER_STEP - 1


def _na_kernel(q_ref, k_ref, v_ref, kc_ref, vc_ref, bias_ref, o_ref, *, n_rows):
    step = pl.program_id(1)
    hd = NA_HEADS * NA_HEAD_DIM
    lane = lax.broadcasted_iota(jnp.int32, (NA_HEADS, hd), 1)
    hrow = lax.broadcasted_iota(jnp.int32, (NA_HEADS, hd), 0)
    head_mask = (lane // NA_HEAD_DIM == hrow).astype(F32)
    kc = kc_ref[...]
    vc = vc_ref[...]
    last = pl.num_programs(1) - 1
    regime = jnp.where(step == 0, 0, jnp.where(step == last, 2, 1))
    ws = jnp.clip(step * NA_ROWS_PER_STEP - NA_WIN_ROWS // 2, 0, n_rows - NA_UNION_ROWS)
    start = pl.multiple_of(ws * GRID_W, GRID_W)
    kw = k_ref[pl.ds(start, NA_UNION_ROWS * GRID_W), :]
    vw = v_ref[pl.ds(start, NA_UNION_ROWS * GRID_W), :]
    q = q_ref[...].astype(F32)
    acc = jnp.zeros(q.shape, F32)
    for h in range(NA_HEADS):
        m_h = head_mask[h:h + 1]
        qh = (q * m_h).astype(BF16)
        s_lat = lax.dot_general(qh, kw, (((1,), (1,)), ((), ())), preferred_element_type=F32)
        s_lat = s_lat + bias_ref[regime, h]
        s_ctx = lax.dot_general(qh, kc, (((1,), (1,)), ((), ())), preferred_element_type=F32)
        m = jnp.maximum(jnp.max(s_lat, axis=1, keepdims=True), jnp.max(s_ctx, axis=1, keepdims=True))
        p_lat = jnp.exp(s_lat - m)
        p_ctx = jnp.exp(s_ctx - m)
        den = jnp.sum(p_lat, axis=1, keepdims=True) + jnp.sum(p_ctx, axis=1, keepdims=True)
        o = jnp.dot(p_lat.astype(BF16), vw, preferred_element_type=F32)
        o += jnp.dot(p_ctx.astype(BF16), vc, preferred_element_type=F32)
        acc += (o / den) * m_h
    o_ref[...] = acc.astype(o_ref.dtype)


def _na_ctx_kernel(q_ref, k_ref, v_ref, o_ref):
    hd = NA_HEADS * NA_HEAD_DIM
    lane = lax.broadcasted_iota(jnp.int32, (NA_HEADS, hd), 1)
    hrow = lax.broadcasted_iota(jnp.int32, (NA_HEADS, hd), 0)
    head_mask = (lane // NA_HEAD_DIM == hrow).astype(F32)
    q = q_ref[...].astype(F32)
    k = k_ref[...]
    v = v_ref[...]
    acc = jnp.zeros(q.shape, F32)
    for h in range(NA_HEADS):
        m_h = head_mask[h:h + 1]
        s = lax.dot_general((q * m_h).astype(BF16), k, (((1,), (1,)), ((), ())), preferred_element_type=F32)
        p = jnp.exp(s - jnp.max(s, axis=1, keepdims=True))
        o = jnp.dot(p.astype(BF16), v, preferred_element_type=F32) / jnp.sum(p, axis=1, keepdims=True)
        acc += o * m_h
    o_ref[...] = acc.astype(o_ref.dtype)


def neighbourhood_attention(nq, nk, nv, bias, *, nb, seq, ctx_len):
    n, hd = nq.shape
    n_rows = seq // GRID_W
    qb = NA_ROWS_PER_STEP * GRID_W
    steps = seq // qb
    ctx0 = nb * seq // ctx_len
    o_lat = pl.pallas_call(
        functools.partial(_na_kernel, n_rows=n_rows),
        grid=(nb, steps),
        in_specs=[pl.BlockSpec((qb, hd), lambda b, s: (b * steps + s, 0)),
                  pl.BlockSpec((seq, hd), lambda b, s: (b, 0)),
                  pl.BlockSpec((seq, hd), lambda b, s: (b, 0)),
                  pl.BlockSpec((ctx_len, hd), lambda b, s: (ctx0 + b, 0)),
                  pl.BlockSpec((ctx_len, hd), lambda b, s: (ctx0 + b, 0)),
                  _resident(bias.shape)],
        out_specs=pl.BlockSpec((qb, hd), lambda b, s: (b * steps + s, 0)),
        out_shape=jax.ShapeDtypeStruct((nb * seq, hd), BF16),
        compiler_params=_cparams(("arbitrary", "arbitrary")),
        name="na_latent",
    )(nq, nk, nv, nk, nv, bias)
    o_ctx = pl.pallas_call(
        _na_ctx_kernel,
        grid=(nb,),
        in_specs=[pl.BlockSpec((ctx_len, hd), lambda b: (ctx0 + b, 0))] * 3,
        out_specs=pl.BlockSpec((ctx_len, hd), lambda b: (b, 0)),
        out_shape=jax.ShapeDtypeStruct((nb * ctx_len, hd), BF16),
        compiler_params=_cparams(("arbitrary",)),
        name="na_context",
    )(nq, nk, nv)
    return jnp.concatenate([o_lat, o_ctx], axis=0)


def na_bias_tables(rpb):
    rpb = rpb.astype(F32)
    wr, nq, nu = NA_WIN_ROWS, NA_ROWS_PER_STEP, NA_UNION_ROWS
    assert nq <= wr // 2 + 1 and wr >= nq + wr // 2
    c_idx = jnp.arange(GRID_W)
    col_start = jnp.clip(c_idx - NA_WIN_COLS // 2, 0, GRID_W - NA_WIN_COLS)
    col_in = (c_idx[None, :] >= col_start[:, None]) & (c_idx[None, :] < col_start[:, None] + NA_WIN_COLS)
    wc = NA_WIN_COLS
    pad = jnp.zeros(rpb.shape[:-1] + (2 * GRID_W - (2 * wc - 1),), F32)
    table = jnp.concatenate([rpb[..., wc - 1:], pad, rpb[..., :wc - 1]], axis=-1)
    toe = jnp.tile(table, GRID_W)[..., :GRID_W * (2 * GRID_W - 1)]
    toe = toe.reshape(table.shape[:-1] + (GRID_W, 2 * GRID_W - 1))[..., :GRID_W]
    toe = jnp.transpose(jnp.where(col_in[None, None], toe, NEG_BIG), (0, 2, 1, 3))
    per_regime = []
    for lo_of, dr0_of in ((lambda i: 0, lambda i: wr - 1 - i),
                          (lambda i: i, lambda i: wr // 2 - 1),
                          (lambda i: nu - wr, lambda i: nu - wr - i)):
        rows = []
        for i in range(nq):
            lo, dr0 = lo_of(i), dr0_of(i)
            piece = toe[:, :, dr0:dr0 + wr]
            rows.append(jnp.pad(piece, ((0, 0), (0, 0), (lo, nu - wr - lo), (0, 0)), constant_values=NEG_BIG))
        per_regime.append(jnp.stack(rows, axis=1))
    b = jnp.stack(per_regime, axis=0)
    return b.reshape(3, NA_HEADS, nq * GRID_W, nu * GRID_W)


def _conv_kernel(prev_ref, main_ref, next_ref, dw_ref, dwb_ref, lng_ref, lnb_ref, o_ref, buf_ref, shift_ref,
                 *, n_lat_tiles, tiles_per_batch):
    i = pl.program_id(0)
    j = i % tiles_per_batch
    is_lat = i < n_lat_tiles
    has_prev = jnp.logical_and(is_lat, j > 0)
    has_next = jnp.logical_and(is_lat, j < tiles_per_batch - 1)
    cw = CONV_WIDTH
    tl = main_ref.shape[0]

    def glu(a):
        a = a.astype(F32)
        return a[:, :cw] * _sigmoid(a[:, cw:])

    buf_ref[0:CONV_HALO, :] = glu(prev_ref[...]) * has_prev.astype(F32)
    buf_ref[CONV_HALO:CONV_HALO + tl, :] = glu(main_ref[...])
    buf_ref[CONV_HALO + tl:, :] = glu(next_ref[...]) * has_next.astype(F32)
    dw = dw_ref[...]
    acc = jnp.zeros((tl, cw), F32) + dwb_ref[...]
    base = CONV_HALO - CONV_KERNEL // 2
    span = tl + 2 * CONV_HALO - SUBLANES
    for r in range(SUBLANES):
        shift_ref[r] = buf_ref[r:r + span, :]
    for k in range(CONV_KERNEL):
        q, r = divmod(base + k, SUBLANES)
        acc += shift_ref[r, q * SUBLANES:q * SUBLANES + tl, :] * dw[k:k + 1, :]
    mu = jnp.mean(acc, axis=-1, keepdims=True)
    xc = acc - mu
    y = xc * lax.rsqrt(jnp.mean(xc * xc, axis=-1, keepdims=True) + NORM_EPS)
    y = y * lng_ref[...] + lnb_ref[...]
    o_ref[...] = _silu(y).astype(o_ref.dtype)


def conv_branch(cv, dw, dw_b, ln_g, ln_b, *, nb, seq):
    n = cv.shape[0]
    tl = SEQ_BLOCK
    hb = tl // CONV_HALO
    n_tiles = n // tl
    cw = CONV_WIDTH
    return pl.pallas_call(
        functools.partial(_conv_kernel, n_lat_tiles=nb * seq // tl, tiles_per_batch=seq // tl),
        grid=(n_tiles,),
        in_specs=[pl.BlockSpec((CONV_HALO, 2 * cw), lambda i: (jnp.maximum(i * hb - 1, 0), 0)),
                  pl.BlockSpec((tl, 2 * cw), lambda i: (i, 0)),
                  pl.BlockSpec((CONV_HALO, 2 * cw), lambda i: (jnp.minimum((i + 1) * hb, n_tiles * hb - 1), 0)),
                  _resident((CONV_KERNEL, cw)), _resident((1, cw)), _resident((1, cw)), _resident((1, cw))],
        out_specs=pl.BlockSpec((tl, cw), lambda i: (i, 0)),
        out_shape=jax.ShapeDtypeStruct((n, cw), BF16),
        scratch_shapes=[pltpu.VMEM((tl + 2 * CONV_HALO, cw), F32),
                        pltpu.VMEM((SUBLANES, tl + 2 * CONV_HALO - SUBLANES, cw), F32)],
        compiler_params=_cparams(("arbitrary",)),
        name="conv_branch",
    )(cv, cv, cv, dw.astype(F32), dw_b.reshape(1, cw).astype(F32), ln_g.reshape(1, cw).astype(F32),
      ln_b.reshape(1, cw).astype(F32))


def _merge_kernel(xa_ref, xb_ref, mod_ref, gt_ref, ys5a_ref, ys5b_ref, of_ref, or_ref, gr_ref, na_ref, cv_ref,
                  gate_b_ref, wglu_ref, bglu_ref, ws5_ref, gng_ref, wgla_ref, wna_ref, wcv_ref, wmix_ref, o_ref,
                  *, n_lat_tiles):
    d = xa_ref.shape[1]
    z = jax.nn.gelu(jnp.concatenate([ys5a_ref[...], ys5b_ref[...]], axis=1))
    z = z * _sigmoid(jnp.dot(z.astype(BF16), wglu_ref[...], preferred_element_type=F32) + bglu_ref[...])
    br_s5 = jnp.dot(z.astype(BF16), ws5_ref[...], preferred_element_type=F32)
    o = of_ref[...].astype(F32) + or_ref[...].astype(F32)
    r = _silu(gr_ref[...].astype(F32))
    parts = []
    for h in range(GLA_HEADS):
        oh = o[:, h * GLA_DV:(h + 1) * GLA_DV]
        oh = oh * lax.rsqrt(jnp.mean(oh * oh, axis=-1, keepdims=True) + NORM_EPS) * gng_ref[...]
        parts.append(oh * r[:, h * GLA_DV:(h + 1) * GLA_DV])
    y_gla = jnp.concatenate(parts, axis=1).astype(BF16)
    br_gla = jnp.dot(y_gla, wgla_ref[...], preferred_element_type=F32)
    br_na = jnp.dot(na_ref[...], wna_ref[...], preferred_element_type=F32)
    br_cv = jnp.dot(cv_ref[...], wcv_ref[...], preferred_element_type=F32)
    merged = jnp.zeros((xa_ref.shape[0], d), F32)
    for i, br in enumerate((br_s5, br_gla, br_na, br_cv)):
        pre = gt_ref[:, i * d:(i + 1) * d] + gate_b_ref[:, i * d:(i + 1) * d].astype(BF16)
        gate = 0.5 * jnp.tanh(0.5 * pre).astype(F32) + 0.5
        merged += gate * br
    mix = jnp.dot(merged.astype(BF16), wmix_ref[...], preferred_element_type=F32)
    o_ref[...] = _x_pair_tile(xa_ref, xb_ref, n_lat_tiles) + mod_ref[:, 2 * d:3 * d] * mix


def merge_branches(x_lat, x_ctx, ctx_tile0, modtab, gt, ys5a, ys5b, o_f, o_r, gr, o_na, y_cv, weights,
                   *, n_rows, nb, seq, tm):
    d = x_lat.shape[1]
    n_lat_tiles = nb * seq // tm
    tpb = seq // tm

    def row(i):
        return (i, 0)

    acts = (gt, ys5a, ys5b, o_f, o_r, gr, o_na, y_cv)
    return pl.pallas_call(
        functools.partial(_merge_kernel, n_lat_tiles=n_lat_tiles),
        grid=(n_rows // tm,),
        in_specs=_x_pair_specs(tm, d, n_lat_tiles, ctx_tile0)
        + [_mod_row_spec(n_lat_tiles, tpb, nb, modtab.shape[-1])]
        + [pl.BlockSpec((tm, a.shape[1]), row) for a in acts]
        + [_resident(w.shape) for w in weights],
        out_specs=pl.BlockSpec((tm, d), row),
        out_shape=jax.ShapeDtypeStruct((n_rows, d), F32),
        compiler_params=_cparams(("arbitrary",)),
        name="merge_mix",
    )(x_lat, x_ctx, modtab, *acts, *weights)


MOE_BLOCK = 128


def _route_kernel(x_ref, mod_ref, g_ref, wr_ref, br_ref, ltri_ref, h_ref, comb_ref, meta_ref, cnt_ref):
    d = x_ref.shape[1]
    mod = mod_ref[...]
    h = _modulated_norm(x_ref[...], g_ref[...], mod[:, 3 * d:4 * d], mod[:, 4 * d:5 * d])
    h_ref[...] = h.astype(BF16)
    h_head, h_rem = _split_bf16(h)
    nl = br_ref.shape[1]
    both = jnp.dot(h_head, wr_ref[...], preferred_element_type=F32)
    logits = (both[:, :nl] + both[:, nl:] + jnp.dot(h_rem, wr_ref[:, :nl], preferred_element_type=F32)
              + br_ref[...])
    lane = lax.broadcasted_iota(jnp.int32, logits.shape, 1)
    big = jnp.int32(1 << 20)
    is_g = lane < MOE_GROUPS
    gl = jnp.where(is_g, logits, -jnp.inf)
    gmax = jnp.max(gl, axis=1, keepdims=True)
    gidx = jnp.min(jnp.where(gl == gmax, lane, big), axis=1, keepdims=True)
    group_p = 1.0 / jnp.sum(jnp.where(is_g, jnp.exp(logits - gmax), 0.0), axis=1, keepdims=True)
    first = MOE_GROUPS + gidx * MOE_EXPERTS_PER_GROUP
    in_group = jnp.logical_and(lane >= first, lane < first + MOE_EXPERTS_PER_GROUP)
    el = jnp.where(in_group, logits, -jnp.inf)
    v1 = jnp.max(el, axis=1, keepdims=True)
    i1 = jnp.min(jnp.where(el == v1, lane, big), axis=1, keepdims=True)
    el2 = jnp.where(lane == i1, -jnp.inf, el)
    v2 = jnp.max(el2, axis=1, keepdims=True)
    i2 = jnp.min(jnp.where(el2 == v2, lane, big), axis=1, keepdims=True)
    t = jnp.exp(v2 - v1)
    w1 = group_p / (1.0 + t)
    w2 = group_p * t / (1.0 + t)
    k1 = i1 - first
    k2 = i2 - first
    epg = MOE_EXPERTS_PER_GROUP
    comb = (jnp.where(jnp.logical_or(lane == k1, lane == k1 + epg), w1, 0.0)
            + jnp.where(jnp.logical_or(lane == k2, lane == k2 + epg), w2, 0.0))
    head = comb.astype(BF16).astype(F32)
    comb_ref[...] = jnp.where(lane < epg, head, comb - head).astype(BF16)
    onehot = (lane == gidx).astype(BF16)
    rank_all = jnp.dot(ltri_ref[...], onehot, preferred_element_type=F32)
    rank = jnp.sum(jnp.where(lane == gidx, rank_all, 0.0), axis=1, keepdims=True)
    packed = jnp.where(lane == 0, gidx.astype(F32), jnp.where(lane == 1, rank, 0.0))
    meta_ref[0] = jnp.transpose(packed)[0:8, :]
    counts = jnp.sum(onehot.astype(F32), axis=0, keepdims=True)
    cnt_ref[0] = jnp.broadcast_to(counts, cnt_ref.shape[1:]).astype(jnp.int32)


def _moe_kernel(cnt_ref, x_ref, mod_ref, h_ref, comb_ref, meta_ref, w1_ref, w3_ref, w2_ref, fg_ref, o_ref,
                *, final):
    t = pl.program_id(0)
    g = pl.program_id(1)
    tm, d = x_ref.shape

    @pl.when(g == 0)
    def _():
        o_ref[...] = jnp.zeros_like(o_ref)

    n_tok = cnt_ref[t * MOE_GROUPS + g]
    gid_row = meta_ref[0, 0:1, :]
    rank_row = meta_ref[0, 1:2, :]
    in_grp = gid_row == g.astype(F32)
    row = lax.broadcasted_iota(jnp.int32, (MOE_BLOCK, tm), 0).astype(F32)

    def block(blk, carry):
        sel = jnp.logical_and(in_grp, rank_row == row + (blk * MOE_BLOCK).astype(F32))
        p = sel.astype(BF16)
        xg = jnp.dot(p, h_ref[...], preferred_element_type=F32).astype(BF16)
        cw = jnp.dot(p, comb_ref[...], preferred_element_type=F32)
        yg = jnp.zeros((MOE_BLOCK, d), F32)
        for e in range(MOE_EXPERTS_PER_GROUP):
            a = jnp.dot(xg, w1_ref[e], preferred_element_type=F32)
            b = jnp.dot(xg, w3_ref[e], preferred_element_type=F32)
            cw_e = cw[:, e:e + 1] + cw[:, MOE_EXPERTS_PER_GROUP + e:MOE_EXPERTS_PER_GROUP + e + 1]
            act = (_silu(a) * b * cw_e).astype(BF16)
            yg += jnp.dot(act, w2_ref[e], preferred_element_type=F32)
        o_ref[...] += lax.dot_general(p, yg.astype(BF16), (((0,), (0,)), ((), ())), preferred_element_type=F32)
        return carry

    lax.fori_loop(0, (n_tok + MOE_BLOCK - 1) // MOE_BLOCK, block, 0)

    @pl.when(g == pl.num_programs(1) - 1)
    def _():
        y = x_ref[...] + mod_ref[:, 5 * d:6 * d] * o_ref[...]
        if final:
            y = y * lax.rsqrt(jnp.mean(y * y, axis=-1, keepdims=True) + NORM_EPS) * fg_ref[...]
        o_ref[...] = y


def moe_layer(xs, modtab, norm_g, w_router, b_router, w1, w3, w2, final_g, *, layer, n_rows, nb, seq, tm, final):
    d = xs.shape[1]
    n_lat_tiles = nb * seq // tm
    tpb = seq // tm
    n_tiles = n_rows // tm
    ltri = (lax.broadcasted_iota(jnp.int32, (tm, tm), 0) > lax.broadcasted_iota(jnp.int32, (tm, tm), 1)).astype(BF16)
    mod_spec = _mod_row_spec(n_lat_tiles, tpb, nb, modtab.shape[-1])
    h, comb, meta, cnt = pl.pallas_call(
        _route_kernel,
        grid=(n_tiles,),
        in_specs=[pl.BlockSpec((tm, d), lambda i: (i, 0)), mod_spec,
                  _resident((1, d)), _resident(w_router.shape), _resident(b_router.shape), _resident(ltri.shape)],
        out_specs=[pl.BlockSpec((tm, d), lambda i: (i, 0)), pl.BlockSpec((tm, LANES), lambda i: (i, 0)),
                   pl.BlockSpec((1, 8, tm), lambda i: (i, 0, 0)), pl.BlockSpec((1, 8, LANES), lambda i: (i, 0, 0))],
        out_shape=[jax.ShapeDtypeStruct((n_rows, d), BF16), jax.ShapeDtypeStruct((n_rows, LANES), BF16),
                   jax.ShapeDtypeStruct((n_tiles, 8, tm), F32), jax.ShapeDtypeStruct((n_tiles, 8, LANES), jnp.int32)],
        compiler_params=_cparams(("arbitrary",)),
        name="moe_route",
    )(xs, modtab, norm_g.reshape(1, d), w_router, b_router, ltri)
    counts = cnt[:, 0, :MOE_GROUPS].reshape(n_tiles * MOE_GROUPS)
    epg, hid = w1.shape[2], w1.shape[-1]
    grid_spec = pltpu.PrefetchScalarGridSpec(
        num_scalar_prefetch=1,
        grid=(n_tiles, MOE_GROUPS),
        in_specs=[pl.BlockSpec((tm, d), lambda i, g, c: (i, 0)),
                  pl.BlockSpec((None, 1, modtab.shape[-1]),
                               lambda i, g, c: (jnp.where(i < n_lat_tiles, i // tpb, nb), 0, 0)),
                  pl.BlockSpec((tm, d), lambda i, g, c: (i, 0)),
                  pl.BlockSpec((tm, LANES), lambda i, g, c: (i, 0)),
                  pl.BlockSpec((1, 8, tm), lambda i, g, c: (i, 0, 0)),
                  pl.BlockSpec((None, None, epg, d, hid), lambda i, g, c: (layer, g, 0, 0, 0)),
                  pl.BlockSpec((None, None, epg, d, hid), lambda i, g, c: (layer, g, 0, 0, 0)),
                  pl.BlockSpec((None, None, epg, hid, d), lambda i, g, c: (layer, g, 0, 0, 0)),
                  pl.BlockSpec((1, d), lambda i, g, c: (0, 0))],
        out_specs=pl.BlockSpec((tm, d), lambda i, g, c: (i, 0)),
    )
    return pl.pallas_call(
        functools.partial(_moe_kernel, final=final),
        grid_spec=grid_spec,
        out_shape=jax.ShapeDtypeStruct((n_rows, d), F32),
        compiler_params=_cparams(("arbitrary", "arbitrary")),
        name="moe_experts",
    )(counts, xs, modtab, h, comb, meta, w1, w3, w2, final_g.reshape(1, d).astype(F32))


def rope_tables(seq, pad_rows):
    t = jnp.arange(seq, dtype=jnp.int32)
    row = (t // GRID_W).astype(F32)
    colp = (t % GRID_W).astype(F32)
    half = GLA_DK // 2
    inv_freq = ROPE_BASE ** (-jnp.arange(0, half, 2, dtype=F32) / half)
    ang_r = row[:, None] * inv_freq
    ang_c = colp[:, None] * inv_freq
    dd = jnp.arange(GLA_DK)
    ang = jnp.where((dd < half)[None, :], ang_r[:, dd % (half // 2)], ang_c[:, dd % (half // 2)])
    sign = jnp.where((dd % half) < half // 2, -1.0, 1.0).astype(F32)
    cos = jnp.tile(jnp.cos(ang), (1, GLA_HEADS))
    sin = jnp.tile(jnp.sin(ang) * sign[None, :], (1, GLA_HEADS))
    cos = jnp.concatenate([cos, jnp.ones((pad_rows, cos.shape[1]), F32)], axis=0)
    sin = jnp.concatenate([sin, jnp.zeros((pad_rows, sin.shape[1]), F32)], axis=0)
    return cos, sin


def split_in_weights(w_in, d):
    widths = (S5_WIDTH, GLA_HEADS * GLA_DK, GLA_HEADS * GLA_DK, GLA_HEADS * GLA_DV, GLA_HEADS * GLA_DV,
              2 * GLA_GATE_RANK, NA_HEADS * NA_HEAD_DIM, NA_HEADS * NA_HEAD_DIM, NA_HEADS * NA_HEAD_DIM,
              2 * CONV_WIDTH, N_BRANCHES * d)
    names = ('u', 'gq', 'gk', 'gv', 'gr', 'ga', 'nq', 'nk', 'nv', 'cv', 'gt')
    parts = {}
    col = 0
    for nme, w in zip(names, widths):
        parts[nme] = w_in[:, col:col + w]
        col += w
    swap = jnp.arange(GLA_HEADS * GLA_DK) ^ (GLA_DK // 4)
    gq = parts['gq'] * (GLA_DK ** -0.5)
    wqk = jnp.concatenate([gq, gq[:, swap], parts['gk'], parts['gk'][:, swap]], axis=1)
    ga = jnp.concatenate([parts['ga']] * 3 + [jnp.zeros((d, LANES - 6 * GLA_GATE_RANK), w_in.dtype)], axis=1)
    wmisc = jnp.concatenate([parts['u'], parts['nq'] * (NA_HEAD_DIM ** -0.5), parts['nk'], parts['nv'],
                             parts['gv'], parts['gr'], parts['cv'], ga], axis=1)
    return wqk.astype(BF16), wmisc.astype(BF16), parts['gt'].astype(BF16)


def kernel(x, c, ctx, c_ctx, norm1_g, norm2_g, w_mod, b_mod, w_in, gate_b, w_mix_out, s5_lam_re, s5_lam_im, s5_log_dt, s5_b_re, s5_b_im, s5_c_re, s5_c_im, s5_d, s5_w_glu, s5_b_glu, s5_w_out, gla_w_a2, gla_b_a, gla_norm_g, gla_w_out, na_rpb, na_w_out, conv_dw, conv_dw_b, conv_ln_g, conv_ln_b, conv_w_out, moe_w_group, moe_b_group, moe_w_expert, moe_b_expert, moe_w1, moe_w3, moe_w2, final_norm_g):
    nb, seq, d = x.shape
    ctx_len = ctx.shape[1]
    depth = w_mod.shape[0]
    n_lat = nb * seq
    tm = 512
    tm_moe = math.gcd(1024, nb * ctx_len)
    assert ctx_len == SEQ_BLOCK and seq % tm_moe == 0 and (nb * ctx_len) % tm_moe == 0 and nb < MOD_ROWS

    n_all = n_lat + nb * ctx_len
    x_lat, x_ctx, ctx_tile0 = x.reshape(n_lat, d).astype(F32), ctx.reshape(nb * ctx_len, d).astype(F32), 0
    c_rows = jnp.zeros((MOD_ROWS, d), F32).at[:nb].set(c.astype(F32)).at[nb].set(c_ctx.astype(F32))
    modtab = modulation_table(c_rows, w_mod.astype(F32), b_mod.astype(F32))
    modtab = modtab.reshape(depth, MOD_ROWS, 1, 6 * d)
    cos_tab, sin_tab = rope_tables(seq, tm)

    moe_w1_bf, moe_w3_bf, moe_w2_bf = moe_w1.astype(BF16), moe_w3.astype(BF16), moe_w2.astype(BF16)

    for i in range(depth):
        last = i == depth - 1
        n_rows = n_lat if last else n_all
        wqk, wmisc, wgate = split_in_weights(w_in[i], d)
        gq, gk, ua, ub, nq, nk, nv, gv, gr, cv, ga, gt = in_projection(
            x_lat, x_ctx, ctx_tile0, modtab[i], norm1_g[i].astype(F32), cos_tab, sin_tab, wqk, wmisc, wgate,
            n=n_all, nb=nb, seq=seq, tm=tm)

        mats = s5_matrices(s5_lam_re[i], s5_lam_im[i], s5_log_dt[i], s5_b_re[i], s5_b_im[i],
                           s5_c_re[i], s5_c_im[i], s5_d[i], nb)
        ys5a, ys5b = s5_mixer(ua, ub, mats, nb=nb, seq=seq, ctx_len=ctx_len)

        hk = GLA_HEADS * GLA_DK
        zero = jnp.zeros((GLA_GATE_RANK, hk), F32)
        wd = jnp.concatenate([jnp.concatenate([gla_w_a2[i, 0].astype(F32), zero], axis=1),
                              jnp.concatenate([zero, gla_w_a2[i, 1].astype(F32)], axis=1)], axis=0)
        wd_head, wd_rem = _split_bf16(wd)
        wa = jnp.concatenate([wd_head, wd_head, wd_rem,
                              jnp.zeros((LANES - 6 * GLA_GATE_RANK, 2 * hk), BF16)], axis=0)
        ba = gla_b_a[i].astype(F32).reshape(1, 2 * hk)
        o_f, o_r = gla_scan(gq, gk, gv, ga, wa, ba, nb=nb, seq=seq, ctx_len=ctx_len)

        o_na = neighbourhood_attention(nq, nk, nv, na_bias_tables(na_rpb[i]), nb=nb, seq=seq, ctx_len=ctx_len)
        y_cv = conv_branch(cv, conv_dw[i], conv_dw_b[i], conv_ln_g[i], conv_ln_b[i], nb=nb, seq=seq)

        weights = (gate_b[i].astype(F32).reshape(1, N_BRANCHES * d), s5_w_glu[i].astype(BF16),
                   s5_b_glu[i].astype(F32).reshape(1, S5_WIDTH), s5_w_out[i].astype(BF16),
                   gla_norm_g[i].astype(F32).reshape(1, GLA_DV), gla_w_out[i].astype(BF16),
                   na_w_out[i].astype(BF16), conv_w_out[i].astype(BF16), w_mix_out[i].astype(BF16))
        xs = merge_branches(x_lat, x_ctx, ctx_tile0, modtab[i], gt, ys5a, ys5b, o_f, o_r, gr, o_na, y_cv, weights,
                            n_rows=n_rows, nb=nb, seq=seq, tm=tm)

        n_router = MOE_GROUPS + N_EXPERTS
        w_router = jnp.pad(jnp.concatenate([moe_w_group[i], moe_w_expert[i]], axis=1).astype(F32),
                           ((0, 0), (0, LANES - n_router)))
        w_router = jnp.concatenate(_split_bf16(w_router), axis=1)
        b_router = jnp.pad(jnp.concatenate([moe_b_group[i], moe_b_expert[i]]).astype(F32),
                           (0, LANES - n_router)).reshape(1, LANES)
        xs = moe_layer(xs, modtab[i], norm2_g[i].astype(F32), w_router, b_router,
                       moe_w1_bf, moe_w3_bf, moe_w2_bf, final_norm_g, layer=i,
                       n_rows=n_rows, nb=nb, seq=seq, tm=tm_moe, final=last)
        x_lat, x_ctx, ctx_tile0 = xs, xs, n_lat // tm

    return xs.reshape(nb, seq, d).astype(x.dtype)
```

```python
import functools
import math

import jax
import jax.numpy as jnp
from jax import lax
from jax.experimental import pallas as pl
from jax.experimental.pallas import tpu as pltpu

F32 = jnp.float32
BF16 = jnp.bfloat16
HIGHEST = lax.Precision.HIGHEST

GRID_W = 64
NORM_EPS = 1e-6
N_BRANCHES = 4
S5_WIDTH = 256
S5_GROUP_SIZE = 16
S5_GROUPS = 16
S5_STATE = 64
GLA_HEADS = 4
GLA_DK = 64
GLA_DV = 128
GLA_GATE_RANK = 16
GLA_TAU = 16.0
GLA_CHUNK = 64
ROPE_BASE = 10000.0
NA_HEADS = 4
NA_HEAD_DIM = 64
NA_WIN_ROWS = 8
NA_WIN_COLS = 16
CONV_WIDTH = 256
CONV_KERNEL = 31
MOE_GROUPS = 4
MOE_EXPERTS_PER_GROUP = 8
MOE_HIDDEN = 256
N_EXPERTS = MOE_GROUPS * MOE_EXPERTS_PER_GROUP

LANES = 128
SUBLANES = 8
MOD_ROWS = 8
VMEM_LIMIT = 56 * 1024 * 1024
S5_CHUNK = 32
SEQ_BLOCK = 256
CONV_HALO = 16
NEG_BIG = -1e30


def _cparams(sem):
    return pltpu.CompilerParams(dimension_semantics=sem, vmem_limit_bytes=VMEM_LIMIT)


def _resident(shape):
    nd = len(shape)
    return pl.BlockSpec(shape, lambda *_: (0,) * nd, pipeline_mode=pl.Buffered(1))


def _sigmoid(x):
    return 0.5 * jnp.tanh(0.5 * x) + 0.5


def _silu(x):
    return x * _sigmoid(x)


def _mod_kernel(c_ref, w_ref, b_ref, o_ref):
    c = c_ref[...]
    o_ref[0] = jnp.dot(_silu(c), w_ref[0], preferred_element_type=F32, precision=HIGHEST) + b_ref[0]


def modulation_table(c_rows, w_mod, b_mod):
    depth, d, n6 = w_mod.shape
    tn = 1024
    return pl.pallas_call(
        _mod_kernel,
        grid=(depth, n6 // tn),
        in_specs=[
            pl.BlockSpec((MOD_ROWS, d), lambda l, j: (0, 0)),
            pl.BlockSpec((1, d, tn), lambda l, j: (l, 0, j)),
            pl.BlockSpec((1, 1, tn), lambda l, j: (l, 0, j)),
        ],
        out_specs=pl.BlockSpec((1, MOD_ROWS, tn), lambda l, j: (l, 0, j)),
        out_shape=jax.ShapeDtypeStruct((depth, MOD_ROWS, n6), F32),
        compiler_params=_cparams(("arbitrary", "arbitrary")),
        name="mod_table",
    )(c_rows, w_mod, b_mod.reshape(depth, 1, n6))


def _mod_row_spec(n_lat_tiles, tiles_per_batch, nb, width):
    def imap(i, *_):
        return (jnp.where(i < n_lat_tiles, i // tiles_per_batch, nb), 0, 0)
    return pl.BlockSpec((None, 1, width), imap)


def _modulated_norm(x, g, shift, scale):
    y = x * lax.rsqrt(jnp.mean(x * x, axis=-1, keepdims=True) + NORM_EPS)
    return (y * g) * (1.0 + scale) + shift


IN_CHUNK = 512


def _x_pair_specs(tm, d, n_lat_tiles, ctx_tile0):
    return [pl.BlockSpec((tm, d), lambda i: (jnp.minimum(i, n_lat_tiles - 1), 0)),
            pl.BlockSpec((tm, d), lambda i: (ctx_tile0 + jnp.maximum(i - n_lat_tiles, 0), 0))]


def _x_pair_tile(xa_ref, xb_ref, n_lat_tiles):
    return jnp.where(pl.program_id(0) < n_lat_tiles, xa_ref[...], xb_ref[...])


def _inproj_kernel(xa_ref, xb_ref, mod_ref, g_ref, cos_ref, sin_ref, wqk_ref, wmisc_ref, wgate_ref,
                   gq_ref, gk_ref, ua_ref, ub_ref, nq_ref, nk_ref, nv_ref, gv_ref, gr_ref, cv_ref, ga_ref, gt_ref,
                   *, n_lat_tiles):
    d = xa_ref.shape[1]
    mod = mod_ref[...]
    x = _x_pair_tile(xa_ref, xb_ref, n_lat_tiles)
    h = _modulated_norm(x, g_ref[...], mod[:, 0:d], mod[:, d:2 * d]).astype(BF16)
    cos = cos_ref[...]
    sin = sin_ref[...]
    for j, o_ref in enumerate((gq_ref, gk_ref)):
        y = jnp.dot(h, wqk_ref[:, j * 512:(j + 1) * 512], preferred_element_type=F32)
        o_ref[...] = (y[:, :256] * cos + y[:, 256:] * sin).astype(o_ref.dtype)
    col = 0
    for o_ref in (ua_ref, ub_ref, nq_ref, nk_ref, nv_ref, gv_ref, gr_ref, cv_ref, ga_ref):
        w = o_ref.shape[1]
        o_ref[...] = jnp.dot(h, wmisc_ref[:, col:col + w], preferred_element_type=F32).astype(o_ref.dtype)
        col += w
    for j in range(gt_ref.shape[1] // IN_CHUNK):
        sl = slice(j * IN_CHUNK, (j + 1) * IN_CHUNK)
        gt_ref[:, sl] = jnp.dot(h, wgate_ref[:, sl], preferred_element_type=F32).astype(gt_ref.dtype)


def in_projection(x_lat, x_ctx, ctx_tile0, modtab, norm_g, cos_tab, sin_tab, wqk, wmisc, wgate, *, n, nb, seq, tm):
    d = x_lat.shape[1]
    n_lat_tiles = nb * seq // tm
    tpb = seq // tm
    widths = (256, 256, LANES, LANES, 256, 256, 256, 512, 512, 512, LANES, N_BRANCHES * d)
    dtypes = (BF16, BF16, F32, F32) + (BF16,) * 6 + (F32, BF16)

    def row(i):
        return (i, 0)

    def rope_row(i):
        return (jnp.where(i < n_lat_tiles, i % tpb, tpb), 0)

    return pl.pallas_call(
        functools.partial(_inproj_kernel, n_lat_tiles=n_lat_tiles),
        grid=(n // tm,),
        in_specs=_x_pair_specs(tm, d, n_lat_tiles, ctx_tile0) + [
            _mod_row_spec(n_lat_tiles, tpb, nb, modtab.shape[-1]),
            _resident((1, d)),
            pl.BlockSpec((tm, 256), rope_row),
            pl.BlockSpec((tm, 256), rope_row),
            _resident(wqk.shape),
            _resident(wmisc.shape),
            _resident(wgate.shape),
        ],
        out_specs=[pl.BlockSpec((tm, w), row) for w in widths],
        out_shape=[jax.ShapeDtypeStruct((n, w), dt) for w, dt in zip(widths, dtypes)],
        compiler_params=_cparams(("arbitrary",)),
        name="in_proj",
    )(x_lat, x_ctx, modtab, norm_g.reshape(1, d), cos_tab, sin_tab, wqk, wmisc, wgate)


S5_TAUS_PER_TILE = LANES // S5_GROUP_SIZE
S5_PERM = S5_TAUS_PER_TILE * S5_WIDTH


def _s5_state_kernel(uca_ref, ucb_ref, ula_ref, ulb_ref, perm_ref, w_ref, x_ref, s_ref):
    n_ctx, n_lat = uca_ref.shape[0] // S5_CHUNK, ula_ref.shape[0] // S5_CHUNK
    for v in range(S5_CHUNK // S5_TAUS_PER_TILE):
        pieces = []
        for w in range(S5_TAUS_PER_TILE):
            tau = v * S5_TAUS_PER_TILE + w
            for uc_ref, ul_ref in ((uca_ref, ula_ref), (ucb_ref, ulb_ref)):
                pieces.append(jnp.concatenate([uc_ref[pl.ds(tau, n_ctx, stride=S5_CHUNK), :],
                                               ul_ref[pl.ds(tau, n_lat, stride=S5_CHUNK), :]], axis=0))
        z = jnp.concatenate(pieces, axis=1).astype(BF16)
        xv = jnp.dot(z, perm_ref[...], preferred_element_type=F32).astype(BF16)
        for g in range(S5_GROUPS):
            col = g * S5_CHUNK * S5_GROUP_SIZE + v * LANES
            x_ref[0, :, col:col + LANES] = xv[:, g * LANES:(g + 1) * LANES]
    cw = S5_CHUNK * S5_GROUP_SIZE
    for g in range(S5_GROUPS):
        s = jnp.dot(x_ref[0, :, g * cw:(g + 1) * cw], w_ref[g], preferred_element_type=F32)
        s_ref[0, 0, g] = s[:, :2 * S5_STATE]
        s_ref[1, 0, g] = s[:, 2 * S5_STATE:]


def _s5_scan_kernel(s_ref, a_ref, h_ref, *, n_chunks, n_ctx_chunks):
    d = pl.program_id(0)
    nc = n_chunks
    n_pairs = s_ref.shape[1] // nc
    a1 = a_ref[0, 0]
    a2 = a_ref[0, 1]

    def step(s, h):
        fwd_row = s
        rev_row = jnp.where(s < n_ctx_chunks, n_ctx_chunks - 1 - s, nc - 1 - (s - n_ctx_chunks))
        rows = pl.ds(jnp.where(d == 0, fwd_row, rev_row), n_pairs, stride=nc)
        h_ref[0, rows, :] = h
        return h * a1 + pltpu.roll(h, S5_STATE, axis=1) * a2 + s_ref[0, rows, :]

    lax.fori_loop(0, nc, step, jnp.zeros((n_pairs, s_ref.shape[2]), F32))


def _s5_out_kernel(x_ref, h_ref, perm_ref, t_ref, v_ref, yca_ref, ycb_ref, yla_ref, ylb_ref, y_scr):
    cw = S5_CHUNK * S5_GROUP_SIZE
    n_ctx = yca_ref.shape[0] // S5_CHUNK
    for g in range(S5_GROUPS):
        y = jnp.dot(x_ref[0, :, g * cw:(g + 1) * cw], t_ref[g], preferred_element_type=F32)
        h = jnp.concatenate([h_ref[0, 0, g], h_ref[1, 0, g]], axis=1)
        y += jnp.dot(h.astype(BF16), v_ref[g], preferred_element_type=F32)
        y_scr[:, g * cw:(g + 1) * cw] = y.astype(BF16)
    for v in range(S5_CHUNK // S5_TAUS_PER_TILE):
        yv = jnp.concatenate([y_scr[:, g * cw + v * LANES:g * cw + (v + 1) * LANES] for g in range(S5_GROUPS)],
                             axis=1)
        zv = lax.dot_general(yv, perm_ref[...], (((1,), (1,)), ((), ())), preferred_element_type=F32)
        for w in range(S5_TAUS_PER_TILE):
            tau = v * S5_TAUS_PER_TILE + w
            for hf, (yc_ref, yl_ref) in enumerate(((yca_ref, yla_ref), (ycb_ref, ylb_ref))):
                col = w * S5_WIDTH + hf * LANES
                piece = zv[:, col:col + LANES]
                yc_ref[pl.ds(tau, n_ctx, stride=S5_CHUNK), :] = piece[:n_ctx]
                yl_ref[pl.ds(tau, piece.shape[0] - n_ctx, stride=S5_CHUNK), :] = piece[n_ctx:]


def s5_mixer(ua, ub, mats, *, nb, seq, ctx_len):
    t_sum, w_cat, v_cat, a12 = mats
    g = S5_GROUPS
    nc = (seq + ctx_len) // S5_CHUNK
    n_ctx_chunks = ctx_len // S5_CHUNK
    cw = S5_CHUNK * S5_GROUP_SIZE
    sl = 2 * S5_STATE
    ctx0 = nb * seq // ctx_len
    src = (lax.broadcasted_iota(jnp.int32, (S5_PERM, S5_PERM), 0))
    dst = (lax.broadcasted_iota(jnp.int32, (S5_PERM, S5_PERM), 1))
    src_as_dst = ((src % S5_WIDTH) // S5_GROUP_SIZE) * LANES + (src // S5_WIDTH) * S5_GROUP_SIZE + src % S5_GROUP_SIZE
    perm = (src_as_dst == dst).astype(BF16)
    x_gm, s = pl.pallas_call(
        _s5_state_kernel,
        grid=(nb,),
        in_specs=[pl.BlockSpec((ctx_len, LANES), lambda b: (ctx0 + b, 0))] * 2
        + [pl.BlockSpec((seq, LANES), lambda b: (b, 0))] * 2
        + [_resident(perm.shape), _resident(w_cat.shape)],
        out_specs=[pl.BlockSpec((1, nc, g * cw), lambda b: (b, 0, 0)),
                   pl.BlockSpec((2, 1, g, nc, sl), lambda b: (0, b, 0, 0, 0))],
        out_shape=[jax.ShapeDtypeStruct((nb, nc, g * cw), BF16), jax.ShapeDtypeStruct((2, nb, g, nc, sl), F32)],
        compiler_params=_cparams(("arbitrary",)),
        name="s5_chunk_state",
    )(ua, ub, ua, ub, perm, w_cat)
    h = pl.pallas_call(
        functools.partial(_s5_scan_kernel, n_chunks=nc, n_ctx_chunks=n_ctx_chunks),
        grid=(2,),
        in_specs=[pl.BlockSpec((1, nb * g * nc, sl), lambda d: (d, 0, 0)),
                  pl.BlockSpec((1, 2, nb * g, sl), lambda d: (d, 0, 0, 0))],
        out_specs=pl.BlockSpec((1, nb * g * nc, sl), lambda d: (d, 0, 0)),
        out_shape=jax.ShapeDtypeStruct((2, nb * g * nc, sl), F32),
        compiler_params=_cparams(("arbitrary",)),
        name="s5_chunk_scan",
    )(s.reshape(2, nb * g * nc, sl), a12).reshape(2, nb, g, nc, sl)
    yca, ycb, yla, ylb = pl.pallas_call(
        _s5_out_kernel,
        grid=(nb,),
        in_specs=[pl.BlockSpec((1, nc, g * cw), lambda b: (b, 0, 0), pipeline_mode=pl.Buffered(1)),
                  pl.BlockSpec((2, 1, g, nc, sl), lambda b: (0, b, 0, 0, 0), pipeline_mode=pl.Buffered(1)),
                  _resident(perm.shape), _resident(t_sum.shape), _resident(v_cat.shape)],
        out_specs=[pl.BlockSpec((ctx_len, LANES), lambda b: (b, 0))] * 2
        + [pl.BlockSpec((seq, LANES), lambda b: (b, 0))] * 2,
        out_shape=[jax.ShapeDtypeStruct((nb * ctx_len, LANES), F32)] * 2
        + [jax.ShapeDtypeStruct((nb * seq, LANES), F32)] * 2,
        scratch_shapes=[pltpu.VMEM((nc, g * cw), BF16)],
        compiler_params=_cparams(("arbitrary",)),
        name="s5_readout",
    )(x_gm, h, perm, t_sum, v_cat)
    return jnp.concatenate([yla, yca], axis=0), jnp.concatenate([ylb, ycb], axis=0)


def s5_matrices(lam_re, lam_im, log_dt, b_re, b_im, c_re, c_im, d_skip, nb):
    ch = S5_CHUNK
    gsz = S5_GROUP_SIZE
    dt = jnp.exp(log_dt.astype(F32))[..., None]
    lr = lam_re.astype(F32)
    li = lam_im.astype(F32)

    def power(n):
        n = n.astype(F32)[:, None, None, None]
        mag = jnp.exp(lr * dt * n)
        return mag * jnp.cos(li * dt * n), mag * jnp.sin(li * dt * n)

    ab_re, ab_im = power(jnp.ones((1,), F32))
    ab_re, ab_im = ab_re[0], ab_im[0]
    den = lr * lr + li * li
    nr = ab_re - 1.0
    ni = ab_im
    coef_re = (nr * lr + ni * li) / den
    coef_im = (ni * lr - nr * li) / den
    br = b_re.astype(F32)
    bi = b_im.astype(F32)
    bb_re = coef_re[..., None] * br - coef_im[..., None] * bi
    bb_im = coef_re[..., None] * bi + coef_im[..., None] * br
    cr = c_re.astype(F32)
    ci = c_im.astype(F32)

    p_re, p_im = power(jnp.arange(ch + 1))
    ca_re = cr[None] * p_re[:, :, :, None, :] - ci[None] * p_im[:, :, :, None, :]
    ca_im = cr[None] * p_im[:, :, :, None, :] + ci[None] * p_re[:, :, :, None, :]
    kmat = jnp.einsum('ndgip,dgpj->ndgij', jnp.concatenate([ca_re[:ch], -ca_im[:ch]], axis=-1),
                      jnp.concatenate([bb_re, bb_im], axis=-2), precision=HIGHEST)
    lags = jnp.concatenate([jnp.flip(kmat[1:, 1], axis=0), kmat[:1, 0] + kmat[:1, 1], kmat[1:, 0]], axis=0)
    lag_rows = jnp.transpose(lags, (1, 3, 0, 2)).reshape(S5_GROUPS, gsz, (2 * ch - 1) * gsz)
    t_sum = jnp.concatenate([lag_rows[:, :, (ch - 1 - s) * gsz:(2 * ch - 1 - s) * gsz] for s in range(ch)], axis=1)
    skip = jnp.eye(ch * gsz, dtype=F32)[None] * jnp.tile(d_skip.astype(F32).reshape(S5_GROUPS, 1, gsz), (1, ch, 1)).reshape(S5_GROUPS, 1, ch * gsz)
    t_sum = t_sum + skip

    def w_dir(d, pr, pi):
        wr = pr[..., None] * bb_re[d][None] - pi[..., None] * bb_im[d][None]
        wi = pr[..., None] * bb_im[d][None] + pi[..., None] * bb_re[d][None]
        w = jnp.concatenate([wr, wi], axis=2)
        return jnp.transpose(w, (1, 0, 3, 2)).reshape(S5_GROUPS, ch * gsz, 2 * S5_STATE)
    w_cat = jnp.concatenate([w_dir(0, jnp.flip(p_re[:ch, 0], 0), jnp.flip(p_im[:ch, 0], 0)),
                             w_dir(1, p_re[:ch, 1], p_im[:ch, 1])], axis=-1)

    def v_dir(vr, vi):
        v = jnp.concatenate([vr, vi], axis=-1)
        return jnp.transpose(v, (1, 3, 0, 2)).reshape(S5_GROUPS, 2 * S5_STATE, ch * gsz)
    v_cat = jnp.concatenate([v_dir(ca_re[1:, 0], -ca_im[1:, 0]),
                             v_dir(jnp.flip(ca_re[1:, 1], 0), -jnp.flip(ca_im[1:, 1], 0))], axis=1)

    a1 = jnp.concatenate([p_re[ch], p_re[ch]], axis=-1)
    a2 = jnp.concatenate([-p_im[ch], p_im[ch]], axis=-1)
    a12 = jnp.stack([a1, a2], axis=1)
    a12 = jnp.tile(a12, (1, 1, nb, 1))
    return t_sum.astype(BF16), w_cat.astype(BF16), v_cat.astype(BF16), a12


def _split_bf16(x):
    head = x.astype(BF16)
    return head, (x - head.astype(F32)).astype(BF16)


def _gla_direction(q_ref, k_ref, v_ref, a_ref, wa, ba, o_ref, st_ref, reverse):
    c = GLA_CHUNK
    hk = GLA_HEADS * GLA_DK
    hv = GLA_HEADS * GLA_DV
    rows = q_ref.shape[0]
    nchunks = rows // c
    grank2 = 2 * GLA_GATE_RANK

    a_head, a_rem = _split_bf16(a_ref[...])
    lane = lax.broadcasted_iota(jnp.int32, a_head.shape, 1)
    a_pack = jnp.where(jnp.logical_and(lane >= grank2, lane < 2 * grank2), a_rem, a_head)
    z = jnp.dot(a_pack, wa, preferred_element_type=F32) + ba
    g = (jnp.minimum(z, 0.0) - jnp.log(1.0 + jnp.exp(-jnp.abs(z)))) * (1.0 / GLA_TAU)
    ri = lax.broadcasted_iota(jnp.int32, (rows, rows), 0)
    ci = lax.broadcasted_iota(jnp.int32, (rows, rows), 1)
    ordered = (ri <= ci) if reverse else (ri >= ci)
    tri_bd = jnp.logical_and(ri // c == ci // c, ordered).astype(BF16)
    g_head, g_rem = _split_bf16(g)
    bb = jnp.dot(tri_bd, jnp.concatenate([g_head, g_rem], axis=1), preferred_element_type=F32)
    b = bb[:, :hk] + bb[:, hk:]
    q_t = (q_ref[...].astype(F32) * jnp.exp(b)).astype(BF16)
    k = k_ref[...].astype(F32)
    k_t = (k * jnp.exp(-b)).astype(BF16)

    hrow = lax.broadcasted_iota(jnp.int32, (GLA_HEADS, hk), 0)
    kmask = (lax.broadcasted_iota(jnp.int32, (GLA_HEADS, hk), 1) // GLA_DK == hrow).astype(BF16)
    vmask = (lax.broadcasted_iota(jnp.int32, (GLA_HEADS, hv), 1) // GLA_DV
             == lax.broadcasted_iota(jnp.int32, (GLA_HEADS, hv), 0)).astype(BF16)
    qi = lax.broadcasted_iota(jnp.int32, (c, GLA_HEADS * c), 0)
    kj = lax.broadcasted_iota(jnp.int32, (c, GLA_HEADS * c), 1) % c
    causal = ((qi <= kj) if reverse else (qi >= kj)).astype(F32)
    st_mask = (lax.broadcasted_iota(jnp.int32, (hv, hk), 0) // GLA_DV
               == lax.broadcasted_iota(jnp.int32, (hv, hk), 1) // GLA_DK).astype(F32)

    order = range(nchunks - 1, -1, -1) if reverse else range(nchunks)
    st = st_ref[...]
    for j in order:
        sl = slice(j * c, (j + 1) * c)
        b_c = b[sl]
        b_last = b_c[0:1] if reverse else b_c[c - 1:c]
        v_c = v_ref[sl, :]
        k_bd = jnp.concatenate([k_t[sl] * kmask[h:h + 1] for h in range(GLA_HEADS)], axis=0)
        v_bd = jnp.concatenate([v_c * vmask[h:h + 1] for h in range(GLA_HEADS)], axis=0)
        att = lax.dot_general(q_t[sl], k_bd, (((1,), (1,)), ((), ())), preferred_element_type=F32) * causal
        o = jnp.dot(att.astype(BF16), v_bd, preferred_element_type=F32)
        o += lax.dot_general(q_t[sl], st.astype(BF16), (((1,), (1,)), ((), ())), preferred_element_type=F32)
        o_ref[sl, :] = o.astype(o_ref.dtype)
        k_end = (k[sl] * jnp.exp(b_last - b_c)).astype(BF16)
        kv_t = lax.dot_general(v_c, k_end, (((0,), (0,)), ((), ())), preferred_element_type=F32)
        st = jnp.exp(b_last) * st + kv_t * st_mask
    st_ref[...] = st


def _gla_kernel(qf_ref, kf_ref, vf_ref, af_ref, qr_ref, kr_ref, vr_ref, ar_ref, wa_ref, ba_ref,
                of_ref, or_ref, sf_ref, sr_ref):
    @pl.when(pl.program_id(1) == 0)
    def _():
        sf_ref[...] = jnp.zeros_like(sf_ref)
        sr_ref[...] = jnp.zeros_like(sr_ref)

    hk = GLA_HEADS * GLA_DK
    _gla_direction(qf_ref, kf_ref, vf_ref, af_ref, wa_ref[:, :hk], ba_ref[:, :hk], of_ref, sf_ref, False)
    _gla_direction(qr_ref, kr_ref, vr_ref, ar_ref, wa_ref[:, hk:], ba_ref[:, hk:], or_ref, sr_ref, True)


def gla_scan(gq, gk, gv, ga, wa, ba, *, nb, seq, ctx_len):
    n = gq.shape[0]
    blk = SEQ_BLOCK
    assert ctx_len == blk
    lpb = seq // blk
    ctx0 = nb * lpb

    def fwd(b, s):
        return (jnp.where(s == 0, ctx0 + b, b * lpb + s - 1), 0)

    def rev(b, s):
        return (jnp.where(s == 0, ctx0 + b, b * lpb + lpb - s), 0)

    hk = GLA_HEADS * GLA_DK
    hv = GLA_HEADS * GLA_DV
    specs = []
    for imap in (fwd, rev):
        specs += [pl.BlockSpec((blk, hk), imap), pl.BlockSpec((blk, hk), imap),
                  pl.BlockSpec((blk, hv), imap), pl.BlockSpec((blk, LANES), imap)]
    specs += [_resident(wa.shape), _resident(ba.shape)]
    return pl.pallas_call(
        _gla_kernel,
        grid=(nb, lpb + 1),
        in_specs=specs,
        out_specs=[pl.BlockSpec((blk, hv), fwd), pl.BlockSpec((blk, hv), rev)],
        out_shape=[jax.ShapeDtypeStruct((n, hv), BF16)] * 2,
        scratch_shapes=[pltpu.VMEM((hv, hk), F32), pltpu.VMEM((hv, hk), F32)],
        compiler_params=_cparams(("arbitrary", "arbitrary")),
        name="gla_scan",
    )(gq, gk, gv, ga, gq, gk, gv, ga, wa, ba)


NA_ROWS_PER_STEP = 4
NA_UNION_ROWS = NA_WIN_ROWS + NA_ROWS_PER_STEP - 1


def _na_kernel(q_ref, k_ref, v_ref, kc_ref, vc_ref, bias_ref, o_ref, *, n_rows):
    step = pl.program_id(1)
    hd = NA_HEADS * NA_HEAD_DIM
    lane = lax.broadcasted_iota(jnp.int32, (NA_HEADS, hd), 1)
    hrow = lax.broadcasted_iota(jnp.int32, (NA_HEADS, hd), 0)
    head_mask = (lane // NA_HEAD_DIM == hrow).astype(F32)
    kc = kc_ref[...]
    vc = vc_ref[...]
    last = pl.num_programs(1) - 1
    regime = jnp.where(step == 0, 0, jnp.where(step == last, 2, 1))
    ws = jnp.clip(step * NA_ROWS_PER_STEP - NA_WIN_ROWS // 2, 0, n_rows - NA_UNION_ROWS)
    start = pl.multiple_of(ws * GRID_W, GRID_W)
    kw = k_ref[pl.ds(start, NA_UNION_ROWS * GRID_W), :]
    vw = v_ref[pl.ds(start, NA_UNION_ROWS * GRID_W), :]
    q = q_ref[...].astype(F32)
    acc = jnp.zeros(q.shape, F32)
    for h in range(NA_HEADS):
        m_h = head_mask[h:h + 1]
        qh = (q * m_h).astype(BF16)
        s_lat = lax.dot_general(qh, kw, (((1,), (1,)), ((), ())), preferred_element_type=F32)
        s_lat = s_lat + bias_ref[regime, h]
        s_ctx = lax.dot_general(qh, kc, (((1,), (1,)), ((), ())), preferred_element_type=F32)
        m = jnp.maximum(jnp.max(s_lat, axis=1, keepdims=True), jnp.max(s_ctx, axis=1, keepdims=True))
        p_lat = jnp.exp(s_lat - m)
        p_ctx = jnp.exp(s_ctx - m)
        den = jnp.sum(p_lat, axis=1, keepdims=True) + jnp.sum(p_ctx, axis=1, keepdims=True)
        o = jnp.dot(p_lat.astype(BF16), vw, preferred_element_type=F32)
        o += jnp.dot(p_ctx.astype(BF16), vc, preferred_element_type=F32)
        acc += (o / den) * m_h
    o_ref[...] = acc.astype(o_ref.dtype)


def _na_ctx_kernel(q_ref, k_ref, v_ref, o_ref):
    hd = NA_HEADS * NA_HEAD_DIM
    lane = lax.broadcasted_iota(jnp.int32, (NA_HEADS, hd), 1)
    hrow = lax.broadcasted_iota(jnp.int32, (NA_HEADS, hd), 0)
    head_mask = (lane // NA_HEAD_DIM == hrow).astype(F32)
    q = q_ref[...].astype(F32)
    k = k_ref[...]
    v = v_ref[...]
    acc = jnp.zeros(q.shape, F32)
    for h in range(NA_HEADS):
        m_h = head_mask[h:h + 1]
        s = lax.dot_general((q * m_h).astype(BF16), k, (((1,), (1,)), ((), ())), preferred_element_type=F32)
        p = jnp.exp(s - jnp.max(s, axis=1, keepdims=True))
        o = jnp.dot(p.astype(BF16), v, preferred_element_type=F32) / jnp.sum(p, axis=1, keepdims=True)
        acc += o * m_h
    o_ref[...] = acc.astype(o_ref.dtype)


def neighbourhood_attention(nq, nk, nv, bias, *, nb, seq, ctx_len):
    n, hd = nq.shape
    n_rows = seq // GRID_W
    qb = NA_ROWS_PER_STEP * GRID_W
    steps = seq // qb
    ctx0 = nb * seq // ctx_len
    o_lat = pl.pallas_call(
        functools.partial(_na_kernel, n_rows=n_rows),
        grid=(nb, steps),
        in_specs=[pl.BlockSpec((qb, hd), lambda b, s: (b * steps + s, 0)),
                  pl.BlockSpec((seq, hd), lambda b, s: (b, 0)),
                  pl.BlockSpec((seq, hd), lambda b, s: (b, 0)),
                  pl.BlockSpec((ctx_len, hd), lambda b, s: (ctx0 + b, 0)),
                  pl.BlockSpec((ctx_len, hd), lambda b, s: (ctx0 + b, 0)),
                  _resident(bias.shape)],
        out_specs=pl.BlockSpec((qb, hd), lambda b, s: (b * steps + s, 0)),
        out_shape=jax.ShapeDtypeStruct((nb * seq, hd), BF16),
        compiler_params=_cparams(("arbitrary", "arbitrary")),
        name="na_latent",
    )(nq, nk, nv, nk, nv, bias)
    o_ctx = pl.pallas_call(
        _na_ctx_kernel,
        grid=(nb,),
        in_specs=[pl.BlockSpec((ctx_len, hd), lambda b: (ctx0 + b, 0))] * 3,
        out_specs=pl.BlockSpec((ctx_len, hd), lambda b: (b, 0)),
        out_shape=jax.ShapeDtypeStruct((nb * ctx_len, hd), BF16),
        compiler_params=_cparams(("arbitrary",)),
        name="na_context",
    )(nq, nk, nv)
    return jnp.concatenate([o_lat, o_ctx], axis=0)


def na_bias_tables(rpb):
    rpb = rpb.astype(F32)
    wr, nq, nu = NA_WIN_ROWS, NA_ROWS_PER_STEP, NA_UNION_ROWS
    assert nq <= wr // 2 + 1 and wr >= nq + wr // 2
    c_idx = jnp.arange(GRID_W)
    col_start = jnp.clip(c_idx - NA_WIN_COLS // 2, 0, GRID_W - NA_WIN_COLS)
    col_in = (c_idx[None, :] >= col_start[:, None]) & (c_idx[None, :] < col_start[:, None] + NA_WIN_COLS)
    wc = NA_WIN_COLS
    pad = jnp.zeros(rpb.shape[:-1] + (2 * GRID_W - (2 * wc - 1),), F32)
    table = jnp.concatenate([rpb[..., wc - 1:], pad, rpb[..., :wc - 1]], axis=-1)
    toe = jnp.tile(table, GRID_W)[..., :GRID_W * (2 * GRID_W - 1)]
    toe = toe.reshape(table.shape[:-1] + (GRID_W, 2 * GRID_W - 1))[..., :GRID_W]
    toe = jnp.transpose(jnp.where(col_in[None, None], toe, NEG_BIG), (0, 2, 1, 3))
    per_regime = []
    for lo_of, dr0_of in ((lambda i: 0, lambda i: wr - 1 - i),
                          (lambda i: i, lambda i: wr // 2 - 1),
                          (lambda i: nu - wr, lambda i: nu - wr - i)):
        rows = []
        for i in range(nq):
            lo, dr0 = lo_of(i), dr0_of(i)
            piece = toe[:, :, dr0:dr0 + wr]
            rows.append(jnp.pad(piece, ((0, 0), (0, 0), (lo, nu - wr - lo), (0, 0)), constant_values=NEG_BIG))
        per_regime.append(jnp.stack(rows, axis=1))
    b = jnp.stack(per_regime, axis=0)
    return b.reshape(3, NA_HEADS, nq * GRID_W, nu * GRID_W)


def _conv_kernel(prev_ref, main_ref, next_ref, dw_ref, dwb_ref, lng_ref, lnb_ref, o_ref, buf_ref, shift_ref,
                 *, n_lat_tiles, tiles_per_batch):
    i = pl.program_id(0)
    j = i % tiles_per_batch
    is_lat = i < n_lat_tiles
    has_prev = jnp.logical_and(is_lat, j > 0)
    has_next = jnp.logical_and(is_lat, j < tiles_per_batch - 1)
    cw = CONV_WIDTH
    tl = main_ref.shape[0]

    def glu(a):
        a = a.astype(F32)
        return a[:, :cw] * _sigmoid(a[:, cw:])

    buf_ref[0:CONV_HALO, :] = glu(prev_ref[...]) * has_prev.astype(F32)
    buf_ref[CONV_HALO:CONV_HALO + tl, :] = glu(main_ref[...])
    buf_ref[CONV_HALO + tl:, :] = glu(next_ref[...]) * has_next.astype(F32)
    dw = dw_ref[...]
    acc = jnp.zeros((tl, cw), F32) + dwb_ref[...]
    base = CONV_HALO - CONV_KERNEL // 2
    span = tl + 2 * CONV_HALO - SUBLANES
    for r in range(SUBLANES):
        shift_ref[r] = buf_ref[r:r + span, :]
    for k in range(CONV_KERNEL):
        q, r = divmod(base + k, SUBLANES)
        acc += shift_ref[r, q * SUBLANES:q * SUBLANES + tl, :] * dw[k:k + 1, :]
    mu = jnp.mean(acc, axis=-1, keepdims=True)
    xc = acc - mu
    y = xc * lax.rsqrt(jnp.mean(xc * xc, axis=-1, keepdims=True) + NORM_EPS)
    y = y * lng_ref[...] + lnb_ref[...]
    o_ref[...] = _silu(y).astype(o_ref.dtype)


def conv_branch(cv, dw, dw_b, ln_g, ln_b, *, nb, seq):
    n = cv.shape[0]
    tl = SEQ_BLOCK
    hb = tl // CONV_HALO
    n_tiles = n // tl
    cw = CONV_WIDTH
    return pl.pallas_call(
        functools.partial(_conv_kernel, n_lat_tiles=nb * seq // tl, tiles_per_batch=seq // tl),
        grid=(n_tiles,),
        in_specs=[pl.BlockSpec((CONV_HALO, 2 * cw), lambda i: (jnp.maximum(i * hb - 1, 0), 0)),
                  pl.BlockSpec((tl, 2 * cw), lambda i: (i, 0)),
                  pl.BlockSpec((CONV_HALO, 2 * cw), lambda i: (jnp.minimum((i + 1) * hb, n_tiles * hb - 1), 0)),
                  _resident((CONV_KERNEL, cw)), _resident((1, cw)), _resident((1, cw)), _resident((1, cw))],
        out_specs=pl.BlockSpec((tl, cw), lambda i: (i, 0)),
        out_shape=jax.ShapeDtypeStruct((n, cw), BF16),
        scratch_shapes=[pltpu.VMEM((tl + 2 * CONV_HALO, cw), F32),
                        pltpu.VMEM((SUBLANES, tl + 2 * CONV_HALO - SUBLANES, cw), F32)],
        compiler_params=_cparams(("arbitrary",)),
        name="conv_branch",
    )(cv, cv, cv, dw.astype(F32), dw_b.reshape(1, cw).astype(F32), ln_g.reshape(1, cw).astype(F32),
      ln_b.reshape(1, cw).astype(F32))


def _merge_kernel(xa_ref, xb_ref, mod_ref, gt_ref, ys5a_ref, ys5b_ref, of_ref, or_ref, gr_ref, na_ref, cv_ref,
                  gate_b_ref, wglu_ref, bglu_ref, ws5_ref, gng_ref, wgla_ref, wna_ref, wcv_ref, wmix_ref, o_ref,
                  *, n_lat_tiles):
    d = xa_ref.shape[1]
    z = jax.nn.gelu(jnp.concatenate([ys5a_ref[...], ys5b_ref[...]], axis=1))
    z = z * _sigmoid(jnp.dot(z.astype(BF16), wglu_ref[...], preferred_element_type=F32) + bglu_ref[...])
    br_s5 = jnp.dot(z.astype(BF16), ws5_ref[...], preferred_element_type=F32)
    o = of_ref[...].astype(F32) + or_ref[...].astype(F32)
    r = _silu(gr_ref[...].astype(F32))
    parts = []
    for h in range(GLA_HEADS):
        oh = o[:, h * GLA_DV:(h + 1) * GLA_DV]
        oh = oh * lax.rsqrt(jnp.mean(oh * oh, axis=-1, keepdims=True) + NORM_EPS) * gng_ref[...]
        parts.append(oh * r[:, h * GLA_DV:(h + 1) * GLA_DV])
    y_gla = jnp.concatenate(parts, axis=1).astype(BF16)
    br_gla = jnp.dot(y_gla, wgla_ref[...], preferred_element_type=F32)
    br_na = jnp.dot(na_ref[...], wna_ref[...], preferred_element_type=F32)
    br_cv = jnp.dot(cv_ref[...], wcv_ref[...], preferred_element_type=F32)
    merged = jnp.zeros((xa_ref.shape[0], d), F32)
    for i, br in enumerate((br_s5, br_gla, br_na, br_cv)):
        pre = gt_ref[:, i * d:(i + 1) * d] + gate_b_ref[:, i * d:(i + 1) * d].astype(BF16)
        gate = 0.5 * jnp.tanh(0.5 * pre).astype(F32) + 0.5
        merged += gate * br
    mix = jnp.dot(merged.astype(BF16), wmix_ref[...], preferred_element_type=F32)
    o_ref[...] = _x_pair_tile(xa_ref, xb_ref, n_lat_tiles) + mod_ref[:, 2 * d:3 * d] * mix


def merge_branches(x_lat, x_ctx, ctx_tile0, modtab, gt, ys5a, ys5b, o_f, o_r, gr, o_na, y_cv, weights,
                   *, n_rows, nb, seq, tm):
    d = x_lat.shape[1]
    n_lat_tiles = nb * seq // tm
    tpb = seq // tm

    def row(i):
        return (i, 0)

    acts = (gt, ys5a, ys5b, o_f, o_r, gr, o_na, y_cv)
    return pl.pallas_call(
        functools.partial(_merge_kernel, n_lat_tiles=n_lat_tiles),
        grid=(n_rows // tm,),
        in_specs=_x_pair_specs(tm, d, n_lat_tiles, ctx_tile0)
        + [_mod_row_spec(n_lat_tiles, tpb, nb, modtab.shape[-1])]
        + [pl.BlockSpec((tm, a.shape[1]), row) for a in acts]
        + [_resident(w.shape) for w in weights],
        out_specs=pl.BlockSpec((tm, d), row),
        out_shape=jax.ShapeDtypeStruct((n_rows, d), F32),
        compiler_params=_cparams(("arbitrary",)),
        name="merge_mix",
    )(x_lat, x_ctx, modtab, *acts, *weights)


MOE_PAIRS = MOE_EXPERTS_PER_GROUP * (MOE_EXPERTS_PER_GROUP - 1) // 2
MOE_CLASSES = MOE_GROUPS * MOE_PAIRS
MOE_TILE = 128
PAY_WORDS = 512
PAY_WIDTH = PAY_WORDS + LANES
HIGH_HALF = -65536
LOW_HALF = 65535


def _pack_bf16_pairs(a):
    w = a.shape[1] // 2
    hi = lax.bitcast_convert_type(a[:, :w].astype(BF16).astype(F32), jnp.int32)
    lo = lax.bitcast_convert_type(a[:, w:].astype(BF16).astype(F32), jnp.int32)
    return jnp.bitwise_or(jnp.bitwise_and(hi, HIGH_HALF), jnp.bitwise_and(jnp.right_shift(lo, 16), LOW_HALF))


def _unpack_bf16_pairs(words):
    hi = lax.bitcast_convert_type(jnp.bitwise_and(words, HIGH_HALF), F32)
    lo = lax.bitcast_convert_type(jnp.left_shift(words, 16), F32)
    return jnp.concatenate([hi, lo], axis=1)


def _route_pair_kernel(x_ref, mod_ref, g_ref, wr_ref, br_ref, ltri_ref, pay_ref, meta_ref, cnt_ref):
    d = x_ref.shape[1]
    mod = mod_ref[...]
    h = _modulated_norm(x_ref[...], g_ref[...], mod[:, 3 * d:4 * d], mod[:, 4 * d:5 * d])
    h_head, h_rem = _split_bf16(h)
    nl = br_ref.shape[1]
    both = jnp.dot(h_head, wr_ref[...], preferred_element_type=F32)
    logits = (both[:, :nl] + both[:, nl:] + jnp.dot(h_rem, wr_ref[:, :nl], preferred_element_type=F32)
              + br_ref[...])
    lane = lax.broadcasted_iota(jnp.int32, logits.shape, 1)
    big = jnp.int32(1 << 20)
    is_g = lane < MOE_GROUPS
    gl = jnp.where(is_g, logits, -jnp.inf)
    gmax = jnp.max(gl, axis=1, keepdims=True)
    gidx = jnp.min(jnp.where(gl == gmax, lane, big), axis=1, keepdims=True)
    group_p = 1.0 / jnp.sum(jnp.where(is_g, jnp.exp(logits - gmax), 0.0), axis=1, keepdims=True)
    first = MOE_GROUPS + gidx * MOE_EXPERTS_PER_GROUP
    in_group = jnp.logical_and(lane >= first, lane < first + MOE_EXPERTS_PER_GROUP)
    el = jnp.where(in_group, logits, -jnp.inf)
    v1 = jnp.max(el, axis=1, keepdims=True)
    i1 = jnp.min(jnp.where(el == v1, lane, big), axis=1, keepdims=True)
    el2 = jnp.where(lane == i1, -jnp.inf, el)
    v2 = jnp.max(el2, axis=1, keepdims=True)
    i2 = jnp.min(jnp.where(el2 == v2, lane, big), axis=1, keepdims=True)
    t = jnp.exp(v2 - v1)
    w1 = group_p / (1.0 + t)
    w2 = group_p * t / (1.0 + t)
    k_lo = jnp.minimum(i1, i2) - first
    k_hi = jnp.maximum(i1, i2) - first
    w_lo = jnp.where(i1 < i2, w1, w2)
    w_hi = jnp.where(i1 < i2, w2, w1)
    pair = k_lo * (MOE_EXPERTS_PER_GROUP - 1) - jnp.right_shift(k_lo * (k_lo - 1), 1) + k_hi - k_lo - 1
    cls = gidx * MOE_PAIRS + pair
    pay_ref[:, :PAY_WORDS] = _pack_bf16_pairs(h)
    weights = jnp.where(lane == 0, w_lo, jnp.where(lane == 1, w_hi, 0.0))
    pay_ref[:, PAY_WORDS:] = lax.bitcast_convert_type(weights, jnp.int32)
    onehot = (lane == cls).astype(BF16)
    before = jnp.dot(ltri_ref[...], onehot, preferred_element_type=F32)
    rank = jnp.sum(jnp.where(lane == cls, before, 0.0), axis=1, keepdims=True)
    packed = jnp.where(lane == 0, cls.astype(F32), jnp.where(lane == 1, rank, 0.0))
    meta_ref[0] = jnp.transpose(packed)[0:8, :]
    counts = jnp.sum(onehot.astype(F32), axis=0, keepdims=True)
    cnt_ref[0] = jnp.broadcast_to(counts, cnt_ref.shape[1:]).astype(jnp.int32)


def _row_copy(src_hbm, src_row, dst, dst_row, sem):
    return pltpu.make_async_copy(src_hbm.at[pl.ds(src_row, 1)], dst.at[pl.ds(dst_row, 1)], sem)


def _dispatch_kernel(pos_ref, pay_hbm, zero_hbm, sorted_hbm, sem, *, rows):
    del zero_hbm
    tile = pl.program_id(0)

    def copy(i):
        tok = tile * rows + i
        return _row_copy(pay_hbm, tok, sorted_hbm, pos_ref[tok], sem)

    def issue(i, carry):
        copy(i).start()
        return carry

    def drain(i, carry):
        copy(i).wait()
        return carry

    lax.fori_loop(0, rows, issue, 0, unroll=8)
    lax.fori_loop(0, rows, drain, 0, unroll=8)


def _pair_ffn_kernel(ea_ref, eb_ref, nt_ref, x_ref, w1a_ref, w3a_ref, w2a_ref, w1b_ref, w3b_ref, w2b_ref, y_ref):
    del ea_ref, eb_ref
    j = pl.program_id(0)

    @pl.when(j < nt_ref[0])
    def _():
        x = _unpack_bf16_pairs(x_ref[:, :PAY_WORDS]).astype(BF16)
        wts = lax.bitcast_convert_type(x_ref[:, PAY_WORDS:], F32)

        def ffn(w1_ref, w3_ref, w2_ref, cw):
            a = jnp.dot(x, w1_ref[...], preferred_element_type=F32)
            b = jnp.dot(x, w3_ref[...], preferred_element_type=F32)
            return jnp.dot((_silu(a) * b * cw).astype(BF16), w2_ref[...], preferred_element_type=F32)

        y = ffn(w1a_ref, w3a_ref, w2a_ref, wts[:, 0:1]) + ffn(w1b_ref, w3b_ref, w2b_ref, wts[:, 1:2])
        y_ref[...] = _pack_bf16_pairs(y)

    @pl.when(j >= nt_ref[0])
    def _():
        y_ref[...] = jnp.zeros_like(y_ref)


def _undispatch_kernel(pos_ref, y_hbm, x_ref, mod_ref, fg_ref, o_ref, ybuf, sem, *, final):
    tile = pl.program_id(0)
    rows, d = x_ref.shape

    def copy(i):
        return _row_copy(y_hbm, pos_ref[tile * rows + i], ybuf, i, sem)

    def issue(i, carry):
        copy(i).start()
        return carry

    def drain(i, carry):
        copy(i).wait()
        return carry

    lax.fori_loop(0, rows, issue, 0, unroll=8)
    lax.fori_loop(0, rows, drain, 0, unroll=8)
    y = x_ref[...] + mod_ref[:, 5 * d:6 * d] * _unpack_bf16_pairs(ybuf[...])
    if final:
        y = y * lax.rsqrt(jnp.mean(y * y, axis=-1, keepdims=True) + NORM_EPS) * fg_ref[...]
    o_ref[...] = y


def moe_dispatch_layer(xs, modtab, norm_g, w_router, b_router, w1, w3, w2, final_g,
                       *, layer, n_rows, nb, seq, tm, final):
    d = xs.shape[1]
    assert d == 2 * PAY_WORDS
    n_lat_tiles = nb * seq // tm
    tpb = seq // tm
    n_tiles = n_rows // tm
    ltri = (lax.broadcasted_iota(jnp.int32, (tm, tm), 0) > lax.broadcasted_iota(jnp.int32, (tm, tm), 1)).astype(BF16)
    mod_spec = _mod_row_spec(n_lat_tiles, tpb, nb, modtab.shape[-1])
    pay, meta, cnt = pl.pallas_call(
        _route_pair_kernel,
        grid=(n_tiles,),
        in_specs=[pl.BlockSpec((tm, d), lambda i: (i, 0)), mod_spec,
                  _resident((1, d)), _resident(w_router.shape), _resident(b_router.shape), _resident(ltri.shape)],
        out_specs=[pl.BlockSpec((tm, PAY_WIDTH), lambda i: (i, 0)),
                   pl.BlockSpec((1, 8, tm), lambda i: (i, 0, 0)), pl.BlockSpec((1, 8, LANES), lambda i: (i, 0, 0))],
        out_shape=[jax.ShapeDtypeStruct((n_rows, PAY_WIDTH), jnp.int32),
                   jax.ShapeDtypeStruct((n_tiles, 8, tm), F32), jax.ShapeDtypeStruct((n_tiles, 8, LANES), jnp.int32)],
        compiler_params=_cparams(("arbitrary",)),
        name="moe_route",
    )(xs, modtab, norm_g.reshape(1, d), w_router, b_router, ltri)

    max_tiles = n_rows // MOE_TILE + MOE_CLASSES
    cnt = cnt[:, 0, :]
    total = jnp.sum(cnt, axis=0)
    tiles_per_class = (total + MOE_TILE - 1) // MOE_TILE
    class_start = (jnp.cumsum(tiles_per_class) - tiles_per_class) * MOE_TILE
    base = class_start[None, :] + jnp.cumsum(cnt, axis=0) - cnt
    cls_tok = meta[:, 0, :].astype(jnp.int32)
    rank_tok = meta[:, 1, :].astype(jnp.int32)
    lanes = jnp.arange(LANES, dtype=jnp.int32)
    pos = jnp.sum(jnp.where(cls_tok[:, :, None] == lanes, base[:, None, :], 0), axis=-1) + rank_tok
    pos = pos.reshape(n_rows)
    tile_end = jnp.cumsum(tiles_per_class)
    tile_cls = jnp.sum((jnp.arange(max_tiles, dtype=jnp.int32)[:, None] >= tile_end[None, :MOE_CLASSES]), axis=1)
    tile_cls = jnp.minimum(tile_cls, MOE_CLASSES - 1)
    grp, pair = tile_cls // MOE_PAIRS, tile_cls % MOE_PAIRS
    pair_lo = [a for a in range(MOE_EXPERTS_PER_GROUP) for _ in range(a + 1, MOE_EXPERTS_PER_GROUP)]
    pair_hi = [b for a in range(MOE_EXPERTS_PER_GROUP) for b in range(a + 1, MOE_EXPERTS_PER_GROUP)]
    pid = jnp.arange(MOE_PAIRS, dtype=jnp.int32)
    k_lo = jnp.sum(jnp.where(pair[:, None] == pid, jnp.asarray(pair_lo, jnp.int32), 0), axis=1)
    k_hi = jnp.sum(jnp.where(pair[:, None] == pid, jnp.asarray(pair_hi, jnp.int32), 0), axis=1)
    ea = (grp * MOE_EXPERTS_PER_GROUP + k_lo).astype(jnp.int32)
    eb = (grp * MOE_EXPERTS_PER_GROUP + k_hi).astype(jnp.int32)
    n_used = tile_end[MOE_CLASSES - 1].reshape(1).astype(jnp.int32)

    n_slots = max_tiles * MOE_TILE
    hbm = pl.BlockSpec(memory_space=pl.ANY)
    sorted_pay = pl.pallas_call(
        functools.partial(_dispatch_kernel, rows=tm),
        grid_spec=pltpu.PrefetchScalarGridSpec(
            num_scalar_prefetch=1, grid=(n_tiles,), in_specs=[hbm, hbm], out_specs=hbm,
            scratch_shapes=[pltpu.SemaphoreType.DMA(())]),
        out_shape=jax.ShapeDtypeStruct((n_slots, PAY_WIDTH), jnp.int32),
        input_output_aliases={2: 0},
        compiler_params=_cparams(("arbitrary",)),
        name="moe_dispatch",
    )(pos, pay, jnp.zeros((n_slots, PAY_WIDTH), jnp.int32))

    epg, hid = w1.shape[2], w1.shape[-1]

    def wspec(shape, which):
        def imap(j, ea_ref, eb_ref, nt_ref):
            e = (ea_ref, eb_ref)[which][j]
            return (layer, e // epg, e % epg, 0, 0)
        return pl.BlockSpec((None, None, None) + shape, imap)

    y_sorted = pl.pallas_call(
        _pair_ffn_kernel,
        grid_spec=pltpu.PrefetchScalarGridSpec(
            num_scalar_prefetch=3, grid=(max_tiles,),
            in_specs=[pl.BlockSpec((MOE_TILE, PAY_WIDTH), lambda j, *_: (j, 0)),
                      wspec((d, hid), 0), wspec((d, hid), 0), wspec((hid, d), 0),
                      wspec((d, hid), 1), wspec((d, hid), 1), wspec((hid, d), 1)],
            out_specs=pl.BlockSpec((MOE_TILE, PAY_WORDS), lambda j, *_: (j, 0))),
        out_shape=jax.ShapeDtypeStruct((n_slots, PAY_WORDS), jnp.int32),
        compiler_params=_cparams(("arbitrary",)),
        name="moe_pair_ffn",
    )(ea, eb, n_used, sorted_pay, w1, w3, w2, w1, w3, w2)

    return pl.pallas_call(
        functools.partial(_undispatch_kernel, final=final),
        grid_spec=pltpu.PrefetchScalarGridSpec(
            num_scalar_prefetch=1, grid=(n_tiles,),
            in_specs=[hbm,
                      pl.BlockSpec((tm, d), lambda i, p: (i, 0)),
                      pl.BlockSpec((None, 1, modtab.shape[-1]),
                                   lambda i, p: (jnp.where(i < n_lat_tiles, i // tpb, nb), 0, 0)),
                      pl.BlockSpec((1, d), lambda i, p: (0, 0))],
            out_specs=pl.BlockSpec((tm, d), lambda i, p: (i, 0)),
            scratch_shapes=[pltpu.VMEM((tm, PAY_WORDS), jnp.int32), pltpu.SemaphoreType.DMA(())]),
        out_shape=jax.ShapeDtypeStruct((n_rows, d), F32),
        compiler_params=_cparams(("arbitrary",)),
        name="moe_undispatch",
    )(pos, y_sorted, xs, modtab, final_g.reshape(1, d).astype(F32))


def rope_tables(seq, pad_rows):
    t = jnp.arange(seq, dtype=jnp.int32)
    row = (t // GRID_W).astype(F32)
    colp = (t % GRID_W).astype(F32)
    half = GLA_DK // 2
    inv_freq = ROPE_BASE ** (-jnp.arange(0, half, 2, dtype=F32) / half)
    ang_r = row[:, None] * inv_freq
    ang_c = colp[:, None] * inv_freq
    dd = jnp.arange(GLA_DK)
    ang = jnp.where((dd < half)[None, :], ang_r[:, dd % (half // 2)], ang_c[:, dd % (half // 2)])
    sign = jnp.where((dd % half) < half // 2, -1.0, 1.0).astype(F32)
    cos = jnp.tile(jnp.cos(ang), (1, GLA_HEADS))
    sin = jnp.tile(jnp.sin(ang) * sign[None, :], (1, GLA_HEADS))
    cos = jnp.concatenate([cos, jnp.ones((pad_rows, cos.shape[1]), F32)], axis=0)
    sin = jnp.concatenate([sin, jnp.zeros((pad_rows, sin.shape[1]), F32)], axis=0)
    return cos, sin


def split_in_weights(w_in, d):
    widths = (S5_WIDTH, GLA_HEADS * GLA_DK, GLA_HEADS * GLA_DK, GLA_HEADS * GLA_DV, GLA_HEADS * GLA_DV,
              2 * GLA_GATE_RANK, NA_HEADS * NA_HEAD_DIM, NA_HEADS * NA_HEAD_DIM, NA_HEADS * NA_HEAD_DIM,
              2 * CONV_WIDTH, N_BRANCHES * d)
    names = ('u', 'gq', 'gk', 'gv', 'gr', 'ga', 'nq', 'nk', 'nv', 'cv', 'gt')
    parts = {}
    col = 0
    for nme, w in zip(names, widths):
        parts[nme] = w_in[:, col:col + w]
        col += w
    swap = jnp.arange(GLA_HEADS * GLA_DK) ^ (GLA_DK // 4)
    gq = parts['gq'] * (GLA_DK ** -0.5)
    wqk = jnp.concatenate([gq, gq[:, swap], parts['gk'], parts['gk'][:, swap]], axis=1)
    ga = jnp.concatenate([parts['ga']] * 3 + [jnp.zeros((d, LANES - 6 * GLA_GATE_RANK), w_in.dtype)], axis=1)
    wmisc = jnp.concatenate([parts['u'], parts['nq'] * (NA_HEAD_DIM ** -0.5), parts['nk'], parts['nv'],
                             parts['gv'], parts['gr'], parts['cv'], ga], axis=1)
    return wqk.astype(BF16), wmisc.astype(BF16), parts['gt'].astype(BF16)


def kernel(x, c, ctx, c_ctx, norm1_g, norm2_g, w_mod, b_mod, w_in, gate_b, w_mix_out, s5_lam_re, s5_lam_im, s5_log_dt, s5_b_re, s5_b_im, s5_c_re, s5_c_im, s5_d, s5_w_glu, s5_b_glu, s5_w_out, gla_w_a2, gla_b_a, gla_norm_g, gla_w_out, na_rpb, na_w_out, conv_dw, conv_dw_b, conv_ln_g, conv_ln_b, conv_w_out, moe_w_group, moe_b_group, moe_w_expert, moe_b_expert, moe_w1, moe_w3, moe_w2, final_norm_g):
    nb, seq, d = x.shape
    ctx_len = ctx.shape[1]
    depth = w_mod.shape[0]
    n_lat = nb * seq
    tm = 512
    tm_moe = math.gcd(1024, nb * ctx_len)
    assert ctx_len == SEQ_BLOCK and seq % tm_moe == 0 and (nb * ctx_len) % tm_moe == 0 and nb < MOD_ROWS

    n_all = n_lat + nb * ctx_len
    x_lat, x_ctx, ctx_tile0 = x.reshape(n_lat, d).astype(F32), ctx.reshape(nb * ctx_len, d).astype(F32), 0
    c_rows = jnp.zeros((MOD_ROWS, d), F32).at[:nb].set(c.astype(F32)).at[nb].set(c_ctx.astype(F32))
    modtab = modulation_table(c_rows, w_mod.astype(F32), b_mod.astype(F32))
    modtab = modtab.reshape(depth, MOD_ROWS, 1, 6 * d)
    cos_tab, sin_tab = rope_tables(seq, tm)

    moe_w1_bf, moe_w3_bf, moe_w2_bf = moe_w1.astype(BF16), moe_w3.astype(BF16), moe_w2.astype(BF16)

    for i in range(depth):
        last = i == depth - 1
        n_rows = n_lat if last else n_all
        wqk, wmisc, wgate = split_in_weights(w_in[i], d)
        gq, gk, ua, ub, nq, nk, nv, gv, gr, cv, ga, gt = in_projection(
            x_lat, x_ctx, ctx_tile0, modtab[i], norm1_g[i].astype(F32), cos_tab, sin_tab, wqk, wmisc, wgate,
            n=n_all, nb=nb, seq=seq, tm=tm)

        mats = s5_matrices(s5_lam_re[i], s5_lam_im[i], s5_log_dt[i], s5_b_re[i], s5_b_im[i],
                           s5_c_re[i], s5_c_im[i], s5_d[i], nb)
        ys5a, ys5b = s5_mixer(ua, ub, mats, nb=nb, seq=seq, ctx_len=ctx_len)

        hk = GLA_HEADS * GLA_DK
        zero = jnp.zeros((GLA_GATE_RANK, hk), F32)
        wd = jnp.concatenate([jnp.concatenate([gla_w_a2[i, 0].astype(F32), zero], axis=1),
                              jnp.concatenate([zero, gla_w_a2[i, 1].astype(F32)], axis=1)], axis=0)
        wd_head, wd_rem = _split_bf16(wd)
        wa = jnp.concatenate([wd_head, wd_head, wd_rem,
                              jnp.zeros((LANES - 6 * GLA_GATE_RANK, 2 * hk), BF16)], axis=0)
        ba = gla_b_a[i].astype(F32).reshape(1, 2 * hk)
        o_f, o_r = gla_scan(gq, gk, gv, ga, wa, ba, nb=nb, seq=seq, ctx_len=ctx_len)

        o_na = neighbourhood_attention(nq, nk, nv, na_bias_tables(na_rpb[i]), nb=nb, seq=seq, ctx_len=ctx_len)
        y_cv = conv_branch(cv, conv_dw[i], conv_dw_b[i], conv_ln_g[i], conv_ln_b[i], nb=nb, seq=seq)

        weights = (gate_b[i].astype(F32).reshape(1, N_BRANCHES * d), s5_w_glu[i].astype(BF16),
                   s5_b_glu[i].astype(F32).reshape(1, S5_WIDTH), s5_w_out[i].astype(BF16),
                   gla_norm_g[i].astype(F32).reshape(1, GLA_DV), gla_w_out[i].astype(BF16),
                   na_w_out[i].astype(BF16), conv_w_out[i].astype(BF16), w_mix_out[i].astype(BF16))
        xs = merge_branches(x_lat, x_ctx, ctx_tile0, modtab[i], gt, ys5a, ys5b, o_f, o_r, gr, o_na, y_cv, weights,
                            n_rows=n_rows, nb=nb, seq=seq, tm=tm)

        n_router = MOE_GROUPS + N_EXPERTS
        w_router = jnp.pad(jnp.concatenate([moe_w_group[i], moe_w_expert[i]], axis=1).astype(F32),
                           ((0, 0), (0, LANES - n_router)))
        w_router = jnp.concatenate(_split_bf16(w_router), axis=1)
        b_router = jnp.pad(jnp.concatenate([moe_b_group[i], moe_b_expert[i]]).astype(F32),
                           (0, LANES - n_router)).reshape(1, LANES)
        xs = moe_dispatch_layer(xs, modtab[i], norm2_g[i].astype(F32), w_router, b_router,
                       moe_w1_bf, moe_w3_bf, moe_w2_bf, final_norm_g, layer=i,
                       n_rows=n_rows, nb=nb, seq=seq, tm=tm_moe, final=last)
        x_lat, x_ctx, ctx_tile0 = xs, xs, n_lat // tm

    return xs.reshape(nb, seq, d).astype(x.dtype)
```

```python
import functools
import math

import jax
import jax.numpy as jnp
from jax import lax
from jax.experimental import pallas as pl
from jax.experimental.pallas import tpu as pltpu

F32 = jnp.float32
BF16 = jnp.bfloat16
HIGHEST = lax.Precision.HIGHEST

GRID_W = 64
NORM_EPS = 1e-6
N_BRANCHES = 4
S5_WIDTH = 256
S5_GROUP_SIZE = 16
S5_GROUPS = 16
S5_STATE = 64
GLA_HEADS = 4
GLA_DK = 64
GLA_DV = 128
GLA_GATE_RANK = 16
GLA_TAU = 16.0
GLA_CHUNK = 64
ROPE_BASE = 10000.0
NA_HEADS = 4
NA_HEAD_DIM = 64
NA_WIN_ROWS = 8
NA_WIN_COLS = 16
CONV_WIDTH = 256
CONV_KERNEL = 31
MOE_GROUPS = 4
MOE_EXPERTS_PER_GROUP = 8
MOE_HIDDEN = 256
N_EXPERTS = MOE_GROUPS * MOE_EXPERTS_PER_GROUP

LANES = 128
SUBLANES = 8
MOD_ROWS = 8
VMEM_LIMIT = 56 * 1024 * 1024
S5_CHUNK = 32
SEQ_BLOCK = 256
CONV_HALO = 16
NEG_BIG = -1e30


def _cparams(sem):
    return pltpu.CompilerParams(dimension_semantics=sem, vmem_limit_bytes=VMEM_LIMIT)


def _resident(shape):
    nd = len(shape)
    return pl.BlockSpec(shape, lambda *_: (0,) * nd, pipeline_mode=pl.Buffered(1))


def _sigmoid(x):
    return 0.5 * jnp.tanh(0.5 * x) + 0.5


def _silu(x):
    return x * _sigmoid(x)


def _mod_kernel(c_ref, w_ref, b_ref, o_ref):
    c = c_ref[...]
    o_ref[0] = jnp.dot(_silu(c), w_ref[0], preferred_element_type=F32, precision=HIGHEST) + b_ref[0]


def modulation_table(c_rows, w_mod, b_mod):
    depth, d, n6 = w_mod.shape
    tn = 1024
    return pl.pallas_call(
        _mod_kernel,
        grid=(depth, n6 // tn),
        in_specs=[
            pl.BlockSpec((MOD_ROWS, d), lambda l, j: (0, 0)),
            pl.BlockSpec((1, d, tn), lambda l, j: (l, 0, j)),
            pl.BlockSpec((1, 1, tn), lambda l, j: (l, 0, j)),
        ],
        out_specs=pl.BlockSpec((1, MOD_ROWS, tn), lambda l, j: (l, 0, j)),
        out_shape=jax.ShapeDtypeStruct((depth, MOD_ROWS, n6), F32),
        compiler_params=_cparams(("arbitrary", "arbitrary")),
        name="mod_table",
    )(c_rows, w_mod, b_mod.reshape(depth, 1, n6))


def _mod_row_spec(n_lat_tiles, tiles_per_batch, nb, width):
    def imap(i, *_):
        return (jnp.where(i < n_lat_tiles, i // tiles_per_batch, nb), 0, 0)
    return pl.BlockSpec((None, 1, width), imap)


def _modulated_norm(x, g, shift, scale):
    y = x * lax.rsqrt(jnp.mean(x * x, axis=-1, keepdims=True) + NORM_EPS)
    return (y * g) * (1.0 + scale) + shift


IN_CHUNK = 512


def _x_pair_specs(tm, d, n_lat_tiles, ctx_tile0):
    return [pl.BlockSpec((tm, d), lambda i: (jnp.minimum(i, n_lat_tiles - 1), 0)),
            pl.BlockSpec((tm, d), lambda i: (ctx_tile0 + jnp.maximum(i - n_lat_tiles, 0), 0))]


def _x_pair_tile(xa_ref, xb_ref, n_lat_tiles):
    return jnp.where(pl.program_id(0) < n_lat_tiles, xa_ref[...], xb_ref[...])


def _inproj_kernel(xa_ref, xb_ref, mod_ref, g_ref, cos_ref, sin_ref, wqk_ref, wmisc_ref, wgate_ref,
                   gq_ref, gk_ref, ua_ref, ub_ref, nq_ref, nk_ref, nv_ref, gv_ref, gr_ref, cv_ref, ga_ref, gt_ref,
                   *, n_lat_tiles):
    d = xa_ref.shape[1]
    mod = mod_ref[...]
    x = _x_pair_tile(xa_ref, xb_ref, n_lat_tiles)
    h = _modulated_norm(x, g_ref[...], mod[:, 0:d], mod[:, d:2 * d]).astype(BF16)
    cos = cos_ref[...]
    sin = sin_ref[...]
    for j, o_ref in enumerate((gq_ref, gk_ref)):
        y = jnp.dot(h, wqk_ref[:, j * 512:(j + 1) * 512], preferred_element_type=F32)
        o_ref[...] = (y[:, :256] * cos + y[:, 256:] * sin).astype(o_ref.dtype)
    col = 0
    for o_ref in (ua_ref, ub_ref, nq_ref, nk_ref, nv_ref, gv_ref, gr_ref, cv_ref, ga_ref):
        w = o_ref.shape[1]
        o_ref[...] = jnp.dot(h, wmisc_ref[:, col:col + w], preferred_element_type=F32).astype(o_ref.dtype)
        col += w
    for j in range(gt_ref.shape[1] // IN_CHUNK):
        sl = slice(j * IN_CHUNK, (j + 1) * IN_CHUNK)
        gt_ref[:, sl] = jnp.dot(h, wgate_ref[:, sl], preferred_element_type=F32).astype(gt_ref.dtype)


def in_projection(x_lat, x_ctx, ctx_tile0, modtab, norm_g, cos_tab, sin_tab, wqk, wmisc, wgate, *, n, nb, seq, tm):
    d = x_lat.shape[1]
    n_lat_tiles = nb * seq // tm
    tpb = seq // tm
    widths = (256, 256, LANES, LANES, 256, 256, 256, 512, 512, 512, LANES, N_BRANCHES * d)
    dtypes = (BF16, BF16, F32, F32) + (BF16,) * 6 + (F32, BF16)

    def row(i):
        return (i, 0)

    def rope_row(i):
        return (jnp.where(i < n_lat_tiles, i % tpb, tpb), 0)

    return pl.pallas_call(
        functools.partial(_inproj_kernel, n_lat_tiles=n_lat_tiles),
        grid=(n // tm,),
        in_specs=_x_pair_specs(tm, d, n_lat_tiles, ctx_tile0) + [
            _mod_row_spec(n_lat_tiles, tpb, nb, modtab.shape[-1]),
            _resident((1, d)),
            pl.BlockSpec((tm, 256), rope_row),
            pl.BlockSpec((tm, 256), rope_row),
            _resident(wqk.shape),
            _resident(wmisc.shape),
            _resident(wgate.shape),
        ],
        out_specs=[pl.BlockSpec((tm, w), row) for w in widths],
        out_shape=[jax.ShapeDtypeStruct((n, w), dt) for w, dt in zip(widths, dtypes)],
        compiler_params=_cparams(("arbitrary",)),
        name="in_proj",
    )(x_lat, x_ctx, modtab, norm_g.reshape(1, d), cos_tab, sin_tab, wqk, wmisc, wgate)


S5_TAUS_PER_TILE = LANES // S5_GROUP_SIZE
S5_PERM = S5_TAUS_PER_TILE * S5_WIDTH


def _s5_state_kernel(uca_ref, ucb_ref, ula_ref, ulb_ref, perm_ref, w_ref, x_ref, s_ref):
    n_ctx, n_lat = uca_ref.shape[0] // S5_CHUNK, ula_ref.shape[0] // S5_CHUNK
    for v in range(S5_CHUNK // S5_TAUS_PER_TILE):
        pieces = []
        for w in range(S5_TAUS_PER_TILE):
            tau = v * S5_TAUS_PER_TILE + w
            for uc_ref, ul_ref in ((uca_ref, ula_ref), (ucb_ref, ulb_ref)):
                pieces.append(jnp.concatenate([uc_ref[pl.ds(tau, n_ctx, stride=S5_CHUNK), :],
                                               ul_ref[pl.ds(tau, n_lat, stride=S5_CHUNK), :]], axis=0))
        z = jnp.concatenate(pieces, axis=1).astype(BF16)
        xv = jnp.dot(z, perm_ref[...], preferred_element_type=F32).astype(BF16)
        for g in range(S5_GROUPS):
            col = g * S5_CHUNK * S5_GROUP_SIZE + v * LANES
            x_ref[0, :, col:col + LANES] = xv[:, g * LANES:(g + 1) * LANES]
    cw = S5_CHUNK * S5_GROUP_SIZE
    for g in range(S5_GROUPS):
        s = jnp.dot(x_ref[0, :, g * cw:(g + 1) * cw], w_ref[g], preferred_element_type=F32)
        s_ref[0, 0, g] = s[:, :2 * S5_STATE]
        s_ref[1, 0, g] = s[:, 2 * S5_STATE:]


def _s5_scan_kernel(s_ref, a_ref, h_ref, *, n_chunks, n_ctx_chunks):
    d = pl.program_id(0)
    nc = n_chunks
    n_pairs = s_ref.shape[1] // nc
    a1 = a_ref[0, 0]
    a2 = a_ref[0, 1]

    def step(s, h):
        fwd_row = s
        rev_row = jnp.where(s < n_ctx_chunks, n_ctx_chunks - 1 - s, nc - 1 - (s - n_ctx_chunks))
        rows = pl.ds(jnp.where(d == 0, fwd_row, rev_row), n_pairs, stride=nc)
        h_ref[0, rows, :] = h
        return h * a1 + pltpu.roll(h, S5_STATE, axis=1) * a2 + s_ref[0, rows, :]

    lax.fori_loop(0, nc, step, jnp.zeros((n_pairs, s_ref.shape[2]), F32))


def _s5_out_kernel(x_ref, h_ref, perm_ref, t_ref, v_ref, yca_ref, ycb_ref, yla_ref, ylb_ref, y_scr):
    cw = S5_CHUNK * S5_GROUP_SIZE
    n_ctx = yca_ref.shape[0] // S5_CHUNK
    for g in range(S5_GROUPS):
        y = jnp.dot(x_ref[0, :, g * cw:(g + 1) * cw], t_ref[g], preferred_element_type=F32)
        h = jnp.concatenate([h_ref[0, 0, g], h_ref[1, 0, g]], axis=1)
        y += jnp.dot(h.astype(BF16), v_ref[g], preferred_element_type=F32)
        y_scr[:, g * cw:(g + 1) * cw] = y.astype(BF16)
    for v in range(S5_CHUNK // S5_TAUS_PER_TILE):
        yv = jnp.concatenate([y_scr[:, g * cw + v * LANES:g * cw + (v + 1) * LANES] for g in range(S5_GROUPS)],
                             axis=1)
        zv = lax.dot_general(yv, perm_ref[...], (((1,), (1,)), ((), ())), preferred_element_type=F32)
        for w in range(S5_TAUS_PER_TILE):
            tau = v * S5_TAUS_PER_TILE + w
            for hf, (yc_ref, yl_ref) in enumerate(((yca_ref, yla_ref), (ycb_ref, ylb_ref))):
                col = w * S5_WIDTH + hf * LANES
                piece = zv[:, col:col + LANES]
                yc_ref[pl.ds(tau, n_ctx, stride=S5_CHUNK), :] = piece[:n_ctx]
                yl_ref[pl.ds(tau, piece.shape[0] - n_ctx, stride=S5_CHUNK), :] = piece[n_ctx:]


def s5_mixer(ua, ub, mats, *, nb, seq, ctx_len):
    t_sum, w_cat, v_cat, a12 = mats
    g = S5_GROUPS
    nc = (seq + ctx_len) // S5_CHUNK
    n_ctx_chunks = ctx_len // S5_CHUNK
    cw = S5_CHUNK * S5_GROUP_SIZE
    sl = 2 * S5_STATE
    ctx0 = nb * seq // ctx_len
    src = (lax.broadcasted_iota(jnp.int32, (S5_PERM, S5_PERM), 0))
    dst = (lax.broadcasted_iota(jnp.int32, (S5_PERM, S5_PERM), 1))
    src_as_dst = ((src % S5_WIDTH) // S5_GROUP_SIZE) * LANES + (src // S5_WIDTH) * S5_GROUP_SIZE + src % S5_GROUP_SIZE
    perm = (src_as_dst == dst).astype(BF16)
    x_gm, s = pl.pallas_call(
        _s5_state_kernel,
        grid=(nb,),
        in_specs=[pl.BlockSpec((ctx_len, LANES), lambda b: (ctx0 + b, 0))] * 2
        + [pl.BlockSpec((seq, LANES), lambda b: (b, 0))] * 2
        + [_resident(perm.shape), _resident(w_cat.shape)],
        out_specs=[pl.BlockSpec((1, nc, g * cw), lambda b: (b, 0, 0)),
                   pl.BlockSpec((2, 1, g, nc, sl), lambda b: (0, b, 0, 0, 0))],
        out_shape=[jax.ShapeDtypeStruct((nb, nc, g * cw), BF16), jax.ShapeDtypeStruct((2, nb, g, nc, sl), F32)],
        compiler_params=_cparams(("arbitrary",)),
        name="s5_chunk_state",
    )(ua, ub, ua, ub, perm, w_cat)
    h = pl.pallas_call(
        functools.partial(_s5_scan_kernel, n_chunks=nc, n_ctx_chunks=n_ctx_chunks),
        grid=(2,),
        in_specs=[pl.BlockSpec((1, nb * g * nc, sl), lambda d: (d, 0, 0)),
                  pl.BlockSpec((1, 2, nb * g, sl), lambda d: (d, 0, 0, 0))],
        out_specs=pl.BlockSpec((1, nb * g * nc, sl), lambda d: (d, 0, 0)),
        out_shape=jax.ShapeDtypeStruct((2, nb * g * nc, sl), F32),
        compiler_params=_cparams(("arbitrary",)),
        name="s5_chunk_scan",
    )(s.reshape(2, nb * g * nc, sl), a12).reshape(2, nb, g, nc, sl)
    yca, ycb, yla, ylb = pl.pallas_call(
        _s5_out_kernel,
        grid=(nb,),
        in_specs=[pl.BlockSpec((1, nc, g * cw), lambda b: (b, 0, 0), pipeline_mode=pl.Buffered(1)),
                  pl.BlockSpec((2, 1, g, nc, sl), lambda b: (0, b, 0, 0, 0), pipeline_mode=pl.Buffered(1)),
                  _resident(perm.shape), _resident(t_sum.shape), _resident(v_cat.shape)],
        out_specs=[pl.BlockSpec((ctx_len, LANES), lambda b: (b, 0))] * 2
        + [pl.BlockSpec((seq, LANES), lambda b: (b, 0))] * 2,
        out_shape=[jax.ShapeDtypeStruct((nb * ctx_len, LANES), F32)] * 2
        + [jax.ShapeDtypeStruct((nb * seq, LANES), F32)] * 2,
        scratch_shapes=[pltpu.VMEM((nc, g * cw), BF16)],
        compiler_params=_cparams(("arbitrary",)),
        name="s5_readout",
    )(x_gm, h, perm, t_sum, v_cat)
    return jnp.concatenate([yla, yca], axis=0), jnp.concatenate([ylb, ycb], axis=0)


def s5_matrices(lam_re, lam_im, log_dt, b_re, b_im, c_re, c_im, d_skip, nb):
    ch = S5_CHUNK
    gsz = S5_GROUP_SIZE
    dt = jnp.exp(log_dt.astype(F32))[..., None]
    lr = lam_re.astype(F32)
    li = lam_im.astype(F32)

    def power(n):
        n = n.astype(F32)[:, None, None, None]
        mag = jnp.exp(lr * dt * n)
        return mag * jnp.cos(li * dt * n), mag * jnp.sin(li * dt * n)

    ab_re, ab_im = power(jnp.ones((1,), F32))
    ab_re, ab_im = ab_re[0], ab_im[0]
    den = lr * lr + li * li
    nr = ab_re - 1.0
    ni = ab_im
    coef_re = (nr * lr + ni * li) / den
    coef_im = (ni * lr - nr * li) / den
    br = b_re.astype(F32)
    bi = b_im.astype(F32)
    bb_re = coef_re[..., None] * br - coef_im[..., None] * bi
    bb_im = coef_re[..., None] * bi + coef_im[..., None] * br
    cr = c_re.astype(F32)
    ci = c_im.astype(F32)

    p_re, p_im = power(jnp.arange(ch + 1))
    ca_re = cr[None] * p_re[:, :, :, None, :] - ci[None] * p_im[:, :, :, None, :]
    ca_im = cr[None] * p_im[:, :, :, None, :] + ci[None] * p_re[:, :, :, None, :]
    kmat = jnp.einsum('ndgip,dgpj->ndgij', jnp.concatenate([ca_re[:ch], -ca_im[:ch]], axis=-1),
                      jnp.concatenate([bb_re, bb_im], axis=-2), precision=HIGHEST)
    lags = jnp.concatenate([jnp.flip(kmat[1:, 1], axis=0), kmat[:1, 0] + kmat[:1, 1], kmat[1:, 0]], axis=0)
    lag_rows = jnp.transpose(lags, (1, 3, 0, 2)).reshape(S5_GROUPS, gsz, (2 * ch - 1) * gsz)
    t_sum = jnp.concatenate([lag_rows[:, :, (ch - 1 - s) * gsz:(2 * ch - 1 - s) * gsz] for s in range(ch)], axis=1)
    skip = jnp.eye(ch * gsz, dtype=F32)[None] * jnp.tile(d_skip.astype(F32).reshape(S5_GROUPS, 1, gsz), (1, ch, 1)).reshape(S5_GROUPS, 1, ch * gsz)
    t_sum = t_sum + skip

    def w_dir(d, pr, pi):
        wr = pr[..., None] * bb_re[d][None] - pi[..., None] * bb_im[d][None]
        wi = pr[..., None] * bb_im[d][None] + pi[..., None] * bb_re[d][None]
        w = jnp.concatenate([wr, wi], axis=2)
        return jnp.transpose(w, (1, 0, 3, 2)).reshape(S5_GROUPS, ch * gsz, 2 * S5_STATE)
    w_cat = jnp.concatenate([w_dir(0, jnp.flip(p_re[:ch, 0], 0), jnp.flip(p_im[:ch, 0], 0)),
                             w_dir(1, p_re[:ch, 1], p_im[:ch, 1])], axis=-1)

    def v_dir(vr, vi):
        v = jnp.concatenate([vr, vi], axis=-1)
        return jnp.transpose(v, (1, 3, 0, 2)).reshape(S5_GROUPS, 2 * S5_STATE, ch * gsz)
    v_cat = jnp.concatenate([v_dir(ca_re[1:, 0], -ca_im[1:, 0]),
                             v_dir(jnp.flip(ca_re[1:, 1], 0), -jnp.flip(ca_im[1:, 1], 0))], axis=1)

    a1 = jnp.concatenate([p_re[ch], p_re[ch]], axis=-1)
    a2 = jnp.concatenate([-p_im[ch], p_im[ch]], axis=-1)
    a12 = jnp.stack([a1, a2], axis=1)
    a12 = jnp.tile(a12, (1, 1, nb, 1))
    return t_sum.astype(BF16), w_cat.astype(BF16), v_cat.astype(BF16), a12


def _split_bf16(x):
    head = x.astype(BF16)
    return head, (x - head.astype(F32)).astype(BF16)


def _gla_direction(q_ref, k_ref, v_ref, a_ref, wa, ba, o_ref, st_ref, reverse):
    c = GLA_CHUNK
    hk = GLA_HEADS * GLA_DK
    hv = GLA_HEADS * GLA_DV
    rows = q_ref.shape[0]
    nchunks = rows // c
    grank2 = 2 * GLA_GATE_RANK

    a_head, a_rem = _split_bf16(a_ref[...])
    lane = lax.broadcasted_iota(jnp.int32, a_head.shape, 1)
    a_pack = jnp.where(jnp.logical_and(lane >= grank2, lane < 2 * grank2), a_rem, a_head)
    z = jnp.dot(a_pack, wa, preferred_element_type=F32) + ba
    g = (jnp.minimum(z, 0.0) - jnp.log(1.0 + jnp.exp(-jnp.abs(z)))) * (1.0 / GLA_TAU)
    ri = lax.broadcasted_iota(jnp.int32, (rows, rows), 0)
    ci = lax.broadcasted_iota(jnp.int32, (rows, rows), 1)
    ordered = (ri <= ci) if reverse else (ri >= ci)
    tri_bd = jnp.logical_and(ri // c == ci // c, ordered).astype(BF16)
    g_head, g_rem = _split_bf16(g)
    bb = jnp.dot(tri_bd, jnp.concatenate([g_head, g_rem], axis=1), preferred_element_type=F32)
    b = bb[:, :hk] + bb[:, hk:]
    q_t = (q_ref[...].astype(F32) * jnp.exp(b)).astype(BF16)
    k = k_ref[...].astype(F32)
    k_t = (k * jnp.exp(-b)).astype(BF16)

    hrow = lax.broadcasted_iota(jnp.int32, (GLA_HEADS, hk), 0)
    kmask = (lax.broadcasted_iota(jnp.int32, (GLA_HEADS, hk), 1) // GLA_DK == hrow).astype(BF16)
    vmask = (lax.broadcasted_iota(jnp.int32, (GLA_HEADS, hv), 1) // GLA_DV
             == lax.broadcasted_iota(jnp.int32, (GLA_HEADS, hv), 0)).astype(BF16)
    qi = lax.broadcasted_iota(jnp.int32, (c, GLA_HEADS * c), 0)
    kj = lax.broadcasted_iota(jnp.int32, (c, GLA_HEADS * c), 1) % c
    causal = ((qi <= kj) if reverse else (qi >= kj)).astype(F32)
    st_mask = (lax.broadcasted_iota(jnp.int32, (hv, hk), 0) // GLA_DV
               == lax.broadcasted_iota(jnp.int32, (hv, hk), 1) // GLA_DK).astype(F32)

    order = range(nchunks - 1, -1, -1) if reverse else range(nchunks)
    st = st_ref[...]
    for j in order:
        sl = slice(j * c, (j + 1) * c)
        b_c = b[sl]
        b_last = b_c[0:1] if reverse else b_c[c - 1:c]
        v_c = v_ref[sl, :]
        k_bd = jnp.concatenate([k_t[sl] * kmask[h:h + 1] for h in range(GLA_HEADS)], axis=0)
        v_bd = jnp.concatenate([v_c * vmask[h:h + 1] for h in range(GLA_HEADS)], axis=0)
        att = lax.dot_general(q_t[sl], k_bd, (((1,), (1,)), ((), ())), preferred_element_type=F32) * causal
        o = jnp.dot(att.astype(BF16), v_bd, preferred_element_type=F32)
        o += lax.dot_general(q_t[sl], st.astype(BF16), (((1,), (1,)), ((), ())), preferred_element_type=F32)
        o_ref[sl, :] = o.astype(o_ref.dtype)
        k_end = (k[sl] * jnp.exp(b_last - b_c)).astype(BF16)
        kv_t = lax.dot_general(v_c, k_end, (((0,), (0,)), ((), ())), preferred_element_type=F32)
        st = jnp.exp(b_last) * st + kv_t * st_mask
    st_ref[...] = st


def _gla_kernel(qf_ref, kf_ref, vf_ref, af_ref, qr_ref, kr_ref, vr_ref, ar_ref, wa_ref, ba_ref,
                of_ref, or_ref, sf_ref, sr_ref):
    @pl.when(pl.program_id(1) == 0)
    def _():
        sf_ref[...] = jnp.zeros_like(sf_ref)
        sr_ref[...] = jnp.zeros_like(sr_ref)

    hk = GLA_HEADS * GLA_DK
    _gla_direction(qf_ref, kf_ref, vf_ref, af_ref, wa_ref[:, :hk], ba_ref[:, :hk], of_ref, sf_ref, False)
    _gla_direction(qr_ref, kr_ref, vr_ref, ar_ref, wa_ref[:, hk:], ba_ref[:, hk:], or_ref, sr_ref, True)


def gla_scan(gq, gk, gv, ga, wa, ba, *, nb, seq, ctx_len):
    n = gq.shape[0]
    blk = SEQ_BLOCK
    assert ctx_len == blk
    lpb = seq // blk
    ctx0 = nb * lpb

    def fwd(b, s):
        return (jnp.where(s == 0, ctx0 + b, b * lpb + s - 1), 0)

    def rev(b, s):
        return (jnp.where(s == 0, ctx0 + b, b * lpb + lpb - s), 0)

    hk = GLA_HEADS * GLA_DK
    hv = GLA_HEADS * GLA_DV
    specs = []
    for imap in (fwd, rev):
        specs += [pl.BlockSpec((blk, hk), imap), pl.BlockSpec((blk, hk), imap),
                  pl.BlockSpec((blk, hv), imap), pl.BlockSpec((blk, LANES), imap)]
    specs += [_resident(wa.shape), _resident(ba.shape)]
    return pl.pallas_call(
        _gla_kernel,
        grid=(nb, lpb + 1),
        in_specs=specs,
        out_specs=[pl.BlockSpec((blk, hv), fwd), pl.BlockSpec((blk, hv), rev)],
        out_shape=[jax.ShapeDtypeStruct((n, hv), BF16)] * 2,
        scratch_shapes=[pltpu.VMEM((hv, hk), F32), pltpu.VMEM((hv, hk), F32)],
        compiler_params=_cparams(("arbitrary", "arbitrary")),
        name="gla_scan",
    )(gq, gk, gv, ga, gq, gk, gv, ga, wa, ba)


NA_ROWS_PER_STEP = 4
NA_UNION_ROWS = NA_WIN_ROWS + NA_ROWS_PER_STEP - 1


def _na_kernel(q_ref, k_ref, v_ref, kc_ref, vc_ref, bias_ref, o_ref, *, n_rows):
    step = pl.program_id(1)
    hd = NA_HEADS * NA_HEAD_DIM
    lane = lax.broadcasted_iota(jnp.int32, (NA_HEADS, hd), 1)
    hrow = lax.broadcasted_iota(jnp.int32, (NA_HEADS, hd), 0)
    head_mask = (lane // NA_HEAD_DIM == hrow).astype(F32)
    kc = kc_ref[...]
    vc = vc_ref[...]
    last = pl.num_programs(1) - 1
    regime = jnp.where(step == 0, 0, jnp.where(step == last, 2, 1))
    ws = jnp.clip(step * NA_ROWS_PER_STEP - NA_WIN_ROWS // 2, 0, n_rows - NA_UNION_ROWS)
    start = pl.multiple_of(ws * GRID_W, GRID_W)
    kw = k_ref[pl.ds(start, NA_UNION_ROWS * GRID_W), :]
    vw = v_ref[pl.ds(start, NA_UNION_ROWS * GRID_W), :]
    q = q_ref[...].astype(F32)
    acc = jnp.zeros(q.shape, F32)
    for h in range(NA_HEADS):
        m_h = head_mask[h:h + 1]
        qh = (q * m_h).astype(BF16)
        s_lat = lax.dot_general(qh, kw, (((1,), (1,)), ((), ())), preferred_element_type=F32)
        s_lat = s_lat + bias_ref[regime, h]
        s_ctx = lax.dot_general(qh, kc, (((1,), (1,)), ((), ())), preferred_element_type=F32)
        m = jnp.maximum(jnp.max(s_lat, axis=1, keepdims=True), jnp.max(s_ctx, axis=1, keepdims=True))
        p_lat = jnp.exp(s_lat - m)
        p_ctx = jnp.exp(s_ctx - m)
        den = jnp.sum(p_lat, axis=1, keepdims=True) + jnp.sum(p_ctx, axis=1, keepdims=True)
        o = jnp.dot(p_lat.astype(BF16), vw, preferred_element_type=F32)
        o += jnp.dot(p_ctx.astype(BF16), vc, preferred_element_type=F32)
        acc += (o / den) * m_h
    o_ref[...] = acc.astype(o_ref.dtype)


def _na_ctx_kernel(q_ref, k_ref, v_ref, o_ref):
    hd = NA_HEADS * NA_HEAD_DIM
    lane = lax.broadcasted_iota(jnp.int32, (NA_HEADS, hd), 1)
    hrow = lax.broadcasted_iota(jnp.int32, (NA_HEADS, hd), 0)
    head_mask = (lane // NA_HEAD_DIM == hrow).astype(F32)
    q = q_ref[...].astype(F32)
    k = k_ref[...]
    v = v_ref[...]
    acc = jnp.zeros(q.shape, F32)
    for h in range(NA_HEADS):
        m_h = head_mask[h:h + 1]
        s = lax.dot_general((q * m_h).astype(BF16), k, (((1,), (1,)), ((), ())), preferred_element_type=F32)
        p = jnp.exp(s - jnp.max(s, axis=1, keepdims=True))
        o = jnp.dot(p.astype(BF16), v, preferred_element_type=F32) / jnp.sum(p, axis=1, keepdims=True)
        acc += o * m_h
    o_ref[...] = acc.astype(o_ref.dtype)


def neighbourhood_attention(nq, nk, nv, bias, *, nb, seq, ctx_len):
    n, hd = nq.shape
    n_rows = seq // GRID_W
    qb = NA_ROWS_PER_STEP * GRID_W
    steps = seq // qb
    ctx0 = nb * seq // ctx_len
    o_lat = pl.pallas_call(
        functools.partial(_na_kernel, n_rows=n_rows),
        grid=(nb, steps),
        in_specs=[pl.BlockSpec((qb, hd), lambda b, s: (b * steps + s, 0)),
                  pl.BlockSpec((seq, hd), lambda b, s: (b, 0)),
                  pl.BlockSpec((seq, hd), lambda b, s: (b, 0)),
                  pl.BlockSpec((ctx_len, hd), lambda b, s: (ctx0 + b, 0)),
                  pl.BlockSpec((ctx_len, hd), lambda b, s: (ctx0 + b, 0)),
                  _resident(bias.shape)],
        out_specs=pl.BlockSpec((qb, hd), lambda b, s: (b * steps + s, 0)),
        out_shape=jax.ShapeDtypeStruct((nb * seq, hd), BF16),
        compiler_params=_cparams(("arbitrary", "arbitrary")),
        name="na_latent",
    )(nq, nk, nv, nk, nv, bias)
    o_ctx = pl.pallas_call(
        _na_ctx_kernel,
        grid=(nb,),
        in_specs=[pl.BlockSpec((ctx_len, hd), lambda b: (ctx0 + b, 0))] * 3,
        out_specs=pl.BlockSpec((ctx_len, hd), lambda b: (b, 0)),
        out_shape=jax.ShapeDtypeStruct((nb * ctx_len, hd), BF16),
        compiler_params=_cparams(("arbitrary",)),
        name="na_context",
    )(nq, nk, nv)
    return jnp.concatenate([o_lat, o_ctx], axis=0)


def na_bias_tables(rpb):
    rpb = rpb.astype(F32)
    wr, nq, nu = NA_WIN_ROWS, NA_ROWS_PER_STEP, NA_UNION_ROWS
    assert nq <= wr // 2 + 1 and wr >= nq + wr // 2
    c_idx = jnp.arange(GRID_W)
    col_start = jnp.clip(c_idx - NA_WIN_COLS // 2, 0, GRID_W - NA_WIN_COLS)
    col_in = (c_idx[None, :] >= col_start[:, None]) & (c_idx[None, :] < col_start[:, None] + NA_WIN_COLS)
    wc = NA_WIN_COLS
    pad = jnp.zeros(rpb.shape[:-1] + (2 * GRID_W - (2 * wc - 1),), F32)
    table = jnp.concatenate([rpb[..., wc - 1:], pad, rpb[..., :wc - 1]], axis=-1)
    toe = jnp.tile(table, GRID_W)[..., :GRID_W * (2 * GRID_W - 1)]
    toe = toe.reshape(table.shape[:-1] + (GRID_W, 2 * GRID_W - 1))[..., :GRID_W]
    toe = jnp.transpose(jnp.where(col_in[None, None], toe, NEG_BIG), (0, 2, 1, 3))
    per_regime = []
    for lo_of, dr0_of in ((lambda i: 0, lambda i: wr - 1 - i),
                          (lambda i: i, lambda i: wr // 2 - 1),
                          (lambda i: nu - wr, lambda i: nu - wr - i)):
        rows = []
        for i in range(nq):
            lo, dr0 = lo_of(i), dr0_of(i)
            piece = toe[:, :, dr0:dr0 + wr]
            rows.append(jnp.pad(piece, ((0, 0), (0, 0), (lo, nu - wr - lo), (0, 0)), constant_values=NEG_BIG))
        per_regime.append(jnp.stack(rows, axis=1))
    b = jnp.stack(per_regime, axis=0)
    return b.reshape(3, NA_HEADS, nq * GRID_W, nu * GRID_W)


def _conv_kernel(prev_ref, main_ref, next_ref, dw_ref, dwb_ref, lng_ref, lnb_ref, o_ref, buf_ref, shift_ref,
                 *, n_lat_tiles, tiles_per_batch):
    i = pl.program_id(0)
    j = i % tiles_per_batch
    is_lat = i < n_lat_tiles
    has_prev = jnp.logical_and(is_lat, j > 0)
    has_next = jnp.logical_and(is_lat, j < tiles_per_batch - 1)
    cw = CONV_WIDTH
    tl = main_ref.shape[0]

    def glu(a):
        a = a.astype(F32)
        return a[:, :cw] * _sigmoid(a[:, cw:])

    buf_ref[0:CONV_HALO, :] = glu(prev_ref[...]) * has_prev.astype(F32)
    buf_ref[CONV_HALO:CONV_HALO + tl, :] = glu(main_ref[...])
    buf_ref[CONV_HALO + tl:, :] = glu(next_ref[...]) * has_next.astype(F32)
    dw = dw_ref[...]
    acc = jnp.zeros((tl, cw), F32) + dwb_ref[...]
    base = CONV_HALO - CONV_KERNEL // 2
    span = tl + 2 * CONV_HALO - SUBLANES
    for r in range(SUBLANES):
        shift_ref[r] = buf_ref[r:r + span, :]
    for k in range(CONV_KERNEL):
        q, r = divmod(base + k, SUBLANES)
        acc += shift_ref[r, q * SUBLANES:q * SUBLANES + tl, :] * dw[k:k + 1, :]
    mu = jnp.mean(acc, axis=-1, keepdims=True)
    xc = acc - mu
    y = xc * lax.rsqrt(jnp.mean(xc * xc, axis=-1, keepdims=True) + NORM_EPS)
    y = y * lng_ref[...] + lnb_ref[...]
    o_ref[...] = _silu(y).astype(o_ref.dtype)


def conv_branch(cv, dw, dw_b, ln_g, ln_b, *, nb, seq):
    n = cv.shape[0]
    tl = SEQ_BLOCK
    hb = tl // CONV_HALO
    n_tiles = n // tl
    cw = CONV_WIDTH
    return pl.pallas_call(
        functools.partial(_conv_kernel, n_lat_tiles=nb * seq // tl, tiles_per_batch=seq // tl),
        grid=(n_tiles,),
        in_specs=[pl.BlockSpec((CONV_HALO, 2 * cw), lambda i: (jnp.maximum(i * hb - 1, 0), 0)),
                  pl.BlockSpec((tl, 2 * cw), lambda i: (i, 0)),
                  pl.BlockSpec((CONV_HALO, 2 * cw), lambda i: (jnp.minimum((i + 1) * hb, n_tiles * hb - 1), 0)),
                  _resident((CONV_KERNEL, cw)), _resident((1, cw)), _resident((1, cw)), _resident((1, cw))],
        out_specs=pl.BlockSpec((tl, cw), lambda i: (i, 0)),
        out_shape=jax.ShapeDtypeStruct((n, cw), BF16),
        scratch_shapes=[pltpu.VMEM((tl + 2 * CONV_HALO, cw), F32),
                        pltpu.VMEM((SUBLANES, tl + 2 * CONV_HALO - SUBLANES, cw), F32)],
        compiler_params=_cparams(("arbitrary",)),
        name="conv_branch",
    )(cv, cv, cv, dw.astype(F32), dw_b.reshape(1, cw).astype(F32), ln_g.reshape(1, cw).astype(F32),
      ln_b.reshape(1, cw).astype(F32))


def _merge_kernel(xa_ref, xb_ref, mod_ref, gt_ref, ys5a_ref, ys5b_ref, of_ref, or_ref, gr_ref, na_ref, cv_ref,
                  gate_b_ref, wglu_ref, bglu_ref, ws5_ref, gng_ref, wgla_ref, wna_ref, wcv_ref, wmix_ref, o_ref,
                  *, n_lat_tiles):
    d = xa_ref.shape[1]
    z = jax.nn.gelu(jnp.concatenate([ys5a_ref[...], ys5b_ref[...]], axis=1))
    z = z * _sigmoid(jnp.dot(z.astype(BF16), wglu_ref[...], preferred_element_type=F32) + bglu_ref[...])
    br_s5 = jnp.dot(z.astype(BF16), ws5_ref[...], preferred_element_type=F32)
    o = of_ref[...].astype(F32) + or_ref[...].astype(F32)
    r = _silu(gr_ref[...].astype(F32))
    parts = []
    for h in range(GLA_HEADS):
        oh = o[:, h * GLA_DV:(h + 1) * GLA_DV]
        oh = oh * lax.rsqrt(jnp.mean(oh * oh, axis=-1, keepdims=True) + NORM_EPS) * gng_ref[...]
        parts.append(oh * r[:, h * GLA_DV:(h + 1) * GLA_DV])
    y_gla = jnp.concatenate(parts, axis=1).astype(BF16)
    br_gla = jnp.dot(y_gla, wgla_ref[...], preferred_element_type=F32)
    br_na = jnp.dot(na_ref[...], wna_ref[...], preferred_element_type=F32)
    br_cv = jnp.dot(cv_ref[...], wcv_ref[...], preferred_element_type=F32)
    merged = jnp.zeros((xa_ref.shape[0], d), F32)
    for i, br in enumerate((br_s5, br_gla, br_na, br_cv)):
        pre = gt_ref[:, i * d:(i + 1) * d] + gate_b_ref[:, i * d:(i + 1) * d].astype(BF16)
        gate = 0.5 * jnp.tanh(0.5 * pre).astype(F32) + 0.5
        merged += gate * br
    mix = jnp.dot(merged.astype(BF16), wmix_ref[...], preferred_element_type=F32)
    o_ref[...] = _x_pair_tile(xa_ref, xb_ref, n_lat_tiles) + mod_ref[:, 2 * d:3 * d] * mix


def merge_branches(x_lat, x_ctx, ctx_tile0, modtab, gt, ys5a, ys5b, o_f, o_r, gr, o_na, y_cv, weights,
                   *, n_rows, nb, seq, tm):
    d = x_lat.shape[1]
    n_lat_tiles = nb * seq // tm
    tpb = seq // tm

    def row(i):
        return (i, 0)

    acts = (gt, ys5a, ys5b, o_f, o_r, gr, o_na, y_cv)
    return pl.pallas_call(
        functools.partial(_merge_kernel, n_lat_tiles=n_lat_tiles),
        grid=(n_rows // tm,),
        in_specs=_x_pair_specs(tm, d, n_lat_tiles, ctx_tile0)
        + [_mod_row_spec(n_lat_tiles, tpb, nb, modtab.shape[-1])]
        + [pl.BlockSpec((tm, a.shape[1]), row) for a in acts]
        + [_resident(w.shape) for w in weights],
        out_specs=pl.BlockSpec((tm, d), row),
        out_shape=jax.ShapeDtypeStruct((n_rows, d), F32),
        compiler_params=_cparams(("arbitrary",)),
        name="merge_mix",
    )(x_lat, x_ctx, modtab, *acts, *weights)


MOE_PAIRS = MOE_EXPERTS_PER_GROUP * (MOE_EXPERTS_PER_GROUP - 1) // 2
MOE_CLASSES = MOE_GROUPS * MOE_PAIRS
MOE_TILE = 256
PAY_WORDS = 512
PAY_WIDTH = PAY_WORDS + LANES
HIGH_HALF = -65536
LOW_HALF = 65535


def _pack_bf16_pairs(a):
    w = a.shape[1] // 2
    hi = lax.bitcast_convert_type(a[:, :w].astype(BF16).astype(F32), jnp.int32)
    lo = lax.bitcast_convert_type(a[:, w:].astype(BF16).astype(F32), jnp.int32)
    return jnp.bitwise_or(jnp.bitwise_and(hi, HIGH_HALF), jnp.bitwise_and(jnp.right_shift(lo, 16), LOW_HALF))


def _unpack_bf16_pairs(words):
    hi = lax.bitcast_convert_type(jnp.bitwise_and(words, HIGH_HALF), F32)
    lo = lax.bitcast_convert_type(jnp.left_shift(words, 16), F32)
    return jnp.concatenate([hi, lo], axis=1)


def _route_pair_kernel(x_ref, mod_ref, g_ref, wr_ref, br_ref, ltri_ref, pay_ref, meta_ref, cnt_ref):
    d = x_ref.shape[1]
    mod = mod_ref[...]
    h = _modulated_norm(x_ref[...], g_ref[...], mod[:, 3 * d:4 * d], mod[:, 4 * d:5 * d])
    h_head, h_rem = _split_bf16(h)
    nl = br_ref.shape[1]
    both = jnp.dot(h_head, wr_ref[...], preferred_element_type=F32)
    logits = (both[:, :nl] + both[:, nl:] + jnp.dot(h_rem, wr_ref[:, :nl], preferred_element_type=F32)
              + br_ref[...])
    lane = lax.broadcasted_iota(jnp.int32, logits.shape, 1)
    big = jnp.int32(1 << 20)
    is_g = lane < MOE_GROUPS
    gl = jnp.where(is_g, logits, -jnp.inf)
    gmax = jnp.max(gl, axis=1, keepdims=True)
    gidx = jnp.min(jnp.where(gl == gmax, lane, big), axis=1, keepdims=True)
    group_p = 1.0 / jnp.sum(jnp.where(is_g, jnp.exp(logits - gmax), 0.0), axis=1, keepdims=True)
    first = MOE_GROUPS + gidx * MOE_EXPERTS_PER_GROUP
    in_group = jnp.logical_and(lane >= first, lane < first + MOE_EXPERTS_PER_GROUP)
    el = jnp.where(in_group, logits, -jnp.inf)
    v1 = jnp.max(el, axis=1, keepdims=True)
    i1 = jnp.min(jnp.where(el == v1, lane, big), axis=1, keepdims=True)
    el2 = jnp.where(lane == i1, -jnp.inf, el)
    v2 = jnp.max(el2, axis=1, keepdims=True)
    i2 = jnp.min(jnp.where(el2 == v2, lane, big), axis=1, keepdims=True)
    t = jnp.exp(v2 - v1)
    w1 = group_p / (1.0 + t)
    w2 = group_p * t / (1.0 + t)
    k_lo = jnp.minimum(i1, i2) - first
    k_hi = jnp.maximum(i1, i2) - first
    w_lo = jnp.where(i1 < i2, w1, w2)
    w_hi = jnp.where(i1 < i2, w2, w1)
    pair = k_lo * (MOE_EXPERTS_PER_GROUP - 1) - jnp.right_shift(k_lo * (k_lo - 1), 1) + k_hi - k_lo - 1
    cls = gidx * MOE_PAIRS + pair
    pay_ref[:, :PAY_WORDS] = _pack_bf16_pairs(h)
    weights = jnp.where(lane == 0, w_lo, jnp.where(lane == 1, w_hi, 0.0))
    pay_ref[:, PAY_WORDS:] = lax.bitcast_convert_type(weights, jnp.int32)
    onehot = (lane == cls).astype(BF16)
    before = jnp.dot(ltri_ref[...], onehot, preferred_element_type=F32)
    rank = jnp.sum(jnp.where(lane == cls, before, 0.0), axis=1, keepdims=True)
    packed = jnp.where(lane == 0, cls.astype(F32), jnp.where(lane == 1, rank, 0.0))
    meta_ref[0] = jnp.transpose(packed)[0:8, :]
    counts = jnp.sum(onehot.astype(F32), axis=0, keepdims=True)
    cnt_ref[0] = jnp.broadcast_to(counts, cnt_ref.shape[1:]).astype(jnp.int32)


def _row_copy(src_hbm, src_row, dst, dst_row, sem):
    return pltpu.make_async_copy(src_hbm.at[pl.ds(src_row, 1)], dst.at[pl.ds(dst_row, 1)], sem)


def _dispatch_kernel(pos_ref, pay_ref, zero_hbm, sorted_hbm, sem, *, rows):
    del zero_hbm
    tile = pl.program_id(0)

    def copy(i):
        return _row_copy(pay_ref, i, sorted_hbm, pos_ref[tile * rows + i], sem)

    def issue(i, carry):
        copy(i).start()
        return carry

    def drain(i, carry):
        copy(i).wait()
        return carry

    lax.fori_loop(0, rows, issue, 0, unroll=8)
    lax.fori_loop(0, rows, drain, 0, unroll=8)


def _pair_ffn_kernel(ea_ref, eb_ref, nt_ref, x_ref, w1a_ref, w3a_ref, w2a_ref, w1b_ref, w3b_ref, w2b_ref, y_ref):
    del ea_ref, eb_ref
    j = pl.program_id(0)

    @pl.when(j < nt_ref[0])
    def _():
        x = _unpack_bf16_pairs(x_ref[:, :PAY_WORDS]).astype(BF16)
        wts = lax.bitcast_convert_type(x_ref[:, PAY_WORDS:], F32)

        def ffn(w1_ref, w3_ref, w2_ref, cw):
            a = jnp.dot(x, w1_ref[...], preferred_element_type=F32)
            b = jnp.dot(x, w3_ref[...], preferred_element_type=F32)
            return jnp.dot((_silu(a) * b * cw).astype(BF16), w2_ref[...], preferred_element_type=F32)

        y = ffn(w1a_ref, w3a_ref, w2a_ref, wts[:, 0:1]) + ffn(w1b_ref, w3b_ref, w2b_ref, wts[:, 1:2])
        y_ref[...] = _pack_bf16_pairs(y)

    @pl.when(j >= nt_ref[0])
    def _():
        y_ref[...] = jnp.zeros_like(y_ref)


def _undispatch_kernel(pos_ref, y_hbm, x_ref, mod_ref, fg_ref, o_ref, ybuf, sem, *, final):
    tile = pl.program_id(0)
    rows, d = x_ref.shape

    def copy(i):
        return _row_copy(y_hbm, pos_ref[tile * rows + i], ybuf, i, sem)

    def issue(i, carry):
        copy(i).start()
        return carry

    def drain(i, carry):
        copy(i).wait()
        return carry

    lax.fori_loop(0, rows, issue, 0, unroll=8)
    lax.fori_loop(0, rows, drain, 0, unroll=8)
    y = x_ref[...] + mod_ref[:, 5 * d:6 * d] * _unpack_bf16_pairs(ybuf[...])
    if final:
        y = y * lax.rsqrt(jnp.mean(y * y, axis=-1, keepdims=True) + NORM_EPS) * fg_ref[...]
    o_ref[...] = y


def moe_dispatch_layer(xs, modtab, norm_g, w_router, b_router, w1, w3, w2, final_g,
                       *, layer, n_rows, nb, seq, tm, final):
    d = xs.shape[1]
    assert d == 2 * PAY_WORDS
    n_lat_tiles = nb * seq // tm
    tpb = seq // tm
    n_tiles = n_rows // tm
    ltri = (lax.broadcasted_iota(jnp.int32, (tm, tm), 0) > lax.broadcasted_iota(jnp.int32, (tm, tm), 1)).astype(BF16)
    mod_spec = _mod_row_spec(n_lat_tiles, tpb, nb, modtab.shape[-1])
    pay, meta, cnt = pl.pallas_call(
        _route_pair_kernel,
        grid=(n_tiles,),
        in_specs=[pl.BlockSpec((tm, d), lambda i: (i, 0)), mod_spec,
                  _resident((1, d)), _resident(w_router.shape), _resident(b_router.shape), _resident(ltri.shape)],
        out_specs=[pl.BlockSpec((tm, PAY_WIDTH), lambda i: (i, 0)),
                   pl.BlockSpec((1, 8, tm), lambda i: (i, 0, 0)), pl.BlockSpec((1, 8, LANES), lambda i: (i, 0, 0))],
        out_shape=[jax.ShapeDtypeStruct((n_rows, PAY_WIDTH), jnp.int32),
                   jax.ShapeDtypeStruct((n_tiles, 8, tm), F32), jax.ShapeDtypeStruct((n_tiles, 8, LANES), jnp.int32)],
        compiler_params=_cparams(("arbitrary",)),
        name="moe_route",
    )(xs, modtab, norm_g.reshape(1, d), w_router, b_router, ltri)

    max_tiles = n_rows // MOE_TILE + MOE_CLASSES
    cnt = cnt[:, 0, :]
    total = jnp.sum(cnt, axis=0)
    tiles_per_class = (total + MOE_TILE - 1) // MOE_TILE
    class_start = (jnp.cumsum(tiles_per_class) - tiles_per_class) * MOE_TILE
    base = class_start[None, :] + jnp.cumsum(cnt, axis=0) - cnt
    cls_tok = meta[:, 0, :].astype(jnp.int32)
    rank_tok = meta[:, 1, :].astype(jnp.int32)
    lanes = jnp.arange(LANES, dtype=jnp.int32)
    pos = jnp.sum(jnp.where(cls_tok[:, :, None] == lanes, base[:, None, :], 0), axis=-1) + rank_tok
    pos = pos.reshape(n_rows)
    tile_end = jnp.cumsum(tiles_per_class)
    tile_cls = jnp.sum((jnp.arange(max_tiles, dtype=jnp.int32)[:, None] >= tile_end[None, :MOE_CLASSES]), axis=1)
    tile_cls = jnp.minimum(tile_cls, MOE_CLASSES - 1)
    grp, pair = tile_cls // MOE_PAIRS, tile_cls % MOE_PAIRS
    pair_lo = [a for a in range(MOE_EXPERTS_PER_GROUP) for _ in range(a + 1, MOE_EXPERTS_PER_GROUP)]
    pair_hi = [b for a in range(MOE_EXPERTS_PER_GROUP) for b in range(a + 1, MOE_EXPERTS_PER_GROUP)]
    pid = jnp.arange(MOE_PAIRS, dtype=jnp.int32)
    k_lo = jnp.sum(jnp.where(pair[:, None] == pid, jnp.asarray(pair_lo, jnp.int32), 0), axis=1)
    k_hi = jnp.sum(jnp.where(pair[:, None] == pid, jnp.asarray(pair_hi, jnp.int32), 0), axis=1)
    ea = (grp * MOE_EXPERTS_PER_GROUP + k_lo).astype(jnp.int32)
    eb = (grp * MOE_EXPERTS_PER_GROUP + k_hi).astype(jnp.int32)
    n_used = tile_end[MOE_CLASSES - 1].reshape(1).astype(jnp.int32)

    n_slots = max_tiles * MOE_TILE
    hbm = pl.BlockSpec(memory_space=pl.ANY)
    sorted_pay = pl.pallas_call(
        functools.partial(_dispatch_kernel, rows=tm),
        grid_spec=pltpu.PrefetchScalarGridSpec(
            num_scalar_prefetch=1, grid=(n_tiles,),
            in_specs=[pl.BlockSpec((tm, PAY_WIDTH), lambda i, p: (i, 0)), hbm], out_specs=hbm,
            scratch_shapes=[pltpu.SemaphoreType.DMA(())]),
        out_shape=jax.ShapeDtypeStruct((n_slots, PAY_WIDTH), jnp.int32),
        input_output_aliases={2: 0},
        compiler_params=_cparams(("arbitrary",)),
        name="moe_dispatch",
    )(pos, pay, jnp.zeros((n_slots, PAY_WIDTH), jnp.int32))

    epg, hid = w1.shape[2], w1.shape[-1]

    def wspec(shape, which):
        def imap(j, ea_ref, eb_ref, nt_ref):
            e = (ea_ref, eb_ref)[which][j]
            return (layer, e // epg, e % epg, 0, 0)
        return pl.BlockSpec((None, None, None) + shape, imap)

    y_sorted = pl.pallas_call(
        _pair_ffn_kernel,
        grid_spec=pltpu.PrefetchScalarGridSpec(
            num_scalar_prefetch=3, grid=(max_tiles,),
            in_specs=[pl.BlockSpec((MOE_TILE, PAY_WIDTH), lambda j, *_: (j, 0)),
                      wspec((d, hid), 0), wspec((d, hid), 0), wspec((hid, d), 0),
                      wspec((d, hid), 1), wspec((d, hid), 1), wspec((hid, d), 1)],
            out_specs=pl.BlockSpec((MOE_TILE, PAY_WORDS), lambda j, *_: (j, 0))),
        out_shape=jax.ShapeDtypeStruct((n_slots, PAY_WORDS), jnp.int32),
        compiler_params=_cparams(("arbitrary",)),
        name="moe_pair_ffn",
    )(ea, eb, n_used, sorted_pay, w1, w3, w2, w1, w3, w2)

    return pl.pallas_call(
        functools.partial(_undispatch_kernel, final=final),
        grid_spec=pltpu.PrefetchScalarGridSpec(
            num_scalar_prefetch=1, grid=(n_tiles,),
            in_specs=[hbm,
                      pl.BlockSpec((tm, d), lambda i, p: (i, 0)),
                      pl.BlockSpec((None, 1, modtab.shape[-1]),
                                   lambda i, p: (jnp.where(i < n_lat_tiles, i // tpb, nb), 0, 0)),
                      pl.BlockSpec((1, d), lambda i, p: (0, 0))],
            out_specs=pl.BlockSpec((tm, d), lambda i, p: (i, 0)),
            scratch_shapes=[pltpu.VMEM((tm, PAY_WORDS), jnp.int32), pltpu.SemaphoreType.DMA(())]),
        out_shape=jax.ShapeDtypeStruct((n_rows, d), F32),
        compiler_params=_cparams(("arbitrary",)),
        name="moe_undispatch",
    )(pos, y_sorted, xs, modtab, final_g.reshape(1, d).astype(F32))


def rope_tables(seq, pad_rows):
    t = jnp.arange(seq, dtype=jnp.int32)
    row = (t // GRID_W).astype(F32)
    colp = (t % GRID_W).astype(F32)
    half = GLA_DK // 2
    inv_freq = ROPE_BASE ** (-jnp.arange(0, half, 2, dtype=F32) / half)
    ang_r = row[:, None] * inv_freq
    ang_c = colp[:, None] * inv_freq
    dd = jnp.arange(GLA_DK)
    ang = jnp.where((dd < half)[None, :], ang_r[:, dd % (half // 2)], ang_c[:, dd % (half // 2)])
    sign = jnp.where((dd % half) < half // 2, -1.0, 1.0).astype(F32)
    cos = jnp.tile(jnp.cos(ang), (1, GLA_HEADS))
    sin = jnp.tile(jnp.sin(ang) * sign[None, :], (1, GLA_HEADS))
    cos = jnp.concatenate([cos, jnp.ones((pad_rows, cos.shape[1]), F32)], axis=0)
    sin = jnp.concatenate([sin, jnp.zeros((pad_rows, sin.shape[1]), F32)], axis=0)
    return cos, sin


def split_in_weights(w_in, d):
    widths = (S5_WIDTH, GLA_HEADS * GLA_DK, GLA_HEADS * GLA_DK, GLA_HEADS * GLA_DV, GLA_HEADS * GLA_DV,
              2 * GLA_GATE_RANK, NA_HEADS * NA_HEAD_DIM, NA_HEADS * NA_HEAD_DIM, NA_HEADS * NA_HEAD_DIM,
              2 * CONV_WIDTH, N_BRANCHES * d)
    names = ('u', 'gq', 'gk', 'gv', 'gr', 'ga', 'nq', 'nk', 'nv', 'cv', 'gt')
    parts = {}
    col = 0
    for nme, w in zip(names, widths):
        parts[nme] = w_in[:, col:col + w]
        col += w
    swap = jnp.arange(GLA_HEADS * GLA_DK) ^ (GLA_DK // 4)
    gq = parts['gq'] * (GLA_DK ** -0.5)
    wqk = jnp.concatenate([gq, gq[:, swap], parts['gk'], parts['gk'][:, swap]], axis=1)
    ga = jnp.concatenate([parts['ga']] * 3 + [jnp.zeros((d, LANES - 6 * GLA_GATE_RANK), w_in.dtype)], axis=1)
    wmisc = jnp.concatenate([parts['u'], parts['nq'] * (NA_HEAD_DIM ** -0.5), parts['nk'], parts['nv'],
                             parts['gv'], parts['gr'], parts['cv'], ga], axis=1)
    return wqk.astype(BF16), wmisc.astype(BF16), parts['gt'].astype(BF16)


def kernel(x, c, ctx, c_ctx, norm1_g, norm2_g, w_mod, b_mod, w_in, gate_b, w_mix_out, s5_lam_re, s5_lam_im, s5_log_dt, s5_b_re, s5_b_im, s5_c_re, s5_c_im, s5_d, s5_w_glu, s5_b_glu, s5_w_out, gla_w_a2, gla_b_a, gla_norm_g, gla_w_out, na_rpb, na_w_out, conv_dw, conv_dw_b, conv_ln_g, conv_ln_b, conv_w_out, moe_w_group, moe_b_group, moe_w_expert, moe_b_expert, moe_w1, moe_w3, moe_w2, final_norm_g):
    nb, seq, d = x.shape
    ctx_len = ctx.shape[1]
    depth = w_mod.shape[0]
    n_lat = nb * seq
    tm = 512
    tm_moe = math.gcd(1024, nb * ctx_len)
    assert ctx_len == SEQ_BLOCK and seq % tm_moe == 0 and (nb * ctx_len) % tm_moe == 0 and nb < MOD_ROWS

    n_all = n_lat + nb * ctx_len
    x_lat, x_ctx, ctx_tile0 = x.reshape(n_lat, d).astype(F32), ctx.reshape(nb * ctx_len, d).astype(F32), 0
    c_rows = jnp.zeros((MOD_ROWS, d), F32).at[:nb].set(c.astype(F32)).at[nb].set(c_ctx.astype(F32))
    modtab = modulation_table(c_rows, w_mod.astype(F32), b_mod.astype(F32))
    modtab = modtab.reshape(depth, MOD_ROWS, 1, 6 * d)
    cos_tab, sin_tab = rope_tables(seq, tm)

    moe_w1_bf, moe_w3_bf, moe_w2_bf = moe_w1.astype(BF16), moe_w3.astype(BF16), moe_w2.astype(BF16)

    for i in range(depth):
        last = i == depth - 1
        n_rows = n_lat if last else n_all
        wqk, wmisc, wgate = split_in_weights(w_in[i], d)
        gq, gk, ua, ub, nq, nk, nv, gv, gr, cv, ga, gt = in_projection(
            x_lat, x_ctx, ctx_tile0, modtab[i], norm1_g[i].astype(F32), cos_tab, sin_tab, wqk, wmisc, wgate,
            n=n_all, nb=nb, seq=seq, tm=tm)

        mats = s5_matrices(s5_lam_re[i], s5_lam_im[i], s5_log_dt[i], s5_b_re[i], s5_b_im[i],
                           s5_c_re[i], s5_c_im[i], s5_d[i], nb)
        ys5a, ys5b = s5_mixer(ua, ub, mats, nb=nb, seq=seq, ctx_len=ctx_len)

        hk = GLA_HEADS * GLA_DK
        zero = jnp.zeros((GLA_GATE_RANK, hk), F32)
        wd = jnp.concatenate([jnp.concatenate([gla_w_a2[i, 0].astype(F32), zero], axis=1),
                              jnp.concatenate([zero, gla_w_a2[i, 1].astype(F32)], axis=1)], axis=0)
        wd_head, wd_rem = _split_bf16(wd)
        wa = jnp.concatenate([wd_head, wd_head, wd_rem,
                              jnp.zeros((LANES - 6 * GLA_GATE_RANK, 2 * hk), BF16)], axis=0)
        ba = gla_b_a[i].astype(F32).reshape(1, 2 * hk)
        o_f, o_r = gla_scan(gq, gk, gv, ga, wa, ba, nb=nb, seq=seq, ctx_len=ctx_len)

        o_na = neighbourhood_attention(nq, nk, nv, na_bias_tables(na_rpb[i]), nb=nb, seq=seq, ctx_len=ctx_len)
        y_cv = conv_branch(cv, conv_dw[i], conv_dw_b[i], conv_ln_g[i], conv_ln_b[i], nb=nb, seq=seq)

        weights = (gate_b[i].astype(F32).reshape(1, N_BRANCHES * d), s5_w_glu[i].astype(BF16),
                   s5_b_glu[i].astype(F32).reshape(1, S5_WIDTH), s5_w_out[i].astype(BF16),
                   gla_norm_g[i].astype(F32).reshape(1, GLA_DV), gla_w_out[i].astype(BF16),
                   na_w_out[i].astype(BF16), conv_w_out[i].astype(BF16), w_mix_out[i].astype(BF16))
        xs = merge_branches(x_lat, x_ctx, ctx_tile0, modtab[i], gt, ys5a, ys5b, o_f, o_r, gr, o_na, y_cv, weights,
                            n_rows=n_rows, nb=nb, seq=seq, tm=tm)

        n_router = MOE_GROUPS + N_EXPERTS
        w_router = jnp.pad(jnp.concatenate([moe_w_group[i], moe_w_expert[i]], axis=1).astype(F32),
                           ((0, 0), (0, LANES - n_router)))
        w_router = jnp.concatenate(_split_bf16(w_router), axis=1)
        b_router = jnp.pad(jnp.concatenate([moe_b_group[i], moe_b_expert[i]]).astype(F32),
                           (0, LANES - n_router)).reshape(1, LANES)
        xs = moe_dispatch_layer(xs, modtab[i], norm2_g[i].astype(F32), w_router, b_router,
                       moe_w1_bf, moe_w3_bf, moe_w2_bf, final_norm_g, layer=i,
                       n_rows=n_rows, nb=nb, seq=seq, tm=tm_moe, final=last)
        x_lat, x_ctx, ctx_tile0 = xs, xs, n_lat // tm

    return xs.reshape(nb, seq, d).astype(x.dtype)
```

```python
import functools
import math

import jax
import jax.numpy as jnp
from jax import lax
from jax.experimental import pallas as pl
from jax.experimental.pallas import tpu as pltpu

F32 = jnp.float32
BF16 = jnp.bfloat16
HIGHEST = lax.Precision.HIGHEST

GRID_W = 64
NORM_EPS = 1e-6
N_BRANCHES = 4
S5_WIDTH = 256
S5_GROUP_SIZE = 16
S5_GROUPS = 16
S5_STATE = 64
GLA_HEADS = 4
GLA_DK = 64
GLA_DV = 128
GLA_GATE_RANK = 16
GLA_TAU = 16.0
GLA_CHUNK = 64
ROPE_BASE = 10000.0
NA_HEADS = 4
NA_HEAD_DIM = 64
NA_WIN_ROWS = 8
NA_WIN_COLS = 16
CONV_WIDTH = 256
CONV_KERNEL = 31
MOE_GROUPS = 4
MOE_EXPERTS_PER_GROUP = 8
MOE_HIDDEN = 256
N_EXPERTS = MOE_GROUPS * MOE_EXPERTS_PER_GROUP

LANES = 128
SUBLANES = 8
MOD_ROWS = 8
VMEM_LIMIT = 56 * 1024 * 1024
S5_CHUNK = 32
SEQ_BLOCK = 256
CONV_HALO = 16
NEG_BIG = -1e30


def _cparams(sem):
    return pltpu.CompilerParams(dimension_semantics=sem, vmem_limit_bytes=VMEM_LIMIT)


def _resident(shape):
    nd = len(shape)
    return pl.BlockSpec(shape, lambda *_: (0,) * nd, pipeline_mode=pl.Buffered(1))


def _sigmoid(x):
    return 0.5 * jnp.tanh(0.5 * x) + 0.5


def _silu(x):
    return x * _sigmoid(x)


def _mod_kernel(c_ref, w_ref, b_ref, o_ref):
    c = c_ref[...]
    o_ref[0] = jnp.dot(_silu(c), w_ref[0], preferred_element_type=F32, precision=HIGHEST) + b_ref[0]


def modulation_table(c_rows, w_mod, b_mod):
    depth, d, n6 = w_mod.shape
    tn = 1024
    return pl.pallas_call(
        _mod_kernel,
        grid=(depth, n6 // tn),
        in_specs=[
            pl.BlockSpec((MOD_ROWS, d), lambda l, j: (0, 0)),
            pl.BlockSpec((1, d, tn), lambda l, j: (l, 0, j)),
            pl.BlockSpec((1, 1, tn), lambda l, j: (l, 0, j)),
        ],
        out_specs=pl.BlockSpec((1, MOD_ROWS, tn), lambda l, j: (l, 0, j)),
        out_shape=jax.ShapeDtypeStruct((depth, MOD_ROWS, n6), F32),
        compiler_params=_cparams(("arbitrary", "arbitrary")),
        name="mod_table",
    )(c_rows, w_mod, b_mod.reshape(depth, 1, n6))


def _mod_row_spec(n_lat_tiles, tiles_per_batch, nb, width):
    def imap(i, *_):
        return (jnp.where(i < n_lat_tiles, i // tiles_per_batch, nb), 0, 0)
    return pl.BlockSpec((None, 1, width), imap)


def _modulated_norm(x, g, shift, scale):
    y = x * lax.rsqrt(jnp.mean(x * x, axis=-1, keepdims=True) + NORM_EPS)
    return (y * g) * (1.0 + scale) + shift


IN_CHUNK = 512


def _x_pair_specs(tm, d, n_lat_tiles, ctx_tile0):
    return [pl.BlockSpec((tm, d), lambda i: (jnp.minimum(i, n_lat_tiles - 1), 0)),
            pl.BlockSpec((tm, d), lambda i: (ctx_tile0 + jnp.maximum(i - n_lat_tiles, 0), 0))]


def _x_pair_tile(xa_ref, xb_ref, n_lat_tiles):
    return jnp.where(pl.program_id(0) < n_lat_tiles, xa_ref[...], xb_ref[...])


def _inproj_kernel(xa_ref, xb_ref, mod_ref, g_ref, cos_ref, sin_ref, wqk_ref, wmisc_ref, wgate_ref,
                   gq_ref, gk_ref, ua_ref, ub_ref, nq_ref, nk_ref, nv_ref, gv_ref, gr_ref, cv_ref, ga_ref, gt_ref,
                   *, n_lat_tiles):
    d = xa_ref.shape[1]
    mod = mod_ref[...]
    x = _x_pair_tile(xa_ref, xb_ref, n_lat_tiles)
    h = _modulated_norm(x, g_ref[...], mod[:, 0:d], mod[:, d:2 * d]).astype(BF16)
    cos = cos_ref[...]
    sin = sin_ref[...]
    hk = cos.shape[1]
    shift = GLA_DK // 4
    lane = lax.broadcasted_iota(jnp.int32, (1, hk), 1)
    partner_above = (lane % (2 * shift)) < shift
    for j, o_ref in enumerate((gq_ref, gk_ref)):
        y = jnp.dot(h, wqk_ref[:, j * hk:(j + 1) * hk], preferred_element_type=F32)
        partner = jnp.where(partner_above, pltpu.roll(y, hk - shift, axis=1), pltpu.roll(y, shift, axis=1))
        o_ref[...] = (y * cos + partner * sin).astype(o_ref.dtype)
    col = 0
    for o_ref in (ua_ref, ub_ref, nq_ref, nk_ref, nv_ref, gv_ref, gr_ref, cv_ref, ga_ref):
        w = o_ref.shape[1]
        o_ref[...] = jnp.dot(h, wmisc_ref[:, col:col + w], preferred_element_type=F32).astype(o_ref.dtype)
        col += w
    for j in range(gt_ref.shape[1] // IN_CHUNK):
        sl = slice(j * IN_CHUNK, (j + 1) * IN_CHUNK)
        gt_ref[:, sl] = jnp.dot(h, wgate_ref[:, sl], preferred_element_type=F32).astype(gt_ref.dtype)


def in_projection(x_lat, x_ctx, ctx_tile0, modtab, norm_g, cos_tab, sin_tab, wqk, wmisc, wgate, *, n, nb, seq, tm):
    d = x_lat.shape[1]
    n_lat_tiles = nb * seq // tm
    tpb = seq // tm
    widths = (256, 256, LANES, LANES, 256, 256, 256, 512, 512, 512, LANES, N_BRANCHES * d)
    dtypes = (BF16, BF16, F32, F32) + (BF16,) * 6 + (F32, BF16)

    def row(i):
        return (i, 0)

    def rope_row(i):
        return (jnp.where(i < n_lat_tiles, i % tpb, tpb), 0)

    return pl.pallas_call(
        functools.partial(_inproj_kernel, n_lat_tiles=n_lat_tiles),
        grid=(n // tm,),
        in_specs=_x_pair_specs(tm, d, n_lat_tiles, ctx_tile0) + [
            _mod_row_spec(n_lat_tiles, tpb, nb, modtab.shape[-1]),
            _resident((1, d)),
            pl.BlockSpec((tm, 256), rope_row),
            pl.BlockSpec((tm, 256), rope_row),
            _resident(wqk.shape),
            _resident(wmisc.shape),
            _resident(wgate.shape),
        ],
        out_specs=[pl.BlockSpec((tm, w), row) for w in widths],
        out_shape=[jax.ShapeDtypeStruct((n, w), dt) for w, dt in zip(widths, dtypes)],
        compiler_params=_cparams(("arbitrary",)),
        name="in_proj",
    )(x_lat, x_ctx, modtab, norm_g.reshape(1, d), cos_tab, sin_tab, wqk, wmisc, wgate)


S5_TAUS_PER_TILE = LANES // S5_GROUP_SIZE
S5_PERM = S5_TAUS_PER_TILE * S5_WIDTH


def _s5_state_kernel(uca_ref, ucb_ref, ula_ref, ulb_ref, perm_ref, w_ref, x_ref, s_ref):
    n_ctx, n_lat = uca_ref.shape[0] // S5_CHUNK, ula_ref.shape[0] // S5_CHUNK
    for v in range(S5_CHUNK // S5_TAUS_PER_TILE):
        pieces = []
        for w in range(S5_TAUS_PER_TILE):
            tau = v * S5_TAUS_PER_TILE + w
            for uc_ref, ul_ref in ((uca_ref, ula_ref), (ucb_ref, ulb_ref)):
                pieces.append(jnp.concatenate([uc_ref[pl.ds(tau, n_ctx, stride=S5_CHUNK), :],
                                               ul_ref[pl.ds(tau, n_lat, stride=S5_CHUNK), :]], axis=0))
        z = jnp.concatenate(pieces, axis=1).astype(BF16)
        xv = jnp.dot(z, perm_ref[...], preferred_element_type=F32).astype(BF16)
        for g in range(S5_GROUPS):
            col = g * S5_CHUNK * S5_GROUP_SIZE + v * LANES
            x_ref[0, :, col:col + LANES] = xv[:, g * LANES:(g + 1) * LANES]
    cw = S5_CHUNK * S5_GROUP_SIZE
    for g in range(S5_GROUPS):
        s = jnp.dot(x_ref[0, :, g * cw:(g + 1) * cw], w_ref[g], preferred_element_type=F32)
        s_ref[0, 0, g] = s[:, :2 * S5_STATE]
        s_ref[1, 0, g] = s[:, 2 * S5_STATE:]


def _s5_scan_kernel(s_ref, a_ref, h_ref, *, n_chunks, n_ctx_chunks):
    d = pl.program_id(0)
    nc = n_chunks
    n_pairs = s_ref.shape[1] // nc
    a1 = a_ref[0, 0]
    a2 = a_ref[0, 1]

    def step(s, h):
        fwd_row = s
        rev_row = jnp.where(s < n_ctx_chunks, n_ctx_chunks - 1 - s, nc - 1 - (s - n_ctx_chunks))
        rows = pl.ds(jnp.where(d == 0, fwd_row, rev_row), n_pairs, stride=nc)
        h_ref[0, rows, :] = h
        return h * a1 + pltpu.roll(h, S5_STATE, axis=1) * a2 + s_ref[0, rows, :]

    lax.fori_loop(0, nc, step, jnp.zeros((n_pairs, s_ref.shape[2]), F32))


def _s5_out_kernel(x_ref, h_ref, perm_ref, t_ref, v_ref, yca_ref, ycb_ref, yla_ref, ylb_ref, y_scr):
    cw = S5_CHUNK * S5_GROUP_SIZE
    n_ctx = yca_ref.shape[0] // S5_CHUNK
    for g in range(S5_GROUPS):
        y = jnp.dot(x_ref[0, :, g * cw:(g + 1) * cw], t_ref[g], preferred_element_type=F32)
        h = jnp.concatenate([h_ref[0, 0, g], h_ref[1, 0, g]], axis=1)
        y += jnp.dot(h.astype(BF16), v_ref[g], preferred_element_type=F32)
        y_scr[:, g * cw:(g + 1) * cw] = y.astype(BF16)
    for v in range(S5_CHUNK // S5_TAUS_PER_TILE):
        yv = jnp.concatenate([y_scr[:, g * cw + v * LANES:g * cw + (v + 1) * LANES] for g in range(S5_GROUPS)],
                             axis=1)
        zv = lax.dot_general(yv, perm_ref[...], (((1,), (1,)), ((), ())), preferred_element_type=F32)
        for w in range(S5_TAUS_PER_TILE):
            tau = v * S5_TAUS_PER_TILE + w
            for hf, (yc_ref, yl_ref) in enumerate(((yca_ref, yla_ref), (ycb_ref, ylb_ref))):
                col = w * S5_WIDTH + hf * LANES
                piece = zv[:, col:col + LANES]
                yc_ref[pl.ds(tau, n_ctx, stride=S5_CHUNK), :] = piece[:n_ctx]
                yl_ref[pl.ds(tau, piece.shape[0] - n_ctx, stride=S5_CHUNK), :] = piece[n_ctx:]


def s5_mixer(ua, ub, mats, *, nb, seq, ctx_len):
    t_sum, w_cat, v_cat, a12 = mats
    g = S5_GROUPS
    nc = (seq + ctx_len) // S5_CHUNK
    n_ctx_chunks = ctx_len // S5_CHUNK
    cw = S5_CHUNK * S5_GROUP_SIZE
    sl = 2 * S5_STATE
    ctx0 = nb * seq // ctx_len
    src = (lax.broadcasted_iota(jnp.int32, (S5_PERM, S5_PERM), 0))
    dst = (lax.broadcasted_iota(jnp.int32, (S5_PERM, S5_PERM), 1))
    src_as_dst = ((src % S5_WIDTH) // S5_GROUP_SIZE) * LANES + (src // S5_WIDTH) * S5_GROUP_SIZE + src % S5_GROUP_SIZE
    perm = (src_as_dst == dst).astype(BF16)
    x_gm, s = pl.pallas_call(
        _s5_state_kernel,
        grid=(nb,),
        in_specs=[pl.BlockSpec((ctx_len, LANES), lambda b: (ctx0 + b, 0))] * 2
        + [pl.BlockSpec((seq, LANES), lambda b: (b, 0))] * 2
        + [_resident(perm.shape), _resident(w_cat.shape)],
        out_specs=[pl.BlockSpec((1, nc, g * cw), lambda b: (b, 0, 0)),
                   pl.BlockSpec((2, 1, g, nc, sl), lambda b: (0, b, 0, 0, 0))],
        out_shape=[jax.ShapeDtypeStruct((nb, nc, g * cw), BF16), jax.ShapeDtypeStruct((2, nb, g, nc, sl), F32)],
        compiler_params=_cparams(("arbitrary",)),
        name="s5_chunk_state",
    )(ua, ub, ua, ub, perm, w_cat)
    h = pl.pallas_call(
        functools.partial(_s5_scan_kernel, n_chunks=nc, n_ctx_chunks=n_ctx_chunks),
        grid=(2,),
        in_specs=[pl.BlockSpec((1, nb * g * nc, sl), lambda d: (d, 0, 0)),
                  pl.BlockSpec((1, 2, nb * g, sl), lambda d: (d, 0, 0, 0))],
        out_specs=pl.BlockSpec((1, nb * g * nc, sl), lambda d: (d, 0, 0)),
        out_shape=jax.ShapeDtypeStruct((2, nb * g * nc, sl), F32),
        compiler_params=_cparams(("arbitrary",)),
        name="s5_chunk_scan",
    )(s.reshape(2, nb * g * nc, sl), a12).reshape(2, nb, g, nc, sl)
    yca, ycb, yla, ylb = pl.pallas_call(
        _s5_out_kernel,
        grid=(nb,),
        in_specs=[pl.BlockSpec((1, nc, g * cw), lambda b: (b, 0, 0), pipeline_mode=pl.Buffered(1)),
                  pl.BlockSpec((2, 1, g, nc, sl), lambda b: (0, b, 0, 0, 0), pipeline_mode=pl.Buffered(1)),
                  _resident(perm.shape), _resident(t_sum.shape), _resident(v_cat.shape)],
        out_specs=[pl.BlockSpec((ctx_len, LANES), lambda b: (b, 0))] * 2
        + [pl.BlockSpec((seq, LANES), lambda b: (b, 0))] * 2,
        out_shape=[jax.ShapeDtypeStruct((nb * ctx_len, LANES), F32)] * 2
        + [jax.ShapeDtypeStruct((nb * seq, LANES), F32)] * 2,
        scratch_shapes=[pltpu.VMEM((nc, g * cw), BF16)],
        compiler_params=_cparams(("arbitrary",)),
        name="s5_readout",
    )(x_gm, h, perm, t_sum, v_cat)
    return (yla, yca), (ylb, ycb)


def s5_matrices(lam_re, lam_im, log_dt, b_re, b_im, c_re, c_im, d_skip, nb):
    ch = S5_CHUNK
    gsz = S5_GROUP_SIZE
    dt = jnp.exp(log_dt.astype(F32))[..., None]
    lr = lam_re.astype(F32)
    li = lam_im.astype(F32)

    def power(n):
        n = n.astype(F32)[:, None, None, None]
        mag = jnp.exp(lr * dt * n)
        return mag * jnp.cos(li * dt * n), mag * jnp.sin(li * dt * n)

    ab_re, ab_im = power(jnp.ones((1,), F32))
    ab_re, ab_im = ab_re[0], ab_im[0]
    den = lr * lr + li * li
    nr = ab_re - 1.0
    ni = ab_im
    coef_re = (nr * lr + ni * li) / den
    coef_im = (ni * lr - nr * li) / den
    br = b_re.astype(F32)
    bi = b_im.astype(F32)
    bb_re = coef_re[..., None] * br - coef_im[..., None] * bi
    bb_im = coef_re[..., None] * bi + coef_im[..., None] * br
    cr = c_re.astype(F32)
    ci = c_im.astype(F32)

    p_re, p_im = power(jnp.arange(ch + 1))
    ca_re = cr[None] * p_re[:, :, :, None, :] - ci[None] * p_im[:, :, :, None, :]
    ca_im = cr[None] * p_im[:, :, :, None, :] + ci[None] * p_re[:, :, :, None, :]
    kmat = jnp.einsum('ndgip,dgpj->ndgij', jnp.concatenate([ca_re[:ch], -ca_im[:ch]], axis=-1),
                      jnp.concatenate([bb_re, bb_im], axis=-2), precision=HIGHEST)
    lags = jnp.concatenate([jnp.flip(kmat[1:, 1], axis=0), kmat[:1, 0] + kmat[:1, 1], kmat[1:, 0]], axis=0)
    lag_rows = jnp.transpose(lags, (1, 3, 0, 2)).reshape(S5_GROUPS, gsz, (2 * ch - 1) * gsz)
    t_sum = jnp.concatenate([lag_rows[:, :, (ch - 1 - s) * gsz:(2 * ch - 1 - s) * gsz] for s in range(ch)], axis=1)
    skip = jnp.eye(ch * gsz, dtype=F32)[None] * jnp.tile(d_skip.astype(F32).reshape(S5_GROUPS, 1, gsz), (1, ch, 1)).reshape(S5_GROUPS, 1, ch * gsz)
    t_sum = t_sum + skip

    def w_dir(d, pr, pi):
        wr = pr[..., None] * bb_re[d][None] - pi[..., None] * bb_im[d][None]
        wi = pr[..., None] * bb_im[d][None] + pi[..., None] * bb_re[d][None]
        w = jnp.concatenate([wr, wi], axis=2)
        return jnp.transpose(w, (1, 0, 3, 2)).reshape(S5_GROUPS, ch * gsz, 2 * S5_STATE)
    w_cat = jnp.concatenate([w_dir(0, jnp.flip(p_re[:ch, 0], 0), jnp.flip(p_im[:ch, 0], 0)),
                             w_dir(1, p_re[:ch, 1], p_im[:ch, 1])], axis=-1)

    def v_dir(vr, vi):
        v = jnp.concatenate([vr, vi], axis=-1)
        return jnp.transpose(v, (1, 3, 0, 2)).reshape(S5_GROUPS, 2 * S5_STATE, ch * gsz)
    v_cat = jnp.concatenate([v_dir(ca_re[1:, 0], -ca_im[1:, 0]),
                             v_dir(jnp.flip(ca_re[1:, 1], 0), -jnp.flip(ca_im[1:, 1], 0))], axis=1)

    a1 = jnp.concatenate([p_re[ch], p_re[ch]], axis=-1)
    a2 = jnp.concatenate([-p_im[ch], p_im[ch]], axis=-1)
    a12 = jnp.stack([a1, a2], axis=1)
    a12 = jnp.tile(a12, (1, 1, nb, 1))
    return t_sum.astype(BF16), w_cat.astype(BF16), v_cat.astype(BF16), a12


def _split_bf16(x):
    head = x.astype(BF16)
    return head, (x - head.astype(F32)).astype(BF16)


def _gla_direction(q_ref, k_ref, v_ref, a_ref, wa, ba, o_ref, st_ref, reverse):
    c = GLA_CHUNK
    hk = GLA_HEADS * GLA_DK
    hv = GLA_HEADS * GLA_DV
    rows = q_ref.shape[0]
    nchunks = rows // c
    grank2 = 2 * GLA_GATE_RANK

    a_head, a_rem = _split_bf16(a_ref[...])
    lane = lax.broadcasted_iota(jnp.int32, a_head.shape, 1)
    a_pack = jnp.where(jnp.logical_and(lane >= grank2, lane < 2 * grank2), a_rem, a_head)
    z = jnp.dot(a_pack, wa, preferred_element_type=F32) + ba
    g = (jnp.minimum(z, 0.0) - jnp.log(1.0 + jnp.exp(-jnp.abs(z)))) * (1.0 / GLA_TAU)
    ri = lax.broadcasted_iota(jnp.int32, (rows, rows), 0)
    ci = lax.broadcasted_iota(jnp.int32, (rows, rows), 1)
    ordered = (ri <= ci) if reverse else (ri >= ci)
    tri_bd = jnp.logical_and(ri // c == ci // c, ordered).astype(BF16)
    g_head, g_rem = _split_bf16(g)
    bb = jnp.dot(tri_bd, jnp.concatenate([g_head, g_rem], axis=1), preferred_element_type=F32)
    b = bb[:, :hk] + bb[:, hk:]
    q_t = (q_ref[...].astype(F32) * jnp.exp(b)).astype(BF16)
    k = k_ref[...].astype(F32)
    k_t = (k * jnp.exp(-b)).astype(BF16)

    hrow = lax.broadcasted_iota(jnp.int32, (GLA_HEADS, hk), 0)
    kmask = (lax.broadcasted_iota(jnp.int32, (GLA_HEADS, hk), 1) // GLA_DK == hrow).astype(BF16)
    vmask = (lax.broadcasted_iota(jnp.int32, (GLA_HEADS, hv), 1) // GLA_DV
             == lax.broadcasted_iota(jnp.int32, (GLA_HEADS, hv), 0)).astype(BF16)
    qi = lax.broadcasted_iota(jnp.int32, (c, GLA_HEADS * c), 0)
    kj = lax.broadcasted_iota(jnp.int32, (c, GLA_HEADS * c), 1) % c
    causal = ((qi <= kj) if reverse else (qi >= kj)).astype(F32)
    st_mask = (lax.broadcasted_iota(jnp.int32, (hv, hk), 0) // GLA_DV
               == lax.broadcasted_iota(jnp.int32, (hv, hk), 1) // GLA_DK).astype(F32)

    order = range(nchunks - 1, -1, -1) if reverse else range(nchunks)
    st = st_ref[...]
    for j in order:
        sl = slice(j * c, (j + 1) * c)
        b_c = b[sl]
        b_last = b_c[0:1] if reverse else b_c[c - 1:c]
        v_c = v_ref[sl, :]
        k_bd = jnp.concatenate([k_t[sl] * kmask[h:h + 1] for h in range(GLA_HEADS)], axis=0)
        v_bd = jnp.concatenate([v_c * vmask[h:h + 1] for h in range(GLA_HEADS)], axis=0)
        att = lax.dot_general(q_t[sl], k_bd, (((1,), (1,)), ((), ())), preferred_element_type=F32) * causal
        o = jnp.dot(att.astype(BF16), v_bd, preferred_element_type=F32)
        o += lax.dot_general(q_t[sl], st.astype(BF16), (((1,), (1,)), ((), ())), preferred_element_type=F32)
        o_ref[sl, :] = o.astype(o_ref.dtype)
        k_end = (k[sl] * jnp.exp(b_last - b_c)).astype(BF16)
        kv_t = lax.dot_general(v_c, k_end, (((0,), (0,)), ((), ())), preferred_element_type=F32)
        st = jnp.exp(b_last) * st + kv_t * st_mask
    st_ref[...] = st


def _gla_kernel(qf_ref, kf_ref, vf_ref, af_ref, qr_ref, kr_ref, vr_ref, ar_ref, wa_ref, ba_ref,
                of_ref, or_ref, sf_ref, sr_ref):
    @pl.when(pl.program_id(1) == 0)
    def _():
        sf_ref[...] = jnp.zeros_like(sf_ref)
        sr_ref[...] = jnp.zeros_like(sr_ref)

    hk = GLA_HEADS * GLA_DK
    _gla_direction(qf_ref, kf_ref, vf_ref, af_ref, wa_ref[:, :hk], ba_ref[:, :hk], of_ref, sf_ref, False)
    _gla_direction(qr_ref, kr_ref, vr_ref, ar_ref, wa_ref[:, hk:], ba_ref[:, hk:], or_ref, sr_ref, True)


def gla_scan(gq, gk, gv, ga, wa, ba, *, nb, seq, ctx_len):
    n = gq.shape[0]
    blk = SEQ_BLOCK
    assert ctx_len == blk
    lpb = seq // blk
    ctx0 = nb * lpb

    def fwd(b, s):
        return (jnp.where(s == 0, ctx0 + b, b * lpb + s - 1), 0)

    def rev(b, s):
        return (jnp.where(s == 0, ctx0 + b, b * lpb + lpb - s), 0)

    hk = GLA_HEADS * GLA_DK
    hv = GLA_HEADS * GLA_DV
    specs = []
    for imap in (fwd, rev):
        specs += [pl.BlockSpec((blk, hk), imap), pl.BlockSpec((blk, hk), imap),
                  pl.BlockSpec((blk, hv), imap), pl.BlockSpec((blk, LANES), imap)]
    specs += [_resident(wa.shape), _resident(ba.shape)]
    return pl.pallas_call(
        _gla_kernel,
        grid=(nb, lpb + 1),
        in_specs=specs,
        out_specs=[pl.BlockSpec((blk, hv), fwd), pl.BlockSpec((blk, hv), rev)],
        out_shape=[jax.ShapeDtypeStruct((n, hv), BF16)] * 2,
        scratch_shapes=[pltpu.VMEM((hv, hk), F32), pltpu.VMEM((hv, hk), F32)],
        compiler_params=_cparams(("arbitrary", "arbitrary")),
        name="gla_scan",
    )(gq, gk, gv, ga, gq, gk, gv, ga, wa, ba)


NA_ROWS_PER_STEP = 4
NA_UNION_ROWS = NA_WIN_ROWS + NA_ROWS_PER_STEP - 1


def _na_kernel(q_ref, k_ref, v_ref, kc_ref, vc_ref, bias_ref, o_ref, *, n_rows):
    step = pl.program_id(1)
    hd = NA_HEADS * NA_HEAD_DIM
    lane = lax.broadcasted_iota(jnp.int32, (NA_HEADS, hd), 1)
    hrow = lax.broadcasted_iota(jnp.int32, (NA_HEADS, hd), 0)
    head_mask = (lane // NA_HEAD_DIM == hrow).astype(F32)
    kc = kc_ref[...]
    vc = vc_ref[...]
    last = pl.num_programs(1) - 1
    regime = jnp.where(step == 0, 0, jnp.where(step == last, 2, 1))
    ws = jnp.clip(step * NA_ROWS_PER_STEP - NA_WIN_ROWS // 2, 0, n_rows - NA_UNION_ROWS)
    start = pl.multiple_of(ws * GRID_W, GRID_W)
    kw = k_ref[pl.ds(start, NA_UNION_ROWS * GRID_W), :]
    vw = v_ref[pl.ds(start, NA_UNION_ROWS * GRID_W), :]
    q = q_ref[...].astype(F32)
    acc = jnp.zeros(q.shape, F32)
    for h in range(NA_HEADS):
        m_h = head_mask[h:h + 1]
        qh = (q * m_h).astype(BF16)
        s_lat = lax.dot_general(qh, kw, (((1,), (1,)), ((), ())), preferred_element_type=F32)
        s_lat = s_lat + bias_ref[regime, h]
        s_ctx = lax.dot_general(qh, kc, (((1,), (1,)), ((), ())), preferred_element_type=F32)
        m = jnp.maximum(jnp.max(s_lat, axis=1, keepdims=True), jnp.max(s_ctx, axis=1, keepdims=True))
        p_lat = jnp.exp(s_lat - m)
        p_ctx = jnp.exp(s_ctx - m)
        den = jnp.sum(p_lat, axis=1, keepdims=True) + jnp.sum(p_ctx, axis=1, keepdims=True)
        o = jnp.dot(p_lat.astype(BF16), vw, preferred_element_type=F32)
        o += jnp.dot(p_ctx.astype(BF16), vc, preferred_element_type=F32)
        acc += (o / den) * m_h
    o_ref[...] = acc.astype(o_ref.dtype)


def _na_ctx_kernel(q_ref, k_ref, v_ref, o_ref):
    hd = NA_HEADS * NA_HEAD_DIM
    lane = lax.broadcasted_iota(jnp.int32, (NA_HEADS, hd), 1)
    hrow = lax.broadcasted_iota(jnp.int32, (NA_HEADS, hd), 0)
    head_mask = (lane // NA_HEAD_DIM == hrow).astype(F32)
    q = q_ref[...].astype(F32)
    k = k_ref[...]
    v = v_ref[...]
    acc = jnp.zeros(q.shape, F32)
    for h in range(NA_HEADS):
        m_h = head_mask[h:h + 1]
        s = lax.dot_general((q * m_h).astype(BF16), k, (((1,), (1,)), ((), ())), preferred_element_type=F32)
        p = jnp.exp(s - jnp.max(s, axis=1, keepdims=True))
        o = jnp.dot(p.astype(BF16), v, preferred_element_type=F32) / jnp.sum(p, axis=1, keepdims=True)
        acc += o * m_h
    o_ref[...] = acc.astype(o_ref.dtype)


def neighbourhood_attention(nq, nk, nv, bias, *, nb, seq, ctx_len):
    n, hd = nq.shape
    n_rows = seq // GRID_W
    qb = NA_ROWS_PER_STEP * GRID_W
    steps = seq // qb
    ctx0 = nb * seq // ctx_len
    o_lat = pl.pallas_call(
        functools.partial(_na_kernel, n_rows=n_rows),
        grid=(nb, steps),
        in_specs=[pl.BlockSpec((qb, hd), lambda b, s: (b * steps + s, 0)),
                  pl.BlockSpec((seq, hd), lambda b, s: (b, 0)),
                  pl.BlockSpec((seq, hd), lambda b, s: (b, 0)),
                  pl.BlockSpec((ctx_len, hd), lambda b, s: (ctx0 + b, 0)),
                  pl.BlockSpec((ctx_len, hd), lambda b, s: (ctx0 + b, 0)),
                  _resident(bias.shape)],
        out_specs=pl.BlockSpec((qb, hd), lambda b, s: (b * steps + s, 0)),
        out_shape=jax.ShapeDtypeStruct((nb * seq, hd), BF16),
        compiler_params=_cparams(("arbitrary", "arbitrary")),
        name="na_latent",
    )(nq, nk, nv, nk, nv, bias)
    o_ctx = pl.pallas_call(
        _na_ctx_kernel,
        grid=(nb,),
        in_specs=[pl.BlockSpec((ctx_len, hd), lambda b: (ctx0 + b, 0))] * 3,
        out_specs=pl.BlockSpec((ctx_len, hd), lambda b: (b, 0)),
        out_shape=jax.ShapeDtypeStruct((nb * ctx_len, hd), BF16),
        compiler_params=_cparams(("arbitrary",)),
        name="na_context",
    )(nq, nk, nv)
    return o_lat, o_ctx


def na_bias_tables(rpb):
    rpb = rpb.astype(F32)
    wr, nq, nu = NA_WIN_ROWS, NA_ROWS_PER_STEP, NA_UNION_ROWS
    assert nq <= wr // 2 + 1 and wr >= nq + wr // 2
    c_idx = jnp.arange(GRID_W)
    col_start = jnp.clip(c_idx - NA_WIN_COLS // 2, 0, GRID_W - NA_WIN_COLS)
    col_in = (c_idx[None, :] >= col_start[:, None]) & (c_idx[None, :] < col_start[:, None] + NA_WIN_COLS)
    wc = NA_WIN_COLS
    pad = jnp.zeros(rpb.shape[:-1] + (2 * GRID_W - (2 * wc - 1),), F32)
    table = jnp.concatenate([rpb[..., wc - 1:], pad, rpb[..., :wc - 1]], axis=-1)
    toe = jnp.tile(table, GRID_W)[..., :GRID_W * (2 * GRID_W - 1)]
    toe = toe.reshape(table.shape[:-1] + (GRID_W, 2 * GRID_W - 1))[..., :GRID_W]
    toe = jnp.transpose(jnp.where(col_in[None, None], toe, NEG_BIG), (0, 2, 1, 3))
    per_regime = []
    for lo_of, dr0_of in ((lambda i: 0, lambda i: wr - 1 - i),
                          (lambda i: i, lambda i: wr // 2 - 1),
                          (lambda i: nu - wr, lambda i: nu - wr - i)):
        rows = []
        for i in range(nq):
            lo, dr0 = lo_of(i), dr0_of(i)
            piece = toe[:, :, dr0:dr0 + wr]
            rows.append(jnp.pad(piece, ((0, 0), (0, 0), (lo, nu - wr - lo), (0, 0)), constant_values=NEG_BIG))
        per_regime.append(jnp.stack(rows, axis=1))
    b = jnp.stack(per_regime, axis=0)
    return b.reshape(3, NA_HEADS, nq * GRID_W, nu * GRID_W)


def _conv_kernel(prev_ref, main_ref, next_ref, dw_ref, dwb_ref, lng_ref, lnb_ref, o_ref, buf_ref, shift_ref,
                 *, n_lat_tiles, tiles_per_batch):
    i = pl.program_id(0)
    j = i % tiles_per_batch
    is_lat = i < n_lat_tiles
    has_prev = jnp.logical_and(is_lat, j > 0)
    has_next = jnp.logical_and(is_lat, j < tiles_per_batch - 1)
    cw = CONV_WIDTH
    tl = main_ref.shape[0]

    def glu(a):
        a = a.astype(F32)
        return a[:, :cw] * _sigmoid(a[:, cw:])

    buf_ref[0:CONV_HALO, :] = glu(prev_ref[...]) * has_prev.astype(F32)
    buf_ref[CONV_HALO:CONV_HALO + tl, :] = glu(main_ref[...])
    buf_ref[CONV_HALO + tl:, :] = glu(next_ref[...]) * has_next.astype(F32)
    dw = dw_ref[...]
    acc = jnp.zeros((tl, cw), F32) + dwb_ref[...]
    base = CONV_HALO - CONV_KERNEL // 2
    span = tl + 2 * CONV_HALO - SUBLANES
    for r in range(SUBLANES):
        shift_ref[r] = buf_ref[r:r + span, :]
    for k in range(CONV_KERNEL):
        q, r = divmod(base + k, SUBLANES)
        acc += shift_ref[r, q * SUBLANES:q * SUBLANES + tl, :] * dw[k:k + 1, :]
    mu = jnp.mean(acc, axis=-1, keepdims=True)
    xc = acc - mu
    y = xc * lax.rsqrt(jnp.mean(xc * xc, axis=-1, keepdims=True) + NORM_EPS)
    y = y * lng_ref[...] + lnb_ref[...]
    o_ref[...] = _silu(y).astype(o_ref.dtype)


def conv_branch(cv, dw, dw_b, ln_g, ln_b, *, nb, seq):
    n = cv.shape[0]
    tl = SEQ_BLOCK
    hb = tl // CONV_HALO
    n_tiles = n // tl
    cw = CONV_WIDTH
    return pl.pallas_call(
        functools.partial(_conv_kernel, n_lat_tiles=nb * seq // tl, tiles_per_batch=seq // tl),
        grid=(n_tiles,),
        in_specs=[pl.BlockSpec((CONV_HALO, 2 * cw), lambda i: (jnp.maximum(i * hb - 1, 0), 0)),
                  pl.BlockSpec((tl, 2 * cw), lambda i: (i, 0)),
                  pl.BlockSpec((CONV_HALO, 2 * cw), lambda i: (jnp.minimum((i + 1) * hb, n_tiles * hb - 1), 0)),
                  _resident((CONV_KERNEL, cw)), _resident((1, cw)), _resident((1, cw)), _resident((1, cw))],
        out_specs=pl.BlockSpec((tl, cw), lambda i: (i, 0)),
        out_shape=jax.ShapeDtypeStruct((n, cw), BF16),
        scratch_shapes=[pltpu.VMEM((tl + 2 * CONV_HALO, cw), F32),
                        pltpu.VMEM((SUBLANES, tl + 2 * CONV_HALO - SUBLANES, cw), F32)],
        compiler_params=_cparams(("arbitrary",)),
        name="conv_branch",
    )(cv, cv, cv, dw.astype(F32), dw_b.reshape(1, cw).astype(F32), ln_g.reshape(1, cw).astype(F32),
      ln_b.reshape(1, cw).astype(F32))


def _merge_kernel(xa_ref, xb_ref, ys5a_l_ref, ys5a_c_ref, ys5b_l_ref, ys5b_c_ref, na_l_ref, na_c_ref,
                  mod_ref, gt_ref, of_ref, or_ref, gr_ref, cv_ref,
                  gate_b_ref, wglu_ref, bglu_ref, ws5_ref, gng_ref, wgla_ref, wna_ref, wcv_ref, wmix_ref, o_ref,
                  *, n_lat_tiles):
    d = xa_ref.shape[1]
    ys5a = _x_pair_tile(ys5a_l_ref, ys5a_c_ref, n_lat_tiles)
    ys5b = _x_pair_tile(ys5b_l_ref, ys5b_c_ref, n_lat_tiles)
    na = _x_pair_tile(na_l_ref, na_c_ref, n_lat_tiles)
    z = jax.nn.gelu(jnp.concatenate([ys5a, ys5b], axis=1))
    z = z * _sigmoid(jnp.dot(z.astype(BF16), wglu_ref[...], preferred_element_type=F32) + bglu_ref[...])
    br_s5 = jnp.dot(z.astype(BF16), ws5_ref[...], preferred_element_type=F32)
    o = of_ref[...].astype(F32) + or_ref[...].astype(F32)
    r = _silu(gr_ref[...].astype(F32))
    parts = []
    for h in range(GLA_HEADS):
        oh = o[:, h * GLA_DV:(h + 1) * GLA_DV]
        oh = oh * lax.rsqrt(jnp.mean(oh * oh, axis=-1, keepdims=True) + NORM_EPS) * gng_ref[...]
        parts.append(oh * r[:, h * GLA_DV:(h + 1) * GLA_DV])
    y_gla = jnp.concatenate(parts, axis=1).astype(BF16)
    br_gla = jnp.dot(y_gla, wgla_ref[...], preferred_element_type=F32)
    br_na = jnp.dot(na, wna_ref[...], preferred_element_type=F32)
    br_cv = jnp.dot(cv_ref[...], wcv_ref[...], preferred_element_type=F32)
    merged = jnp.zeros((xa_ref.shape[0], d), F32)
    for i, br in enumerate((br_s5, br_gla, br_na, br_cv)):
        pre = gt_ref[:, i * d:(i + 1) * d] + gate_b_ref[:, i * d:(i + 1) * d].astype(BF16)
        gate = 0.5 * jnp.tanh(0.5 * pre).astype(F32) + 0.5
        merged += gate * br
    mix = jnp.dot(merged.astype(BF16), wmix_ref[...], preferred_element_type=F32)
    o_ref[...] = _x_pair_tile(xa_ref, xb_ref, n_lat_tiles) + mod_ref[:, 2 * d:3 * d] * mix


def merge_branches(x_lat, x_ctx, ctx_tile0, modtab, gt, ys5a, ys5b, o_f, o_r, gr, o_na, y_cv, weights,
                   *, n_rows, nb, seq, tm):
    d = x_lat.shape[1]
    n_lat_tiles = nb * seq // tm
    tpb = seq // tm

    def row(i):
        return (i, 0)

    pairs = (ys5a, ys5b, o_na)
    acts = (gt, o_f, o_r, gr, y_cv)
    pair_specs = []
    for lat, _ in pairs:
        pair_specs += _x_pair_specs(tm, lat.shape[1], n_lat_tiles, 0)
    return pl.pallas_call(
        functools.partial(_merge_kernel, n_lat_tiles=n_lat_tiles),
        grid=(n_rows // tm,),
        in_specs=_x_pair_specs(tm, d, n_lat_tiles, ctx_tile0) + pair_specs
        + [_mod_row_spec(n_lat_tiles, tpb, nb, modtab.shape[-1])]
        + [pl.BlockSpec((tm, a.shape[1]), row) for a in acts]
        + [_resident(w.shape) for w in weights],
        out_specs=pl.BlockSpec((tm, d), row),
        out_shape=jax.ShapeDtypeStruct((n_rows, d), F32),
        compiler_params=_cparams(("arbitrary",)),
        name="merge_mix",
    )(x_lat, x_ctx, *[a for p in pairs for a in p], modtab, *acts, *weights)


MOE_PAIRS = MOE_EXPERTS_PER_GROUP * (MOE_EXPERTS_PER_GROUP - 1) // 2
MOE_CLASSES = MOE_GROUPS * MOE_PAIRS
MOE_TILE = 256
PAY_WORDS = 512
PAY_WIDTH = PAY_WORDS + LANES
HIGH_HALF = -65536
LOW_HALF = 65535


def _pack_bf16_pairs(a):
    w = a.shape[1] // 2
    hi = lax.bitcast_convert_type(a[:, :w].astype(BF16).astype(F32), jnp.int32)
    lo = lax.bitcast_convert_type(a[:, w:].astype(BF16).astype(F32), jnp.int32)
    return jnp.bitwise_or(jnp.bitwise_and(hi, HIGH_HALF), jnp.bitwise_and(jnp.right_shift(lo, 16), LOW_HALF))


def _unpack_bf16_pairs(words):
    hi = lax.bitcast_convert_type(jnp.bitwise_and(words, HIGH_HALF), F32)
    lo = lax.bitcast_convert_type(jnp.left_shift(words, 16), F32)
    return jnp.concatenate([hi, lo], axis=1)


def _route_pair_kernel(x_ref, mod_ref, g_ref, wr_ref, br_ref, ltri_ref, pay_ref, meta_ref, cnt_ref):
    d = x_ref.shape[1]
    mod = mod_ref[...]
    h = _modulated_norm(x_ref[...], g_ref[...], mod[:, 3 * d:4 * d], mod[:, 4 * d:5 * d])
    h_head, h_rem = _split_bf16(h)
    nl = br_ref.shape[1]
    both = jnp.dot(h_head, wr_ref[...], preferred_element_type=F32)
    logits = (both[:, :nl] + both[:, nl:] + jnp.dot(h_rem, wr_ref[:, :nl], preferred_element_type=F32)
              + br_ref[...])
    lane = lax.broadcasted_iota(jnp.int32, logits.shape, 1)
    big = jnp.int32(1 << 20)
    is_g = lane < MOE_GROUPS
    gl = jnp.where(is_g, logits, -jnp.inf)
    gmax = jnp.max(gl, axis=1, keepdims=True)
    gidx = jnp.min(jnp.where(gl == gmax, lane, big), axis=1, keepdims=True)
    group_p = 1.0 / jnp.sum(jnp.where(is_g, jnp.exp(logits - gmax), 0.0), axis=1, keepdims=True)
    first = MOE_GROUPS + gidx * MOE_EXPERTS_PER_GROUP
    in_group = jnp.logical_and(lane >= first, lane < first + MOE_EXPERTS_PER_GROUP)
    el = jnp.where(in_group, logits, -jnp.inf)
    v1 = jnp.max(el, axis=1, keepdims=True)
    i1 = jnp.min(jnp.where(el == v1, lane, big), axis=1, keepdims=True)
    el2 = jnp.where(lane == i1, -jnp.inf, el)
    v2 = jnp.max(el2, axis=1, keepdims=True)
    i2 = jnp.min(jnp.where(el2 == v2, lane, big), axis=1, keepdims=True)
    t = jnp.exp(v2 - v1)
    w1 = group_p / (1.0 + t)
    w2 = group_p * t / (1.0 + t)
    k_lo = jnp.minimum(i1, i2) - first
    k_hi = jnp.maximum(i1, i2) - first
    w_lo = jnp.where(i1 < i2, w1, w2)
    w_hi = jnp.where(i1 < i2, w2, w1)
    pair = k_lo * (MOE_EXPERTS_PER_GROUP - 1) - jnp.right_shift(k_lo * (k_lo - 1), 1) + k_hi - k_lo - 1
    cls = gidx * MOE_PAIRS + pair
    pay_ref[:, :PAY_WORDS] = _pack_bf16_pairs(h)
    weights = jnp.where(lane == 0, w_lo, jnp.where(lane == 1, w_hi, 0.0))
    pay_ref[:, PAY_WORDS:] = lax.bitcast_convert_type(weights, jnp.int32)
    onehot = (lane == cls).astype(BF16)
    before = jnp.dot(ltri_ref[...], onehot, preferred_element_type=F32)
    rank = jnp.sum(jnp.where(lane == cls, before, 0.0), axis=1, keepdims=True)
    packed = jnp.where(lane == 0, cls.astype(F32), jnp.where(lane == 1, rank, 0.0))
    meta_ref[0] = jnp.transpose(packed)[0:8, :]
    counts = jnp.sum(onehot.astype(F32), axis=0, keepdims=True)
    cnt_ref[0] = jnp.broadcast_to(counts, cnt_ref.shape[1:]).astype(jnp.int32)


def _row_copy(src_hbm, src_row, dst, dst_row, sem):
    return pltpu.make_async_copy(src_hbm.at[pl.ds(src_row, 1)], dst.at[pl.ds(dst_row, 1)], sem)


def _dispatch_kernel(pos_ref, pay_ref, zero_hbm, sorted_hbm, sem, *, rows):
    del zero_hbm
    tile = pl.program_id(0)

    def copy(i):
        return _row_copy(pay_ref, i, sorted_hbm, pos_ref[tile * rows + i], sem)

    def issue(i, carry):
        copy(i).start()
        return carry

    def drain(i, carry):
        copy(i).wait()
        return carry

    lax.fori_loop(0, rows, issue, 0, unroll=8)
    lax.fori_loop(0, rows, drain, 0, unroll=8)


def _pair_ffn_kernel(ea_ref, eb_ref, nt_ref, x_ref, w1a_ref, w3a_ref, w2a_ref, w1b_ref, w3b_ref, w2b_ref, y_ref):
    del ea_ref, eb_ref
    j = pl.program_id(0)

    @pl.when(j < nt_ref[0])
    def _():
        x = _unpack_bf16_pairs(x_ref[:, :PAY_WORDS]).astype(BF16)
        wts = lax.bitcast_convert_type(x_ref[:, PAY_WORDS:], F32)

        def ffn(w1_ref, w3_ref, w2_ref, cw):
            a = jnp.dot(x, w1_ref[...], preferred_element_type=F32)
            b = jnp.dot(x, w3_ref[...], preferred_element_type=F32)
            return jnp.dot((_silu(a) * b * cw).astype(BF16), w2_ref[...], preferred_element_type=F32)

        y = ffn(w1a_ref, w3a_ref, w2a_ref, wts[:, 0:1]) + ffn(w1b_ref, w3b_ref, w2b_ref, wts[:, 1:2])
        y_ref[...] = _pack_bf16_pairs(y)

    @pl.when(j >= nt_ref[0])
    def _():
        y_ref[...] = jnp.zeros_like(y_ref)


def _undispatch_kernel(pos_ref, y_hbm, x_ref, mod_ref, fg_ref, o_ref, ybuf, sem, *, final):
    tile = pl.program_id(0)
    rows, d = x_ref.shape

    def copy(i):
        return _row_copy(y_hbm, pos_ref[tile * rows + i], ybuf, i, sem)

    def issue(i, carry):
        copy(i).start()
        return carry

    def drain(i, carry):
        copy(i).wait()
        return carry

    lax.fori_loop(0, rows, issue, 0, unroll=8)
    lax.fori_loop(0, rows, drain, 0, unroll=8)
    y = x_ref[...] + mod_ref[:, 5 * d:6 * d] * _unpack_bf16_pairs(ybuf[...])
    if final:
        y = y * lax.rsqrt(jnp.mean(y * y, axis=-1, keepdims=True) + NORM_EPS) * fg_ref[...]
    o_ref[...] = y


def moe_dispatch_layer(xs, modtab, norm_g, w_router, b_router, w1, w3, w2, final_g,
                       *, layer, n_rows, nb, seq, tm, final):
    d = xs.shape[1]
    assert d == 2 * PAY_WORDS
    n_lat_tiles = nb * seq // tm
    tpb = seq // tm
    n_tiles = n_rows // tm
    ltri = (lax.broadcasted_iota(jnp.int32, (tm, tm), 0) > lax.broadcasted_iota(jnp.int32, (tm, tm), 1)).astype(BF16)
    mod_spec = _mod_row_spec(n_lat_tiles, tpb, nb, modtab.shape[-1])
    pay, meta, cnt = pl.pallas_call(
        _route_pair_kernel,
        grid=(n_tiles,),
        in_specs=[pl.BlockSpec((tm, d), lambda i: (i, 0)), mod_spec,
                  _resident((1, d)), _resident(w_router.shape), _resident(b_router.shape), _resident(ltri.shape)],
        out_specs=[pl.BlockSpec((tm, PAY_WIDTH), lambda i: (i, 0)),
                   pl.BlockSpec((1, 8, tm), lambda i: (i, 0, 0)), pl.BlockSpec((1, 8, LANES), lambda i: (i, 0, 0))],
        out_shape=[jax.ShapeDtypeStruct((n_rows, PAY_WIDTH), jnp.int32),
                   jax.ShapeDtypeStruct((n_tiles, 8, tm), F32), jax.ShapeDtypeStruct((n_tiles, 8, LANES), jnp.int32)],
        compiler_params=_cparams(("arbitrary",)),
        name="moe_route",
    )(xs, modtab, norm_g.reshape(1, d), w_router, b_router, ltri)

    max_tiles = n_rows // MOE_TILE + MOE_CLASSES
    cnt = cnt[:, 0, :]
    total = jnp.sum(cnt, axis=0)
    tiles_per_class = (total + MOE_TILE - 1) // MOE_TILE
    class_start = (jnp.cumsum(tiles_per_class) - tiles_per_class) * MOE_TILE
    base = class_start[None, :] + jnp.cumsum(cnt, axis=0) - cnt
    cls_tok = meta[:, 0, :].astype(jnp.int32)
    rank_tok = meta[:, 1, :].astype(jnp.int32)
    lanes = jnp.arange(LANES, dtype=jnp.int32)
    pos = jnp.sum(jnp.where(cls_tok[:, :, None] == lanes, base[:, None, :], 0), axis=-1) + rank_tok
    pos = pos.reshape(n_rows)
    tile_end = jnp.cumsum(tiles_per_class)
    tile_cls = jnp.sum((jnp.arange(max_tiles, dtype=jnp.int32)[:, None] >= tile_end[None, :MOE_CLASSES]), axis=1)
    tile_cls = jnp.minimum(tile_cls, MOE_CLASSES - 1)
    grp, pair = tile_cls // MOE_PAIRS, tile_cls % MOE_PAIRS
    pair_lo = [a for a in range(MOE_EXPERTS_PER_GROUP) for _ in range(a + 1, MOE_EXPERTS_PER_GROUP)]
    pair_hi = [b for a in range(MOE_EXPERTS_PER_GROUP) for b in range(a + 1, MOE_EXPERTS_PER_GROUP)]
    pid = jnp.arange(MOE_PAIRS, dtype=jnp.int32)
    k_lo = jnp.sum(jnp.where(pair[:, None] == pid, jnp.asarray(pair_lo, jnp.int32), 0), axis=1)
    k_hi = jnp.sum(jnp.where(pair[:, None] == pid, jnp.asarray(pair_hi, jnp.int32), 0), axis=1)
    ea = (grp * MOE_EXPERTS_PER_GROUP + k_lo).astype(jnp.int32)
    eb = (grp * MOE_EXPERTS_PER_GROUP + k_hi).astype(jnp.int32)
    n_used = tile_end[MOE_CLASSES - 1].reshape(1).astype(jnp.int32)

    n_slots = max_tiles * MOE_TILE
    hbm = pl.BlockSpec(memory_space=pl.ANY)
    sorted_pay = pl.pallas_call(
        functools.partial(_dispatch_kernel, rows=tm),
        grid_spec=pltpu.PrefetchScalarGridSpec(
            num_scalar_prefetch=1, grid=(n_tiles,),
            in_specs=[pl.BlockSpec((tm, PAY_WIDTH), lambda i, p: (i, 0)), hbm], out_specs=hbm,
            scratch_shapes=[pltpu.SemaphoreType.DMA(())]),
        out_shape=jax.ShapeDtypeStruct((n_slots, PAY_WIDTH), jnp.int32),
        input_output_aliases={2: 0},
        compiler_params=_cparams(("arbitrary",)),
        name="moe_dispatch",
    )(pos, pay, jnp.zeros((n_slots, PAY_WIDTH), jnp.int32))

    epg, hid = w1.shape[2], w1.shape[-1]

    def wspec(shape, which):
        def imap(j, ea_ref, eb_ref, nt_ref):
            e = (ea_ref, eb_ref)[which][j]
            return (layer, e // epg, e % epg, 0, 0)
        return pl.BlockSpec((None, None, None) + shape, imap)

    y_sorted = pl.pallas_call(
        _pair_ffn_kernel,
        grid_spec=pltpu.PrefetchScalarGridSpec(
            num_scalar_prefetch=3, grid=(max_tiles,),
            in_specs=[pl.BlockSpec((MOE_TILE, PAY_WIDTH), lambda j, *_: (j, 0)),
                      wspec((d, hid), 0), wspec((d, hid), 0), wspec((hid, d), 0),
                      wspec((d, hid), 1), wspec((d, hid), 1), wspec((hid, d), 1)],
            out_specs=pl.BlockSpec((MOE_TILE, PAY_WORDS), lambda j, *_: (j, 0))),
        out_shape=jax.ShapeDtypeStruct((n_slots, PAY_WORDS), jnp.int32),
        compiler_params=_cparams(("arbitrary",)),
        name="moe_pair_ffn",
    )(ea, eb, n_used, sorted_pay, w1, w3, w2, w1, w3, w2)

    return pl.pallas_call(
        functools.partial(_undispatch_kernel, final=final),
        grid_spec=pltpu.PrefetchScalarGridSpec(
            num_scalar_prefetch=1, grid=(n_tiles,),
            in_specs=[hbm,
                      pl.BlockSpec((tm, d), lambda i, p: (i, 0)),
                      pl.BlockSpec((None, 1, modtab.shape[-1]),
                                   lambda i, p: (jnp.where(i < n_lat_tiles, i // tpb, nb), 0, 0)),
                      pl.BlockSpec((1, d), lambda i, p: (0, 0))],
            out_specs=pl.BlockSpec((tm, d), lambda i, p: (i, 0)),
            scratch_shapes=[pltpu.VMEM((tm, PAY_WORDS), jnp.int32), pltpu.SemaphoreType.DMA(())]),
        out_shape=jax.ShapeDtypeStruct((n_rows, d), F32),
        compiler_params=_cparams(("arbitrary",)),
        name="moe_undispatch",
    )(pos, y_sorted, xs, modtab, final_g.reshape(1, d).astype(F32))


def rope_tables(seq, pad_rows):
    t = jnp.arange(seq, dtype=jnp.int32)
    row = (t // GRID_W).astype(F32)
    colp = (t % GRID_W).astype(F32)
    half = GLA_DK // 2
    inv_freq = ROPE_BASE ** (-jnp.arange(0, half, 2, dtype=F32) / half)
    ang_r = row[:, None] * inv_freq
    ang_c = colp[:, None] * inv_freq
    dd = jnp.arange(GLA_DK)
    ang = jnp.where((dd < half)[None, :], ang_r[:, dd % (half // 2)], ang_c[:, dd % (half // 2)])
    sign = jnp.where((dd % half) < half // 2, -1.0, 1.0).astype(F32)
    cos = jnp.tile(jnp.cos(ang), (1, GLA_HEADS))
    sin = jnp.tile(jnp.sin(ang) * sign[None, :], (1, GLA_HEADS))
    cos = jnp.concatenate([cos, jnp.ones((pad_rows, cos.shape[1]), F32)], axis=0)
    sin = jnp.concatenate([sin, jnp.zeros((pad_rows, sin.shape[1]), F32)], axis=0)
    return cos, sin


def split_in_weights(w_in, d):
    widths = (S5_WIDTH, GLA_HEADS * GLA_DK, GLA_HEADS * GLA_DK, GLA_HEADS * GLA_DV, GLA_HEADS * GLA_DV,
              2 * GLA_GATE_RANK, NA_HEADS * NA_HEAD_DIM, NA_HEADS * NA_HEAD_DIM, NA_HEADS * NA_HEAD_DIM,
              2 * CONV_WIDTH, N_BRANCHES * d)
    names = ('u', 'gq', 'gk', 'gv', 'gr', 'ga', 'nq', 'nk', 'nv', 'cv', 'gt')
    parts = {}
    col = 0
    for nme, w in zip(names, widths):
        parts[nme] = w_in[:, col:col + w]
        col += w
    wqk = jnp.concatenate([parts['gq'] * (GLA_DK ** -0.5), parts['gk']], axis=1)
    ga = jnp.concatenate([parts['ga']] * 3 + [jnp.zeros((d, LANES - 6 * GLA_GATE_RANK), w_in.dtype)], axis=1)
    wmisc = jnp.concatenate([parts['u'], parts['nq'] * (NA_HEAD_DIM ** -0.5), parts['nk'], parts['nv'],
                             parts['gv'], parts['gr'], parts['cv'], ga], axis=1)
    return wqk.astype(BF16), wmisc.astype(BF16), parts['gt'].astype(BF16)


def kernel(x, c, ctx, c_ctx, norm1_g, norm2_g, w_mod, b_mod, w_in, gate_b, w_mix_out, s5_lam_re, s5_lam_im, s5_log_dt, s5_b_re, s5_b_im, s5_c_re, s5_c_im, s5_d, s5_w_glu, s5_b_glu, s5_w_out, gla_w_a2, gla_b_a, gla_norm_g, gla_w_out, na_rpb, na_w_out, conv_dw, conv_dw_b, conv_ln_g, conv_ln_b, conv_w_out, moe_w_group, moe_b_group, moe_w_expert, moe_b_expert, moe_w1, moe_w3, moe_w2, final_norm_g):
    nb, seq, d = x.shape
    ctx_len = ctx.shape[1]
    depth = w_mod.shape[0]
    n_lat = nb * seq
    tm = 512
    tm_moe = math.gcd(1024, nb * ctx_len)
    assert ctx_len == SEQ_BLOCK and seq % tm_moe == 0 and (nb * ctx_len) % tm_moe == 0 and nb < MOD_ROWS

    n_all = n_lat + nb * ctx_len
    x_lat, x_ctx, ctx_tile0 = x.reshape(n_lat, d).astype(F32), ctx.reshape(nb * ctx_len, d).astype(F32), 0
    c_rows = jnp.zeros((MOD_ROWS, d), F32).at[:nb].set(c.astype(F32)).at[nb].set(c_ctx.astype(F32))
    modtab = modulation_table(c_rows, w_mod.astype(F32), b_mod.astype(F32))
    modtab = modtab.reshape(depth, MOD_ROWS, 1, 6 * d)
    cos_tab, sin_tab = rope_tables(seq, tm)

    moe_w1_bf, moe_w3_bf, moe_w2_bf = moe_w1.astype(BF16), moe_w3.astype(BF16), moe_w2.astype(BF16)

    for i in range(depth):
        last = i == depth - 1
        n_rows = n_lat if last else n_all
        wqk, wmisc, wgate = split_in_weights(w_in[i], d)
        gq, gk, ua, ub, nq, nk, nv, gv, gr, cv, ga, gt = in_projection(
            x_lat, x_ctx, ctx_tile0, modtab[i], norm1_g[i].astype(F32), cos_tab, sin_tab, wqk, wmisc, wgate,
            n=n_all, nb=nb, seq=seq, tm=tm)

        mats = s5_matrices(s5_lam_re[i], s5_lam_im[i], s5_log_dt[i], s5_b_re[i], s5_b_im[i],
                           s5_c_re[i], s5_c_im[i], s5_d[i], nb)
        ys5a, ys5b = s5_mixer(ua, ub, mats, nb=nb, seq=seq, ctx_len=ctx_len)

        hk = GLA_HEADS * GLA_DK
        zero = jnp.zeros((GLA_GATE_RANK, hk), F32)
        wd = jnp.concatenate([jnp.concatenate([gla_w_a2[i, 0].astype(F32), zero], axis=1),
                              jnp.concatenate([zero, gla_w_a2[i, 1].astype(F32)], axis=1)], axis=0)
        wd_head, wd_rem = _split_bf16(wd)
        wa = jnp.concatenate([wd_head, wd_head, wd_rem,
                              jnp.zeros((LANES - 6 * GLA_GATE_RANK, 2 * hk), BF16)], axis=0)
        ba = gla_b_a[i].astype(F32).reshape(1, 2 * hk)
        o_f, o_r = gla_scan(gq, gk, gv, ga, wa, ba, nb=nb, seq=seq, ctx_len=ctx_len)

        o_na = neighbourhood_attention(nq, nk, nv, na_bias_tables(na_rpb[i]), nb=nb, seq=seq, ctx_len=ctx_len)
        y_cv = conv_branch(cv, conv_dw[i], conv_dw_b[i], conv_ln_g[i], conv_ln_b[i], nb=nb, seq=seq)

        weights = (gate_b[i].astype(F32).reshape(1, N_BRANCHES * d), s5_w_glu[i].astype(BF16),
                   s5_b_glu[i].astype(F32).reshape(1, S5_WIDTH), s5_w_out[i].astype(BF16),
                   gla_norm_g[i].astype(F32).reshape(1, GLA_DV), gla_w_out[i].astype(BF16),
                   na_w_out[i].astype(BF16), conv_w_out[i].astype(BF16), w_mix_out[i].astype(BF16))
        xs = merge_branches(x_lat, x_ctx, ctx_tile0, modtab[i], gt, ys5a, ys5b, o_f, o_r, gr, o_na, y_cv, weights,
                            n_rows=n_rows, nb=nb, seq=seq, tm=tm)

        n_router = MOE_GROUPS + N_EXPERTS
        w_router = jnp.pad(jnp.concatenate([moe_w_group[i], moe_w_expert[i]], axis=1).astype(F32),
                           ((0, 0), (0, LANES - n_router)))
        w_router = jnp.concatenate(_split_bf16(w_router), axis=1)
        b_router = jnp.pad(jnp.concatenate([moe_b_group[i], moe_b_expert[i]]).astype(F32),
                           (0, LANES - n_router)).reshape(1, LANES)
        xs = moe_dispatch_layer(xs, modtab[i], norm2_g[i].astype(F32), w_router, b_router,
                       moe_w1_bf, moe_w3_bf, moe_w2_bf, final_norm_g, layer=i,
                       n_rows=n_rows, nb=nb, seq=seq, tm=tm_moe, final=last)
        x_lat, x_ctx, ctx_tile0 = xs, xs, n_lat // tm

    return xs.reshape(nb, seq, d).astype(x.dtype)
```

```python
import functools
import math

import jax
import jax.numpy as jnp
from jax import lax
from jax.experimental import pallas as pl
from jax.experimental.pallas import tpu as pltpu

F32 = jnp.float32
BF16 = jnp.bfloat16
HIGHEST = lax.Precision.HIGHEST

GRID_W = 64
NORM_EPS = 1e-6
N_BRANCHES = 4
S5_WIDTH = 256
S5_GROUP_SIZE = 16
S5_GROUPS = 16
S5_STATE = 64
GLA_HEADS = 4
GLA_DK = 64
GLA_DV = 128
GLA_GATE_RANK = 16
GLA_TAU = 16.0
GLA_CHUNK = 64
ROPE_BASE = 10000.0
NA_HEADS = 4
NA_HEAD_DIM = 64
NA_WIN_ROWS = 8
NA_WIN_COLS = 16
CONV_WIDTH = 256
CONV_KERNEL = 31
MOE_GROUPS = 4
MOE_EXPERTS_PER_GROUP = 8
MOE_HIDDEN = 256
N_EXPERTS = MOE_GROUPS * MOE_EXPERTS_PER_GROUP

LANES = 128
SUBLANES = 8
MOD_ROWS = 8
VMEM_LIMIT = 56 * 1024 * 1024
S5_CHUNK = 32
SEQ_BLOCK = 256
CONV_HALO = 16
NEG_BIG = -1e30


def _cparams(sem):
    return pltpu.CompilerParams(dimension_semantics=sem, vmem_limit_bytes=VMEM_LIMIT)


def _resident(shape):
    nd = len(shape)
    return pl.BlockSpec(shape, lambda *_: (0,) * nd, pipeline_mode=pl.Buffered(1))


def _sigmoid(x):
    return 0.5 * jnp.tanh(0.5 * x) + 0.5


def _silu(x):
    return x * _sigmoid(x)


def _mod_kernel(c_ref, w_ref, b_ref, o_ref):
    c = c_ref[...]
    o_ref[0] = jnp.dot(_silu(c), w_ref[0], preferred_element_type=F32, precision=HIGHEST) + b_ref[0]


def modulation_table(c_rows, w_mod, b_mod):
    depth, d, n6 = w_mod.shape
    tn = 1024
    return pl.pallas_call(
        _mod_kernel,
        grid=(depth, n6 // tn),
        in_specs=[
            pl.BlockSpec((MOD_ROWS, d), lambda l, j: (0, 0)),
            pl.BlockSpec((1, d, tn), lambda l, j: (l, 0, j)),
            pl.BlockSpec((1, 1, tn), lambda l, j: (l, 0, j)),
        ],
        out_specs=pl.BlockSpec((1, MOD_ROWS, tn), lambda l, j: (l, 0, j)),
        out_shape=jax.ShapeDtypeStruct((depth, MOD_ROWS, n6), F32),
        compiler_params=_cparams(("arbitrary", "arbitrary")),
        name="mod_table",
    )(c_rows, w_mod, b_mod.reshape(depth, 1, n6))


def _mod_row_spec(n_lat_tiles, tiles_per_batch, nb, width):
    def imap(i, *_):
        return (jnp.where(i < n_lat_tiles, i // tiles_per_batch, nb), 0, 0)
    return pl.BlockSpec((None, 1, width), imap)


def _modulated_norm(x, g, shift, scale):
    y = x * lax.rsqrt(jnp.mean(x * x, axis=-1, keepdims=True) + NORM_EPS)
    return (y * g) * (1.0 + scale) + shift


IN_CHUNK = 512


def _x_pair_specs(tm, d, n_lat_tiles, ctx_tile0):
    return [pl.BlockSpec((tm, d), lambda i: (jnp.minimum(i, n_lat_tiles - 1), 0)),
            pl.BlockSpec((tm, d), lambda i: (ctx_tile0 + jnp.maximum(i - n_lat_tiles, 0), 0))]


def _x_pair_tile(xa_ref, xb_ref, n_lat_tiles):
    return jnp.where(pl.program_id(0) < n_lat_tiles, xa_ref[...], xb_ref[...])


def _inproj_kernel(xa_ref, xb_ref, mod_ref, g_ref, cos_ref, sin_ref, wqk_ref, wmisc_ref, wgate_ref,
                   gq_ref, gk_ref, ua_ref, ub_ref, nq_ref, nk_ref, nv_ref, gv_ref, gr_ref, cv_ref, ga_ref, gt_ref,
                   *, n_lat_tiles):
    d = xa_ref.shape[1]
    mod = mod_ref[...]
    x = _x_pair_tile(xa_ref, xb_ref, n_lat_tiles)
    h = _modulated_norm(x, g_ref[...], mod[:, 0:d], mod[:, d:2 * d]).astype(BF16)
    cos = cos_ref[...]
    sin = sin_ref[...]
    hk = cos.shape[1]
    shift = GLA_DK // 4
    lane = lax.broadcasted_iota(jnp.int32, (1, hk), 1)
    partner_above = (lane % (2 * shift)) < shift
    for j, o_ref in enumerate((gq_ref, gk_ref)):
        y = jnp.dot(h, wqk_ref[:, j * hk:(j + 1) * hk], preferred_element_type=F32)
        partner = jnp.where(partner_above, pltpu.roll(y, hk - shift, axis=1), pltpu.roll(y, shift, axis=1))
        o_ref[...] = (y * cos + partner * sin).astype(o_ref.dtype)
    col = 0
    for o_ref in (ua_ref, ub_ref, nq_ref, nk_ref, nv_ref, gv_ref, gr_ref, cv_ref, ga_ref):
        w = o_ref.shape[1]
        o_ref[...] = jnp.dot(h, wmisc_ref[:, col:col + w], preferred_element_type=F32).astype(o_ref.dtype)
        col += w
    for j in range(gt_ref.shape[1] // IN_CHUNK):
        sl = slice(j * IN_CHUNK, (j + 1) * IN_CHUNK)
        gt_ref[:, sl] = jnp.dot(h, wgate_ref[:, sl], preferred_element_type=F32).astype(gt_ref.dtype)


def in_projection(x_lat, x_ctx, ctx_tile0, modtab, norm_g, cos_tab, sin_tab, wqk, wmisc, wgate, *, n, nb, seq, tm):
    d = x_lat.shape[1]
    n_lat_tiles = nb * seq // tm
    tpb = seq // tm
    widths = (256, 256, LANES, LANES, 256, 256, 256, 512, 512, 512, LANES, N_BRANCHES * d)
    dtypes = (BF16, BF16, F32, F32) + (BF16,) * 6 + (F32, BF16)

    def row(i):
        return (i, 0)

    def rope_row(i):
        return (jnp.where(i < n_lat_tiles, i % tpb, tpb), 0)

    return pl.pallas_call(
        functools.partial(_inproj_kernel, n_lat_tiles=n_lat_tiles),
        grid=(n // tm,),
        in_specs=_x_pair_specs(tm, d, n_lat_tiles, ctx_tile0) + [
            _mod_row_spec(n_lat_tiles, tpb, nb, modtab.shape[-1]),
            _resident((1, d)),
            pl.BlockSpec((tm, 256), rope_row),
            pl.BlockSpec((tm, 256), rope_row),
            _resident(wqk.shape),
            _resident(wmisc.shape),
            _resident(wgate.shape),
        ],
        out_specs=[pl.BlockSpec((tm, w), row) for w in widths],
        out_shape=[jax.ShapeDtypeStruct((n, w), dt) for w, dt in zip(widths, dtypes)],
        compiler_params=_cparams(("arbitrary",)),
        name="in_proj",
    )(x_lat, x_ctx, modtab, norm_g.reshape(1, d), cos_tab, sin_tab, wqk, wmisc, wgate)


S5_TAUS_PER_TILE = LANES // S5_GROUP_SIZE
S5_PERM = S5_TAUS_PER_TILE * S5_WIDTH


def _s5_state_kernel(uca_ref, ucb_ref, ula_ref, ulb_ref, perm_ref, w_ref, x_ref, s_ref):
    n_ctx, n_lat = uca_ref.shape[0] // S5_CHUNK, ula_ref.shape[0] // S5_CHUNK
    for v in range(S5_CHUNK // S5_TAUS_PER_TILE):
        pieces = []
        for w in range(S5_TAUS_PER_TILE):
            tau = v * S5_TAUS_PER_TILE + w
            for uc_ref, ul_ref in ((uca_ref, ula_ref), (ucb_ref, ulb_ref)):
                pieces.append(jnp.concatenate([uc_ref[pl.ds(tau, n_ctx, stride=S5_CHUNK), :],
                                               ul_ref[pl.ds(tau, n_lat, stride=S5_CHUNK), :]], axis=0))
        z = jnp.concatenate(pieces, axis=1).astype(BF16)
        xv = jnp.dot(z, perm_ref[...], preferred_element_type=F32).astype(BF16)
        for g in range(S5_GROUPS):
            col = g * S5_CHUNK * S5_GROUP_SIZE + v * LANES
            x_ref[0, :, col:col + LANES] = xv[:, g * LANES:(g + 1) * LANES]
    cw = S5_CHUNK * S5_GROUP_SIZE
    for g in range(S5_GROUPS):
        s = jnp.dot(x_ref[0, :, g * cw:(g + 1) * cw], w_ref[g], preferred_element_type=F32)
        s_ref[0, 0, g] = s[:, :2 * S5_STATE]
        s_ref[1, 0, g] = s[:, 2 * S5_STATE:]


def _s5_scan_kernel(s_ref, a_ref, h_ref, *, n_chunks, n_ctx_chunks):
    d = pl.program_id(0)
    nc = n_chunks
    n_pairs = s_ref.shape[1] // nc
    a1 = a_ref[0, 0]
    a2 = a_ref[0, 1]

    def step(s, h):
        fwd_row = s
        rev_row = jnp.where(s < n_ctx_chunks, n_ctx_chunks - 1 - s, nc - 1 - (s - n_ctx_chunks))
        rows = pl.ds(jnp.where(d == 0, fwd_row, rev_row), n_pairs, stride=nc)
        h_ref[0, rows, :] = h
        return h * a1 + pltpu.roll(h, S5_STATE, axis=1) * a2 + s_ref[0, rows, :]

    lax.fori_loop(0, nc, step, jnp.zeros((n_pairs, s_ref.shape[2]), F32))


def _s5_out_kernel(x_ref, h_ref, perm_ref, t_ref, v_ref, yca_ref, ycb_ref, yla_ref, ylb_ref, y_scr):
    cw = S5_CHUNK * S5_GROUP_SIZE
    n_ctx = yca_ref.shape[0] // S5_CHUNK
    for g in range(S5_GROUPS):
        y = jnp.dot(x_ref[0, :, g * cw:(g + 1) * cw], t_ref[g], preferred_element_type=F32)
        h = jnp.concatenate([h_ref[0, 0, g], h_ref[1, 0, g]], axis=1)
        y += jnp.dot(h.astype(BF16), v_ref[g], preferred_element_type=F32)
        y_scr[:, g * cw:(g + 1) * cw] = y.astype(BF16)
    for v in range(S5_CHUNK // S5_TAUS_PER_TILE):
        yv = jnp.concatenate([y_scr[:, g * cw + v * LANES:g * cw + (v + 1) * LANES] for g in range(S5_GROUPS)],
                             axis=1)
        zv = lax.dot_general(yv, perm_ref[...], (((1,), (1,)), ((), ())), preferred_element_type=F32)
        for w in range(S5_TAUS_PER_TILE):
            tau = v * S5_TAUS_PER_TILE + w
            for hf, (yc_ref, yl_ref) in enumerate(((yca_ref, yla_ref), (ycb_ref, ylb_ref))):
                col = w * S5_WIDTH + hf * LANES
                piece = zv[:, col:col + LANES]
                yc_ref[pl.ds(tau, n_ctx, stride=S5_CHUNK), :] = piece[:n_ctx]
                yl_ref[pl.ds(tau, piece.shape[0] - n_ctx, stride=S5_CHUNK), :] = piece[n_ctx:]


def s5_mixer(ua, ub, mats, *, nb, seq, ctx_len):
    t_sum, w_cat, v_cat, a12 = mats
    g = S5_GROUPS
    nc = (seq + ctx_len) // S5_CHUNK
    n_ctx_chunks = ctx_len // S5_CHUNK
    cw = S5_CHUNK * S5_GROUP_SIZE
    sl = 2 * S5_STATE
    ctx0 = nb * seq // ctx_len
    src = (lax.broadcasted_iota(jnp.int32, (S5_PERM, S5_PERM), 0))
    dst = (lax.broadcasted_iota(jnp.int32, (S5_PERM, S5_PERM), 1))
    src_as_dst = ((src % S5_WIDTH) // S5_GROUP_SIZE) * LANES + (src // S5_WIDTH) * S5_GROUP_SIZE + src % S5_GROUP_SIZE
    perm = (src_as_dst == dst).astype(BF16)
    x_gm, s = pl.pallas_call(
        _s5_state_kernel,
        grid=(nb,),
        in_specs=[pl.BlockSpec((ctx_len, LANES), lambda b: (ctx0 + b, 0))] * 2
        + [pl.BlockSpec((seq, LANES), lambda b: (b, 0))] * 2
        + [_resident(perm.shape), _resident(w_cat.shape)],
        out_specs=[pl.BlockSpec((1, nc, g * cw), lambda b: (b, 0, 0)),
                   pl.BlockSpec((2, 1, g, nc, sl), lambda b: (0, b, 0, 0, 0))],
        out_shape=[jax.ShapeDtypeStruct((nb, nc, g * cw), BF16), jax.ShapeDtypeStruct((2, nb, g, nc, sl), F32)],
        compiler_params=_cparams(("arbitrary",)),
        name="s5_chunk_state",
    )(ua, ub, ua, ub, perm, w_cat)
    h = pl.pallas_call(
        functools.partial(_s5_scan_kernel, n_chunks=nc, n_ctx_chunks=n_ctx_chunks),
        grid=(2,),
        in_specs=[pl.BlockSpec((1, nb * g * nc, sl), lambda d: (d, 0, 0)),
                  pl.BlockSpec((1, 2, nb * g, sl), lambda d: (d, 0, 0, 0))],
        out_specs=pl.BlockSpec((1, nb * g * nc, sl), lambda d: (d, 0, 0)),
        out_shape=jax.ShapeDtypeStruct((2, nb * g * nc, sl), F32),
        compiler_params=_cparams(("arbitrary",)),
        name="s5_chunk_scan",
    )(s.reshape(2, nb * g * nc, sl), a12).reshape(2, nb, g, nc, sl)
    yca, ycb, yla, ylb = pl.pallas_call(
        _s5_out_kernel,
        grid=(nb,),
        in_specs=[pl.BlockSpec((1, nc, g * cw), lambda b: (b, 0, 0), pipeline_mode=pl.Buffered(1)),
                  pl.BlockSpec((2, 1, g, nc, sl), lambda b: (0, b, 0, 0, 0), pipeline_mode=pl.Buffered(1)),
                  _resident(perm.shape), _resident(t_sum.shape), _resident(v_cat.shape)],
        out_specs=[pl.BlockSpec((ctx_len, LANES), lambda b: (b, 0))] * 2
        + [pl.BlockSpec((seq, LANES), lambda b: (b, 0))] * 2,
        out_shape=[jax.ShapeDtypeStruct((nb * ctx_len, LANES), F32)] * 2
        + [jax.ShapeDtypeStruct((nb * seq, LANES), F32)] * 2,
        scratch_shapes=[pltpu.VMEM((nc, g * cw), BF16)],
        compiler_params=_cparams(("arbitrary",)),
        name="s5_readout",
    )(x_gm, h, perm, t_sum, v_cat)
    return (yla, yca), (ylb, ycb)


def s5_matrices(lam_re, lam_im, log_dt, b_re, b_im, c_re, c_im, d_skip, nb):
    ch = S5_CHUNK
    gsz = S5_GROUP_SIZE
    dt = jnp.exp(log_dt.astype(F32))[..., None]
    lr = lam_re.astype(F32)
    li = lam_im.astype(F32)

    def power(n):
        n = n.astype(F32)[:, None, None, None]
        mag = jnp.exp(lr * dt * n)
        return mag * jnp.cos(li * dt * n), mag * jnp.sin(li * dt * n)

    ab_re, ab_im = power(jnp.ones((1,), F32))
    ab_re, ab_im = ab_re[0], ab_im[0]
    den = lr * lr + li * li
    nr = ab_re - 1.0
    ni = ab_im
    coef_re = (nr * lr + ni * li) / den
    coef_im = (ni * lr - nr * li) / den
    br = b_re.astype(F32)
    bi = b_im.astype(F32)
    bb_re = coef_re[..., None] * br - coef_im[..., None] * bi
    bb_im = coef_re[..., None] * bi + coef_im[..., None] * br
    cr = c_re.astype(F32)
    ci = c_im.astype(F32)

    p_re, p_im = power(jnp.arange(ch + 1))
    ca_re = cr[None] * p_re[:, :, :, None, :] - ci[None] * p_im[:, :, :, None, :]
    ca_im = cr[None] * p_im[:, :, :, None, :] + ci[None] * p_re[:, :, :, None, :]
    kmat = jnp.einsum('ndgip,dgpj->ndgij', jnp.concatenate([ca_re[:ch], -ca_im[:ch]], axis=-1),
                      jnp.concatenate([bb_re, bb_im], axis=-2), precision=HIGHEST)
    lags = jnp.concatenate([jnp.flip(kmat[1:, 1], axis=0), kmat[:1, 0] + kmat[:1, 1], kmat[1:, 0]], axis=0)
    lag_rows = jnp.transpose(lags, (1, 3, 0, 2)).reshape(S5_GROUPS, gsz, (2 * ch - 1) * gsz)
    t_sum = jnp.concatenate([lag_rows[:, :, (ch - 1 - s) * gsz:(2 * ch - 1 - s) * gsz] for s in range(ch)], axis=1)
    skip = jnp.eye(ch * gsz, dtype=F32)[None] * jnp.tile(d_skip.astype(F32).reshape(S5_GROUPS, 1, gsz), (1, ch, 1)).reshape(S5_GROUPS, 1, ch * gsz)
    t_sum = t_sum + skip

    def w_dir(d, pr, pi):
        wr = pr[..., None] * bb_re[d][None] - pi[..., None] * bb_im[d][None]
        wi = pr[..., None] * bb_im[d][None] + pi[..., None] * bb_re[d][None]
        w = jnp.concatenate([wr, wi], axis=2)
        return jnp.transpose(w, (1, 0, 3, 2)).reshape(S5_GROUPS, ch * gsz, 2 * S5_STATE)
    w_cat = jnp.concatenate([w_dir(0, jnp.flip(p_re[:ch, 0], 0), jnp.flip(p_im[:ch, 0], 0)),
                             w_dir(1, p_re[:ch, 1], p_im[:ch, 1])], axis=-1)

    def v_dir(vr, vi):
        v = jnp.concatenate([vr, vi], axis=-1)
        return jnp.transpose(v, (1, 3, 0, 2)).reshape(S5_GROUPS, 2 * S5_STATE, ch * gsz)
    v_cat = jnp.concatenate([v_dir(ca_re[1:, 0], -ca_im[1:, 0]),
                             v_dir(jnp.flip(ca_re[1:, 1], 0), -jnp.flip(ca_im[1:, 1], 0))], axis=1)

    a1 = jnp.concatenate([p_re[ch], p_re[ch]], axis=-1)
    a2 = jnp.concatenate([-p_im[ch], p_im[ch]], axis=-1)
    a12 = jnp.stack([a1, a2], axis=1)
    a12 = jnp.tile(a12, (1, 1, nb, 1))
    return t_sum.astype(BF16), w_cat.astype(BF16), v_cat.astype(BF16), a12


def _split_bf16(x):
    head = x.astype(BF16)
    return head, (x - head.astype(F32)).astype(BF16)


def _gla_direction(q_ref, k_ref, v_ref, a_ref, wa, ba, o_ref, st_ref, reverse):
    c = GLA_CHUNK
    hk = GLA_HEADS * GLA_DK
    hv = GLA_HEADS * GLA_DV
    rows = q_ref.shape[0]
    nchunks = rows // c
    grank2 = 2 * GLA_GATE_RANK

    a_head, a_rem = _split_bf16(a_ref[...])
    lane = lax.broadcasted_iota(jnp.int32, a_head.shape, 1)
    a_pack = jnp.where(jnp.logical_and(lane >= grank2, lane < 2 * grank2), a_rem, a_head)
    z = jnp.dot(a_pack, wa, preferred_element_type=F32) + ba
    g = (jnp.minimum(z, 0.0) - jnp.log(1.0 + jnp.exp(-jnp.abs(z)))) * (1.0 / GLA_TAU)
    ri = lax.broadcasted_iota(jnp.int32, (rows, rows), 0)
    ci = lax.broadcasted_iota(jnp.int32, (rows, rows), 1)
    ordered = (ri <= ci) if reverse else (ri >= ci)
    tri_bd = jnp.logical_and(ri // c == ci // c, ordered).astype(BF16)
    g_head, g_rem = _split_bf16(g)
    bb = jnp.dot(tri_bd, jnp.concatenate([g_head, g_rem], axis=1), preferred_element_type=F32)
    b = bb[:, :hk] + bb[:, hk:]
    q_t = (q_ref[...].astype(F32) * jnp.exp(b)).astype(BF16)
    k = k_ref[...].astype(F32)
    k_t = (k * jnp.exp(-b)).astype(BF16)

    hrow = lax.broadcasted_iota(jnp.int32, (GLA_HEADS, hk), 0)
    kmask = (lax.broadcasted_iota(jnp.int32, (GLA_HEADS, hk), 1) // GLA_DK == hrow).astype(BF16)
    vmask = (lax.broadcasted_iota(jnp.int32, (GLA_HEADS, hv), 1) // GLA_DV
             == lax.broadcasted_iota(jnp.int32, (GLA_HEADS, hv), 0)).astype(BF16)
    qi = lax.broadcasted_iota(jnp.int32, (c, GLA_HEADS * c), 0)
    kj = lax.broadcasted_iota(jnp.int32, (c, GLA_HEADS * c), 1) % c
    causal = ((qi <= kj) if reverse else (qi >= kj)).astype(F32)
    st_mask = (lax.broadcasted_iota(jnp.int32, (hv, hk), 0) // GLA_DV
               == lax.broadcasted_iota(jnp.int32, (hv, hk), 1) // GLA_DK).astype(F32)

    order = range(nchunks - 1, -1, -1) if reverse else range(nchunks)
    st = st_ref[...]
    for j in order:
        sl = slice(j * c, (j + 1) * c)
        b_c = b[sl]
        b_last = b_c[0:1] if reverse else b_c[c - 1:c]
        v_c = v_ref[sl, :]
        k_bd = jnp.concatenate([k_t[sl] * kmask[h:h + 1] for h in range(GLA_HEADS)], axis=0)
        v_bd = jnp.concatenate([v_c * vmask[h:h + 1] for h in range(GLA_HEADS)], axis=0)
        att = lax.dot_general(q_t[sl], k_bd, (((1,), (1,)), ((), ())), preferred_element_type=F32) * causal
        o = jnp.dot(att.astype(BF16), v_bd, preferred_element_type=F32)
        o += lax.dot_general(q_t[sl], st.astype(BF16), (((1,), (1,)), ((), ())), preferred_element_type=F32)
        o_ref[sl, :] = o.astype(o_ref.dtype)
        k_end = (k[sl] * jnp.exp(b_last - b_c)).astype(BF16)
        kv_t = lax.dot_general(v_c, k_end, (((0,), (0,)), ((), ())), preferred_element_type=F32)
        st = jnp.exp(b_last) * st + kv_t * st_mask
    st_ref[...] = st


def _gla_kernel(qf_ref, kf_ref, vf_ref, af_ref, qr_ref, kr_ref, vr_ref, ar_ref, wa_ref, ba_ref,
                of_ref, or_ref, sf_ref, sr_ref):
    @pl.when(pl.program_id(1) == 0)
    def _():
        sf_ref[...] = jnp.zeros_like(sf_ref)
        sr_ref[...] = jnp.zeros_like(sr_ref)

    hk = GLA_HEADS * GLA_DK
    _gla_direction(qf_ref, kf_ref, vf_ref, af_ref, wa_ref[:, :hk], ba_ref[:, :hk], of_ref, sf_ref, False)
    _gla_direction(qr_ref, kr_ref, vr_ref, ar_ref, wa_ref[:, hk:], ba_ref[:, hk:], or_ref, sr_ref, True)


def gla_scan(gq, gk, gv, ga, wa, ba, *, nb, seq, ctx_len):
    n = gq.shape[0]
    blk = SEQ_BLOCK
    assert ctx_len == blk
    lpb = seq // blk
    ctx0 = nb * lpb

    def fwd(b, s):
        return (jnp.where(s == 0, ctx0 + b, b * lpb + s - 1), 0)

    def rev(b, s):
        return (jnp.where(s == 0, ctx0 + b, b * lpb + lpb - s), 0)

    hk = GLA_HEADS * GLA_DK
    hv = GLA_HEADS * GLA_DV
    specs = []
    for imap in (fwd, rev):
        specs += [pl.BlockSpec((blk, hk), imap), pl.BlockSpec((blk, hk), imap),
                  pl.BlockSpec((blk, hv), imap), pl.BlockSpec((blk, LANES), imap)]
    specs += [_resident(wa.shape), _resident(ba.shape)]
    return pl.pallas_call(
        _gla_kernel,
        grid=(nb, lpb + 1),
        in_specs=specs,
        out_specs=[pl.BlockSpec((blk, hv), fwd), pl.BlockSpec((blk, hv), rev)],
        out_shape=[jax.ShapeDtypeStruct((n, hv), BF16)] * 2,
        scratch_shapes=[pltpu.VMEM((hv, hk), F32), pltpu.VMEM((hv, hk), F32)],
        compiler_params=_cparams(("arbitrary", "arbitrary")),
        name="gla_scan",
    )(gq, gk, gv, ga, gq, gk, gv, ga, wa, ba)


NA_ROWS_PER_STEP = 4
NA_UNION_ROWS = NA_WIN_ROWS + NA_ROWS_PER_STEP - 1


def _na_kernel(q_ref, k_ref, v_ref, kc_ref, vc_ref, bias_ref, o_ref, *, n_rows):
    step = pl.program_id(1)
    hd = NA_HEADS * NA_HEAD_DIM
    lane = lax.broadcasted_iota(jnp.int32, (NA_HEADS, hd), 1)
    hrow = lax.broadcasted_iota(jnp.int32, (NA_HEADS, hd), 0)
    head_mask = (lane // NA_HEAD_DIM == hrow).astype(F32)
    kc = kc_ref[...]
    vc = vc_ref[...]
    last = pl.num_programs(1) - 1
    regime = jnp.where(step == 0, 0, jnp.where(step == last, 2, 1))
    ws = jnp.clip(step * NA_ROWS_PER_STEP - NA_WIN_ROWS // 2, 0, n_rows - NA_UNION_ROWS)
    start = pl.multiple_of(ws * GRID_W, GRID_W)
    kw = k_ref[pl.ds(start, NA_UNION_ROWS * GRID_W), :]
    vw = v_ref[pl.ds(start, NA_UNION_ROWS * GRID_W), :]
    q = q_ref[...].astype(F32)
    acc = jnp.zeros(q.shape, F32)
    for h in range(NA_HEADS):
        m_h = head_mask[h:h + 1]
        qh = (q * m_h).astype(BF16)
        s_lat = lax.dot_general(qh, kw, (((1,), (1,)), ((), ())), preferred_element_type=F32)
        s_lat = s_lat + bias_ref[regime, h]
        s_ctx = lax.dot_general(qh, kc, (((1,), (1,)), ((), ())), preferred_element_type=F32)
        m = jnp.maximum(jnp.max(s_lat, axis=1, keepdims=True), jnp.max(s_ctx, axis=1, keepdims=True))
        p_lat = jnp.exp(s_lat - m)
        p_ctx = jnp.exp(s_ctx - m)
        den = jnp.sum(p_lat, axis=1, keepdims=True) + jnp.sum(p_ctx, axis=1, keepdims=True)
        o = jnp.dot(p_lat.astype(BF16), vw, preferred_element_type=F32)
        o += jnp.dot(p_ctx.astype(BF16), vc, preferred_element_type=F32)
        acc += (o / den) * m_h
    o_ref[...] = acc.astype(o_ref.dtype)


def _na_ctx_kernel(q_ref, k_ref, v_ref, o_ref):
    hd = NA_HEADS * NA_HEAD_DIM
    lane = lax.broadcasted_iota(jnp.int32, (NA_HEADS, hd), 1)
    hrow = lax.broadcasted_iota(jnp.int32, (NA_HEADS, hd), 0)
    head_mask = (lane // NA_HEAD_DIM == hrow).astype(F32)
    q = q_ref[...].astype(F32)
    k = k_ref[...]
    v = v_ref[...]
    acc = jnp.zeros(q.shape, F32)
    for h in range(NA_HEADS):
        m_h = head_mask[h:h + 1]
        s = lax.dot_general((q * m_h).astype(BF16), k, (((1,), (1,)), ((), ())), preferred_element_type=F32)
        p = jnp.exp(s - jnp.max(s, axis=1, keepdims=True))
        o = jnp.dot(p.astype(BF16), v, preferred_element_type=F32) / jnp.sum(p, axis=1, keepdims=True)
        acc += o * m_h
    o_ref[...] = acc.astype(o_ref.dtype)


def neighbourhood_attention(nq, nk, nv, bias, *, nb, seq, ctx_len):
    n, hd = nq.shape
    n_rows = seq // GRID_W
    qb = NA_ROWS_PER_STEP * GRID_W
    steps = seq // qb
    ctx0 = nb * seq // ctx_len
    o_lat = pl.pallas_call(
        functools.partial(_na_kernel, n_rows=n_rows),
        grid=(nb, steps),
        in_specs=[pl.BlockSpec((qb, hd), lambda b, s: (b * steps + s, 0)),
                  pl.BlockSpec((seq, hd), lambda b, s: (b, 0)),
                  pl.BlockSpec((seq, hd), lambda b, s: (b, 0)),
                  pl.BlockSpec((ctx_len, hd), lambda b, s: (ctx0 + b, 0)),
                  pl.BlockSpec((ctx_len, hd), lambda b, s: (ctx0 + b, 0)),
                  _resident(bias.shape)],
        out_specs=pl.BlockSpec((qb, hd), lambda b, s: (b * steps + s, 0)),
        out_shape=jax.ShapeDtypeStruct((nb * seq, hd), BF16),
        compiler_params=_cparams(("arbitrary", "arbitrary")),
        name="na_latent",
    )(nq, nk, nv, nk, nv, bias)
    o_ctx = pl.pallas_call(
        _na_ctx_kernel,
        grid=(nb,),
        in_specs=[pl.BlockSpec((ctx_len, hd), lambda b: (ctx0 + b, 0))] * 3,
        out_specs=pl.BlockSpec((ctx_len, hd), lambda b: (b, 0)),
        out_shape=jax.ShapeDtypeStruct((nb * ctx_len, hd), BF16),
        compiler_params=_cparams(("arbitrary",)),
        name="na_context",
    )(nq, nk, nv)
    return o_lat, o_ctx


def na_bias_tables(rpb):
    rpb = rpb.astype(F32)
    wr, nq, nu = NA_WIN_ROWS, NA_ROWS_PER_STEP, NA_UNION_ROWS
    assert nq <= wr // 2 + 1 and wr >= nq + wr // 2
    c_idx = jnp.arange(GRID_W)
    col_start = jnp.clip(c_idx - NA_WIN_COLS // 2, 0, GRID_W - NA_WIN_COLS)
    col_in = (c_idx[None, :] >= col_start[:, None]) & (c_idx[None, :] < col_start[:, None] + NA_WIN_COLS)
    wc = NA_WIN_COLS
    pad = jnp.zeros(rpb.shape[:-1] + (2 * GRID_W - (2 * wc - 1),), F32)
    table = jnp.concatenate([rpb[..., wc - 1:], pad, rpb[..., :wc - 1]], axis=-1)
    toe = jnp.tile(table, GRID_W)[..., :GRID_W * (2 * GRID_W - 1)]
    toe = toe.reshape(table.shape[:-1] + (GRID_W, 2 * GRID_W - 1))[..., :GRID_W]
    toe = jnp.transpose(jnp.where(col_in[None, None], toe, NEG_BIG), (0, 2, 1, 3))
    per_regime = []
    for lo_of, dr0_of in ((lambda i: 0, lambda i: wr - 1 - i),
                          (lambda i: i, lambda i: wr // 2 - 1),
                          (lambda i: nu - wr, lambda i: nu - wr - i)):
        rows = []
        for i in range(nq):
            lo, dr0 = lo_of(i), dr0_of(i)
            piece = toe[:, :, dr0:dr0 + wr]
            rows.append(jnp.pad(piece, ((0, 0), (0, 0), (lo, nu - wr - lo), (0, 0)), constant_values=NEG_BIG))
        per_regime.append(jnp.stack(rows, axis=1))
    b = jnp.stack(per_regime, axis=0)
    return b.reshape(3, NA_HEADS, nq * GRID_W, nu * GRID_W)


def _conv_kernel(prev_ref, main_ref, next_ref, dw_ref, dwb_ref, lng_ref, lnb_ref, o_ref, buf_ref, shift_ref,
                 *, n_lat_tiles, tiles_per_batch):
    i = pl.program_id(0)
    j = i % tiles_per_batch
    is_lat = i < n_lat_tiles
    has_prev = jnp.logical_and(is_lat, j > 0)
    has_next = jnp.logical_and(is_lat, j < tiles_per_batch - 1)
    cw = CONV_WIDTH
    tl = main_ref.shape[0]

    def glu(a):
        a = a.astype(F32)
        return a[:, :cw] * _sigmoid(a[:, cw:])

    buf_ref[0:CONV_HALO, :] = glu(prev_ref[...]) * has_prev.astype(F32)
    buf_ref[CONV_HALO:CONV_HALO + tl, :] = glu(main_ref[...])
    buf_ref[CONV_HALO + tl:, :] = glu(next_ref[...]) * has_next.astype(F32)
    dw = dw_ref[...]
    acc = jnp.zeros((tl, cw), F32) + dwb_ref[...]
    base = CONV_HALO - CONV_KERNEL // 2
    span = tl + 2 * CONV_HALO - SUBLANES
    for r in range(SUBLANES):
        shift_ref[r] = buf_ref[r:r + span, :]
    for k in range(CONV_KERNEL):
        q, r = divmod(base + k, SUBLANES)
        acc += shift_ref[r, q * SUBLANES:q * SUBLANES + tl, :] * dw[k:k + 1, :]
    mu = jnp.mean(acc, axis=-1, keepdims=True)
    xc = acc - mu
    y = xc * lax.rsqrt(jnp.mean(xc * xc, axis=-1, keepdims=True) + NORM_EPS)
    y = y * lng_ref[...] + lnb_ref[...]
    o_ref[...] = _silu(y).astype(o_ref.dtype)


def conv_branch(cv, dw, dw_b, ln_g, ln_b, *, nb, seq):
    n = cv.shape[0]
    tl = SEQ_BLOCK
    hb = tl // CONV_HALO
    n_tiles = n // tl
    cw = CONV_WIDTH
    return pl.pallas_call(
        functools.partial(_conv_kernel, n_lat_tiles=nb * seq // tl, tiles_per_batch=seq // tl),
        grid=(n_tiles,),
        in_specs=[pl.BlockSpec((CONV_HALO, 2 * cw), lambda i: (jnp.maximum(i * hb - 1, 0), 0)),
                  pl.BlockSpec((tl, 2 * cw), lambda i: (i, 0)),
                  pl.BlockSpec((CONV_HALO, 2 * cw), lambda i: (jnp.minimum((i + 1) * hb, n_tiles * hb - 1), 0)),
                  _resident((CONV_KERNEL, cw)), _resident((1, cw)), _resident((1, cw)), _resident((1, cw))],
        out_specs=pl.BlockSpec((tl, cw), lambda i: (i, 0)),
        out_shape=jax.ShapeDtypeStruct((n, cw), BF16),
        scratch_shapes=[pltpu.VMEM((tl + 2 * CONV_HALO, cw), F32),
                        pltpu.VMEM((SUBLANES, tl + 2 * CONV_HALO - SUBLANES, cw), F32)],
        compiler_params=_cparams(("arbitrary",)),
        name="conv_branch",
    )(cv, cv, cv, dw.astype(F32), dw_b.reshape(1, cw).astype(F32), ln_g.reshape(1, cw).astype(F32),
      ln_b.reshape(1, cw).astype(F32))


def _merge_kernel(xa_ref, xb_ref, ys5a_l_ref, ys5a_c_ref, ys5b_l_ref, ys5b_c_ref, na_l_ref, na_c_ref,
                  mod_ref, gt_ref, of_ref, or_ref, gr_ref, cv_ref,
                  gate_b_ref, wglu_ref, bglu_ref, ws5_ref, gng_ref, wgla_ref, wna_ref, wcv_ref, wmix_ref, o_ref,
                  *, n_lat_tiles):
    d = xa_ref.shape[1]
    ys5a = _x_pair_tile(ys5a_l_ref, ys5a_c_ref, n_lat_tiles)
    ys5b = _x_pair_tile(ys5b_l_ref, ys5b_c_ref, n_lat_tiles)
    na = _x_pair_tile(na_l_ref, na_c_ref, n_lat_tiles)
    z = jax.nn.gelu(jnp.concatenate([ys5a, ys5b], axis=1))
    z = z * _sigmoid(jnp.dot(z.astype(BF16), wglu_ref[...], preferred_element_type=F32) + bglu_ref[...])
    br_s5 = jnp.dot(z.astype(BF16), ws5_ref[...], preferred_element_type=F32)
    o = of_ref[...].astype(F32) + or_ref[...].astype(F32)
    r = _silu(gr_ref[...].astype(F32))
    parts = []
    for h in range(GLA_HEADS):
        oh = o[:, h * GLA_DV:(h + 1) * GLA_DV]
        oh = oh * lax.rsqrt(jnp.mean(oh * oh, axis=-1, keepdims=True) + NORM_EPS) * gng_ref[...]
        parts.append(oh * r[:, h * GLA_DV:(h + 1) * GLA_DV])
    y_gla = jnp.concatenate(parts, axis=1).astype(BF16)
    br_gla = jnp.dot(y_gla, wgla_ref[...], preferred_element_type=F32)
    br_na = jnp.dot(na, wna_ref[...], preferred_element_type=F32)
    br_cv = jnp.dot(cv_ref[...], wcv_ref[...], preferred_element_type=F32)
    merged = jnp.zeros((xa_ref.shape[0], d), F32)
    for i, br in enumerate((br_s5, br_gla, br_na, br_cv)):
        pre = gt_ref[:, i * d:(i + 1) * d] + gate_b_ref[:, i * d:(i + 1) * d].astype(BF16)
        gate = 0.5 * jnp.tanh(0.5 * pre).astype(F32) + 0.5
        merged += gate * br
    mix = jnp.dot(merged.astype(BF16), wmix_ref[...], preferred_element_type=F32)
    o_ref[...] = _x_pair_tile(xa_ref, xb_ref, n_lat_tiles) + mod_ref[:, 2 * d:3 * d] * mix


def merge_branches(x_lat, x_ctx, ctx_tile0, modtab, gt, ys5a, ys5b, o_f, o_r, gr, o_na, y_cv, weights,
                   *, n_rows, nb, seq, tm):
    d = x_lat.shape[1]
    n_lat_tiles = nb * seq // tm
    tpb = seq // tm

    def row(i):
        return (i, 0)

    pairs = (ys5a, ys5b, o_na)
    acts = (gt, o_f, o_r, gr, y_cv)
    pair_specs = []
    for lat, _ in pairs:
        pair_specs += _x_pair_specs(tm, lat.shape[1], n_lat_tiles, 0)
    return pl.pallas_call(
        functools.partial(_merge_kernel, n_lat_tiles=n_lat_tiles),
        grid=(n_rows // tm,),
        in_specs=_x_pair_specs(tm, d, n_lat_tiles, ctx_tile0) + pair_specs
        + [_mod_row_spec(n_lat_tiles, tpb, nb, modtab.shape[-1])]
        + [pl.BlockSpec((tm, a.shape[1]), row) for a in acts]
        + [_resident(w.shape) for w in weights],
        out_specs=pl.BlockSpec((tm, d), row),
        out_shape=jax.ShapeDtypeStruct((n_rows, d), F32),
        compiler_params=_cparams(("arbitrary",)),
        name="merge_mix",
    )(x_lat, x_ctx, *[a for p in pairs for a in p], modtab, *acts, *weights)


MOE_PAIRS = MOE_EXPERTS_PER_GROUP * (MOE_EXPERTS_PER_GROUP - 1) // 2
MOE_CLASSES = MOE_GROUPS * MOE_PAIRS
MOE_TILE = 256
PAY_WORDS = 512
PAY_WIDTH = PAY_WORDS + LANES
ROW_DMA_UNROLL = 8
HIGH_HALF = -65536
LOW_HALF = 65535


def _pack_bf16_pairs(a):
    w = a.shape[1] // 2
    hi = lax.bitcast_convert_type(a[:, :w].astype(BF16).astype(F32), jnp.int32)
    lo = lax.bitcast_convert_type(a[:, w:].astype(BF16).astype(F32), jnp.int32)
    return jnp.bitwise_or(jnp.bitwise_and(hi, HIGH_HALF), jnp.bitwise_and(jnp.right_shift(lo, 16), LOW_HALF))


def _unpack_bf16_pairs(words):
    hi = lax.bitcast_convert_type(jnp.bitwise_and(words, HIGH_HALF), F32)
    lo = lax.bitcast_convert_type(jnp.left_shift(words, 16), F32)
    return jnp.concatenate([hi, lo], axis=1)


def _route_pair_kernel(x_ref, mod_ref, g_ref, wr_ref, br_ref, ltri_ref, pay_ref, meta_ref, cnt_ref):
    d = x_ref.shape[1]
    mod = mod_ref[...]
    h = _modulated_norm(x_ref[...], g_ref[...], mod[:, 3 * d:4 * d], mod[:, 4 * d:5 * d])
    h_head, h_rem = _split_bf16(h)
    nl = br_ref.shape[1]
    both = jnp.dot(h_head, wr_ref[...], preferred_element_type=F32)
    logits = (both[:, :nl] + both[:, nl:] + jnp.dot(h_rem, wr_ref[:, :nl], preferred_element_type=F32)
              + br_ref[...])
    lane = lax.broadcasted_iota(jnp.int32, logits.shape, 1)
    big = jnp.int32(1 << 20)
    is_g = lane < MOE_GROUPS
    gl = jnp.where(is_g, logits, -jnp.inf)
    gmax = jnp.max(gl, axis=1, keepdims=True)
    gidx = jnp.min(jnp.where(gl == gmax, lane, big), axis=1, keepdims=True)
    group_p = 1.0 / jnp.sum(jnp.where(is_g, jnp.exp(logits - gmax), 0.0), axis=1, keepdims=True)
    first = MOE_GROUPS + gidx * MOE_EXPERTS_PER_GROUP
    in_group = jnp.logical_and(lane >= first, lane < first + MOE_EXPERTS_PER_GROUP)
    el = jnp.where(in_group, logits, -jnp.inf)
    v1 = jnp.max(el, axis=1, keepdims=True)
    i1 = jnp.min(jnp.where(el == v1, lane, big), axis=1, keepdims=True)
    el2 = jnp.where(lane == i1, -jnp.inf, el)
    v2 = jnp.max(el2, axis=1, keepdims=True)
    i2 = jnp.min(jnp.where(el2 == v2, lane, big), axis=1, keepdims=True)
    t = jnp.exp(v2 - v1)
    w1 = group_p / (1.0 + t)
    w2 = group_p * t / (1.0 + t)
    k_lo = jnp.minimum(i1, i2) - first
    k_hi = jnp.maximum(i1, i2) - first
    w_lo = jnp.where(i1 < i2, w1, w2)
    w_hi = jnp.where(i1 < i2, w2, w1)
    pair = k_lo * (MOE_EXPERTS_PER_GROUP - 1) - jnp.right_shift(k_lo * (k_lo - 1), 1) + k_hi - k_lo - 1
    cls = gidx * MOE_PAIRS + pair
    pay_ref[:, :PAY_WORDS] = _pack_bf16_pairs(h)
    weights = jnp.where(lane == 0, w_lo, jnp.where(lane == 1, w_hi, 0.0))
    pay_ref[:, PAY_WORDS:] = lax.bitcast_convert_type(weights, jnp.int32)
    onehot = (lane == cls).astype(BF16)
    before = jnp.dot(ltri_ref[...], onehot, preferred_element_type=F32)
    rank = jnp.sum(jnp.where(lane == cls, before, 0.0), axis=1, keepdims=True)
    packed = jnp.where(lane == 0, cls.astype(F32), jnp.where(lane == 1, rank, 0.0))
    meta_ref[0] = jnp.transpose(packed)[0:8, :]
    counts = jnp.sum(onehot.astype(F32), axis=0, keepdims=True)
    cnt_ref[0] = jnp.broadcast_to(counts, cnt_ref.shape[1:]).astype(jnp.int32)


def _row_copy(src_hbm, src_row, dst, dst_row, sem):
    return pltpu.make_async_copy(src_hbm.at[pl.ds(src_row, 1)], dst.at[pl.ds(dst_row, 1)], sem)


def _dispatch_kernel(pos_ref, pay_ref, zero_hbm, sorted_hbm, sem, *, rows):
    del zero_hbm
    tile = pl.program_id(0)

    def copy(i):
        return _row_copy(pay_ref, i, sorted_hbm, pos_ref[tile * rows + i], sem)

    def issue(blk, carry):
        for j in range(ROW_DMA_UNROLL):
            copy(blk * ROW_DMA_UNROLL + j).start(priority=j % 2)
        return carry

    def drain(i, carry):
        copy(i).wait()
        return carry

    lax.fori_loop(0, rows // ROW_DMA_UNROLL, issue, 0)
    lax.fori_loop(0, rows, drain, 0, unroll=ROW_DMA_UNROLL)


def _pair_ffn_kernel(ea_ref, eb_ref, nt_ref, x_ref, w1a_ref, w3a_ref, w2a_ref, w1b_ref, w3b_ref, w2b_ref, y_ref):
    del ea_ref, eb_ref
    j = pl.program_id(0)

    @pl.when(j < nt_ref[0])
    def _():
        x = _unpack_bf16_pairs(x_ref[:, :PAY_WORDS]).astype(BF16)
        wts = lax.bitcast_convert_type(x_ref[:, PAY_WORDS:], F32)

        def ffn(w1_ref, w3_ref, w2_ref, cw):
            a = jnp.dot(x, w1_ref[...], preferred_element_type=F32)
            b = jnp.dot(x, w3_ref[...], preferred_element_type=F32)
            return jnp.dot((_silu(a) * b * cw).astype(BF16), w2_ref[...], preferred_element_type=F32)

        y = ffn(w1a_ref, w3a_ref, w2a_ref, wts[:, 0:1]) + ffn(w1b_ref, w3b_ref, w2b_ref, wts[:, 1:2])
        y_ref[...] = _pack_bf16_pairs(y)

    @pl.when(j >= nt_ref[0])
    def _():
        y_ref[...] = jnp.zeros_like(y_ref)


def _undispatch_kernel(pos_ref, y_hbm, x_ref, mod_ref, fg_ref, o_ref, ybuf, sem, *, final):
    tile = pl.program_id(0)
    rows, d = x_ref.shape

    def copy(i):
        return _row_copy(y_hbm, pos_ref[tile * rows + i], ybuf, i, sem)

    def issue(blk, carry):
        for j in range(ROW_DMA_UNROLL):
            copy(blk * ROW_DMA_UNROLL + j).start(priority=j % 2)
        return carry

    def drain(i, carry):
        copy(i).wait()
        return carry

    lax.fori_loop(0, rows // ROW_DMA_UNROLL, issue, 0)
    lax.fori_loop(0, rows, drain, 0, unroll=ROW_DMA_UNROLL)
    y = x_ref[...] + mod_ref[:, 5 * d:6 * d] * _unpack_bf16_pairs(ybuf[...])
    if final:
        y = y * lax.rsqrt(jnp.mean(y * y, axis=-1, keepdims=True) + NORM_EPS) * fg_ref[...]
    o_ref[...] = y


def moe_dispatch_layer(xs, modtab, norm_g, w_router, b_router, w1, w3, w2, final_g,
                       *, layer, n_rows, nb, seq, tm, final):
    d = xs.shape[1]
    assert d == 2 * PAY_WORDS
    n_lat_tiles = nb * seq // tm
    tpb = seq // tm
    n_tiles = n_rows // tm
    ltri = (lax.broadcasted_iota(jnp.int32, (tm, tm), 0) > lax.broadcasted_iota(jnp.int32, (tm, tm), 1)).astype(BF16)
    mod_spec = _mod_row_spec(n_lat_tiles, tpb, nb, modtab.shape[-1])
    pay, meta, cnt = pl.pallas_call(
        _route_pair_kernel,
        grid=(n_tiles,),
        in_specs=[pl.BlockSpec((tm, d), lambda i: (i, 0)), mod_spec,
                  _resident((1, d)), _resident(w_router.shape), _resident(b_router.shape), _resident(ltri.shape)],
        out_specs=[pl.BlockSpec((tm, PAY_WIDTH), lambda i: (i, 0)),
                   pl.BlockSpec((1, 8, tm), lambda i: (i, 0, 0)), pl.BlockSpec((1, 8, LANES), lambda i: (i, 0, 0))],
        out_shape=[jax.ShapeDtypeStruct((n_rows, PAY_WIDTH), jnp.int32),
                   jax.ShapeDtypeStruct((n_tiles, 8, tm), F32), jax.ShapeDtypeStruct((n_tiles, 8, LANES), jnp.int32)],
        compiler_params=_cparams(("arbitrary",)),
        name="moe_route",
    )(xs, modtab, norm_g.reshape(1, d), w_router, b_router, ltri)

    max_tiles = n_rows // MOE_TILE + MOE_CLASSES
    cnt = cnt[:, 0, :]
    total = jnp.sum(cnt, axis=0)
    tiles_per_class = (total + MOE_TILE - 1) // MOE_TILE
    class_start = (jnp.cumsum(tiles_per_class) - tiles_per_class) * MOE_TILE
    base = class_start[None, :] + jnp.cumsum(cnt, axis=0) - cnt
    cls_tok = meta[:, 0, :].astype(jnp.int32)
    rank_tok = meta[:, 1, :].astype(jnp.int32)
    lanes = jnp.arange(LANES, dtype=jnp.int32)
    pos = jnp.sum(jnp.where(cls_tok[:, :, None] == lanes, base[:, None, :], 0), axis=-1) + rank_tok
    pos = pos.reshape(n_rows)
    tile_end = jnp.cumsum(tiles_per_class)
    tile_cls = jnp.sum((jnp.arange(max_tiles, dtype=jnp.int32)[:, None] >= tile_end[None, :MOE_CLASSES]), axis=1)
    tile_cls = jnp.minimum(tile_cls, MOE_CLASSES - 1)
    grp, pair = tile_cls // MOE_PAIRS, tile_cls % MOE_PAIRS
    pair_lo = [a for a in range(MOE_EXPERTS_PER_GROUP) for _ in range(a + 1, MOE_EXPERTS_PER_GROUP)]
    pair_hi = [b for a in range(MOE_EXPERTS_PER_GROUP) for b in range(a + 1, MOE_EXPERTS_PER_GROUP)]
    pid = jnp.arange(MOE_PAIRS, dtype=jnp.int32)
    k_lo = jnp.sum(jnp.where(pair[:, None] == pid, jnp.asarray(pair_lo, jnp.int32), 0), axis=1)
    k_hi = jnp.sum(jnp.where(pair[:, None] == pid, jnp.asarray(pair_hi, jnp.int32), 0), axis=1)
    ea = (grp * MOE_EXPERTS_PER_GROUP + k_lo).astype(jnp.int32)
    eb = (grp * MOE_EXPERTS_PER_GROUP + k_hi).astype(jnp.int32)
    n_used = tile_end[MOE_CLASSES - 1].reshape(1).astype(jnp.int32)

    n_slots = max_tiles * MOE_TILE
    hbm = pl.BlockSpec(memory_space=pl.ANY)
    sorted_pay = pl.pallas_call(
        functools.partial(_dispatch_kernel, rows=tm),
        grid_spec=pltpu.PrefetchScalarGridSpec(
            num_scalar_prefetch=1, grid=(n_tiles,),
            in_specs=[pl.BlockSpec((tm, PAY_WIDTH), lambda i, p: (i, 0)), hbm], out_specs=hbm,
            scratch_shapes=[pltpu.SemaphoreType.DMA(())]),
        out_shape=jax.ShapeDtypeStruct((n_slots, PAY_WIDTH), jnp.int32),
        input_output_aliases={2: 0},
        compiler_params=_cparams(("arbitrary",)),
        name="moe_dispatch",
    )(pos, pay, jnp.zeros((n_slots, PAY_WIDTH), jnp.int32))

    epg, hid = w1.shape[2], w1.shape[-1]

    def wspec(shape, which):
        def imap(j, ea_ref, eb_ref, nt_ref):
            e = (ea_ref, eb_ref)[which][j]
            return (layer, e // epg, e % epg, 0, 0)
        return pl.BlockSpec((None, None, None) + shape, imap)

    y_sorted = pl.pallas_call(
        _pair_ffn_kernel,
        grid_spec=pltpu.PrefetchScalarGridSpec(
            num_scalar_prefetch=3, grid=(max_tiles,),
            in_specs=[pl.BlockSpec((MOE_TILE, PAY_WIDTH), lambda j, *_: (j, 0)),
                      wspec((d, hid), 0), wspec((d, hid), 0), wspec((hid, d), 0),
                      wspec((d, hid), 1), wspec((d, hid), 1), wspec((hid, d), 1)],
            out_specs=pl.BlockSpec((MOE_TILE, PAY_WORDS), lambda j, *_: (j, 0))),
        out_shape=jax.ShapeDtypeStruct((n_slots, PAY_WORDS), jnp.int32),
        compiler_params=_cparams(("arbitrary",)),
        name="moe_pair_ffn",
    )(ea, eb, n_used, sorted_pay, w1, w3, w2, w1, w3, w2)

    return pl.pallas_call(
        functools.partial(_undispatch_kernel, final=final),
        grid_spec=pltpu.PrefetchScalarGridSpec(
            num_scalar_prefetch=1, grid=(n_tiles,),
            in_specs=[hbm,
                      pl.BlockSpec((tm, d), lambda i, p: (i, 0)),
                      pl.BlockSpec((None, 1, modtab.shape[-1]),
                                   lambda i, p: (jnp.where(i < n_lat_tiles, i // tpb, nb), 0, 0)),
                      pl.BlockSpec((1, d), lambda i, p: (0, 0))],
            out_specs=pl.BlockSpec((tm, d), lambda i, p: (i, 0)),
            scratch_shapes=[pltpu.VMEM((tm, PAY_WORDS), jnp.int32), pltpu.SemaphoreType.DMA(())]),
        out_shape=jax.ShapeDtypeStruct((n_rows, d), F32),
        compiler_params=_cparams(("arbitrary",)),
        name="moe_undispatch",
    )(pos, y_sorted, xs, modtab, final_g.reshape(1, d).astype(F32))


def rope_tables(seq, pad_rows):
    t = jnp.arange(seq, dtype=jnp.int32)
    row = (t // GRID_W).astype(F32)
    colp = (t % GRID_W).astype(F32)
    half = GLA_DK // 2
    inv_freq = ROPE_BASE ** (-jnp.arange(0, half, 2, dtype=F32) / half)
    ang_r = row[:, None] * inv_freq
    ang_c = colp[:, None] * inv_freq
    dd = jnp.arange(GLA_DK)
    ang = jnp.where((dd < half)[None, :], ang_r[:, dd % (half // 2)], ang_c[:, dd % (half // 2)])
    sign = jnp.where((dd % half) < half // 2, -1.0, 1.0).astype(F32)
    cos = jnp.tile(jnp.cos(ang), (1, GLA_HEADS))
    sin = jnp.tile(jnp.sin(ang) * sign[None, :], (1, GLA_HEADS))
    cos = jnp.concatenate([cos, jnp.ones((pad_rows, cos.shape[1]), F32)], axis=0)
    sin = jnp.concatenate([sin, jnp.zeros((pad_rows, sin.shape[1]), F32)], axis=0)
    return cos, sin


def split_in_weights(w_in, d):
    widths = (S5_WIDTH, GLA_HEADS * GLA_DK, GLA_HEADS * GLA_DK, GLA_HEADS * GLA_DV, GLA_HEADS * GLA_DV,
              2 * GLA_GATE_RANK, NA_HEADS * NA_HEAD_DIM, NA_HEADS * NA_HEAD_DIM, NA_HEADS * NA_HEAD_DIM,
              2 * CONV_WIDTH, N_BRANCHES * d)
    names = ('u', 'gq', 'gk', 'gv', 'gr', 'ga', 'nq', 'nk', 'nv', 'cv', 'gt')
    parts = {}
    col = 0
    for nme, w in zip(names, widths):
        parts[nme] = w_in[:, col:col + w]
        col += w
    wqk = jnp.concatenate([parts['gq'] * (GLA_DK ** -0.5), parts['gk']], axis=1)
    ga = jnp.concatenate([parts['ga']] * 3 + [jnp.zeros((d, LANES - 6 * GLA_GATE_RANK), w_in.dtype)], axis=1)
    wmisc = jnp.concatenate([parts['u'], parts['nq'] * (NA_HEAD_DIM ** -0.5), parts['nk'], parts['nv'],
                             parts['gv'], parts['gr'], parts['cv'], ga], axis=1)
    return wqk.astype(BF16), wmisc.astype(BF16), parts['gt'].astype(BF16)


def kernel(x, c, ctx, c_ctx, norm1_g, norm2_g, w_mod, b_mod, w_in, gate_b, w_mix_out, s5_lam_re, s5_lam_im, s5_log_dt, s5_b_re, s5_b_im, s5_c_re, s5_c_im, s5_d, s5_w_glu, s5_b_glu, s5_w_out, gla_w_a2, gla_b_a, gla_norm_g, gla_w_out, na_rpb, na_w_out, conv_dw, conv_dw_b, conv_ln_g, conv_ln_b, conv_w_out, moe_w_group, moe_b_group, moe_w_expert, moe_b_expert, moe_w1, moe_w3, moe_w2, final_norm_g):
    nb, seq, d = x.shape
    ctx_len = ctx.shape[1]
    depth = w_mod.shape[0]
    n_lat = nb * seq
    tm = 512
    tm_moe = math.gcd(1024, nb * ctx_len)
    assert ctx_len == SEQ_BLOCK and seq % tm_moe == 0 and (nb * ctx_len) % tm_moe == 0 and nb < MOD_ROWS

    n_all = n_lat + nb * ctx_len
    x_lat, x_ctx, ctx_tile0 = x.reshape(n_lat, d).astype(F32), ctx.reshape(nb * ctx_len, d).astype(F32), 0
    c_rows = jnp.zeros((MOD_ROWS, d), F32).at[:nb].set(c.astype(F32)).at[nb].set(c_ctx.astype(F32))
    modtab = modulation_table(c_rows, w_mod.astype(F32), b_mod.astype(F32))
    modtab = modtab.reshape(depth, MOD_ROWS, 1, 6 * d)
    cos_tab, sin_tab = rope_tables(seq, tm)

    moe_w1_bf, moe_w3_bf, moe_w2_bf = moe_w1.astype(BF16), moe_w3.astype(BF16), moe_w2.astype(BF16)

    for i in range(depth):
        last = i == depth - 1
        n_rows = n_lat if last else n_all
        wqk, wmisc, wgate = split_in_weights(w_in[i], d)
        gq, gk, ua, ub, nq, nk, nv, gv, gr, cv, ga, gt = in_projection(
            x_lat, x_ctx, ctx_tile0, modtab[i], norm1_g[i].astype(F32), cos_tab, sin_tab, wqk, wmisc, wgate,
            n=n_all, nb=nb, seq=seq, tm=tm)

        mats = s5_matrices(s5_lam_re[i], s5_lam_im[i], s5_log_dt[i], s5_b_re[i], s5_b_im[i],
                           s5_c_re[i], s5_c_im[i], s5_d[i], nb)
        ys5a, ys5b = s5_mixer(ua, ub, mats, nb=nb, seq=seq, ctx_len=ctx_len)

        hk = GLA_HEADS * GLA_DK
        zero = jnp.zeros((GLA_GATE_RANK, hk), F32)
        wd = jnp.concatenate([jnp.concatenate([gla_w_a2[i, 0].astype(F32), zero], axis=1),
                              jnp.concatenate([zero, gla_w_a2[i, 1].astype(F32)], axis=1)], axis=0)
        wd_head, wd_rem = _split_bf16(wd)
        wa = jnp.concatenate([wd_head, wd_head, wd_rem,
                              jnp.zeros((LANES - 6 * GLA_GATE_RANK, 2 * hk), BF16)], axis=0)
        ba = gla_b_a[i].astype(F32).reshape(1, 2 * hk)
        o_f, o_r = gla_scan(gq, gk, gv, ga, wa, ba, nb=nb, seq=seq, ctx_len=ctx_len)

        o_na = neighbourhood_attention(nq, nk, nv, na_bias_tables(na_rpb[i]), nb=nb, seq=seq, ctx_len=ctx_len)
        y_cv = conv_branch(cv, conv_dw[i], conv_dw_b[i], conv_ln_g[i], conv_ln_b[i], nb=nb, seq=seq)

        weights = (gate_b[i].astype(F32).reshape(1, N_BRANCHES * d), s5_w_glu[i].astype(BF16),
                   s5_b_glu[i].astype(F32).reshape(1, S5_WIDTH), s5_w_out[i].astype(BF16),
                   gla_norm_g[i].astype(F32).reshape(1, GLA_DV), gla_w_out[i].astype(BF16),
                   na_w_out[i].astype(BF16), conv_w_out[i].astype(BF16), w_mix_out[i].astype(BF16))
        xs = merge_branches(x_lat, x_ctx, ctx_tile0, modtab[i], gt, ys5a, ys5b, o_f, o_r, gr, o_na, y_cv, weights,
                            n_rows=n_rows, nb=nb, seq=seq, tm=tm)

        n_router = MOE_GROUPS + N_EXPERTS
        w_router = jnp.pad(jnp.concatenate([moe_w_group[i], moe_w_expert[i]], axis=1).astype(F32),
                           ((0, 0), (0, LANES - n_router)))
        w_router = jnp.concatenate(_split_bf16(w_router), axis=1)
        b_router = jnp.pad(jnp.concatenate([moe_b_group[i], moe_b_expert[i]]).astype(F32),
                           (0, LANES - n_router)).reshape(1, LANES)
        xs = moe_dispatch_layer(xs, modtab[i], norm2_g[i].astype(F32), w_router, b_router,
                       moe_w1_bf, moe_w3_bf, moe_w2_bf, final_norm_g, layer=i,
                       n_rows=n_rows, nb=nb, seq=seq, tm=tm_moe, final=last)
        x_lat, x_ctx, ctx_tile0 = xs, xs, n_lat // tm

    return xs.reshape(nb, seq, d).astype(x.dtype)
```
